```python
import jax, jax.numpy as jnp
from jax import lax
import numpy as np

D_MODEL = 1024
BATCH = 8
SEQ = 2048
DEPTH = 1

HEAD_DIM = 64
SCALE = HEAD_DIM ** -0.5
EPS = 1e-6
NEG = -1e30
DIL_PAIRS = ((128, 1), (512, 4), (2048, 16))
N_DIL_GROUPS = len(DIL_PAIRS)
HEADS_A = 8
WIDTH_A = N_DIL_GROUPS * HEADS_A * HEAD_DIM
OUT_A = HEADS_A * HEAD_DIM
ROT_DIM_A = HEAD_DIM // 4
THETA_PARTIAL = 500000.0
HEADS_B_Q = 8
HEADS_B_KV = 2
WIDTH_B_Q = HEADS_B_Q * HEAD_DIM
WIDTH_B_KV = HEADS_B_KV * HEAD_DIM
OUT_B = WIDTH_B_Q
Q_BLOCK = 128
GRID_W = 64
THETA_AXIAL = 10000.0
N_BRANCHES = 2
IN_COLS = 3 * WIDTH_A + WIDTH_B_Q + 2 * WIDTH_B_KV + N_BRANCHES * D_MODEL
N_EXPERTS = 32
TOP_K = 4
D_FF = 1024
SWIGLU_LIMIT = 7.0
SWIGLU_ALPHA = 1.702
EXPERT_BLOCK = 128

kernel_name = "hybrid_dilated_axial_gqa_moe_block"


def rmsnorm(x, g):
    xf = x.astype(jnp.float32)
    y = xf * lax.rsqrt(jnp.mean(xf * xf, axis=-1, keepdims=True) + EPS)
    return (y * g).astype(x.dtype)


def rope(x, pos, theta):
    half = x.shape[-1] // 2
    inv = theta ** (-(jnp.arange(half, dtype=jnp.float32) / half))
    ang = pos.astype(jnp.float32)[:, None] * inv[None, :]
    cos, sin = jnp.cos(ang)[:, None, :], jnp.sin(ang)[:, None, :]
    xf = x.astype(jnp.float32)
    x1, x2 = xf[..., :half], xf[..., half:]
    return jnp.concatenate([x1 * cos - x2 * sin, x2 * cos + x1 * sin], axis=-1).astype(x.dtype)


def partial_rope(x, pos):
    return jnp.concatenate([rope(x[..., :ROT_DIM_A], pos, THETA_PARTIAL), x[..., ROT_DIM_A:]], axis=-1)


def axial_rope(x, row, col):
    h = x.shape[-1] // 2
    return jnp.concatenate([rope(x[..., :h], row, THETA_AXIAL), rope(x[..., h:], col, THETA_AXIAL)], axis=-1)


def _pad_axis3(x, lo, hi):
    return jnp.pad(x, ((0, 0), (0, 0), (0, 0), (lo, hi), (0, 0)))


def dilated_window_attention(q, k, v, window, dil):
    b, s, h, e = q.shape
    half = window // (2 * dil)
    L = s // dil
    nb = -(-L // half)
    lp = nb * half
    pad = lp - L

    def to_sub(t):
        return t.reshape(b, L, dil, h, e).transpose(0, 2, 3, 1, 4)

    qs, ks, vs = to_sub(q), to_sub(k), to_sub(v)
    qb = _pad_axis3(qs, 0, pad).reshape(b, dil, h, nb, half, e)

    def windows(t):
        tp = _pad_axis3(t, half, pad + half).reshape(b, dil, h, nb + 2, half, e)
        return jnp.concatenate([tp[:, :, :, :-2], tp[:, :, :, 1:-1], tp[:, :, :, 2:]], axis=4)

    kw, vw = windows(ks), windows(vs)
    sc = jnp.einsum('bdhnqe,bdhnke->bdhnqk', qb, kw).astype(jnp.float32) * SCALE
    blk = jnp.arange(nb)[:, None, None]
    qi = blk * half + jnp.arange(half)[None, :, None]
    ki = (blk - 1) * half + jnp.arange(3 * half)[None, None, :]
    valid = (jnp.abs(ki - qi) <= half) & (ki >= 0) & (ki < L)
    sc = jnp.where(valid, sc, NEG)
    lse = jax.nn.logsumexp(sc, axis=-1)
    p = jnp.exp(sc - lse[..., None])
    o = jnp.einsum('bdhnqk,bdhnke->bdhnqe', p.astype(v.dtype), vw)
    o = o.reshape(b, dil, h, lp, e)[:, :, :, :L].transpose(0, 3, 1, 2, 4).reshape(b, s, h, e)
    lse = lse.reshape(b, dil, h, lp)[..., :L].transpose(0, 3, 1, 2).reshape(b, s, h)
    return o, lse


def gqa_blocked_attention(q, k, v):
    b, s, hq, e = q.shape
    hkv = k.shape[2]
    g = hq // hkv
    nq = s // Q_BLOCK
    qb = q.reshape(b, nq, Q_BLOCK, hkv, g, e).transpose(1, 0, 3, 4, 2, 5)
    kt = k.transpose(0, 2, 1, 3)
    vt = v.transpose(0, 2, 1, 3)

    def one_block(qblk):
        sc = jnp.einsum('bkgqe,bkse->bkgqs', qblk, kt).astype(jnp.float32) * SCALE
        p = jax.nn.softmax(sc, axis=-1)
        return jnp.einsum('bkgqs,bkse->bkgqe', p.astype(vt.dtype), vt)

    ob = lax.map(one_block, qb)
    return ob.transpose(1, 0, 4, 2, 3, 5).reshape(b, s, hq * e)


def moe_ffn(h, w_router, b_router, w_gate_up, b_gate_up, w_down, b_down):
    t, d = h.shape
    logits = (h @ w_router).astype(jnp.float32) + b_router
    topv, idx = lax.top_k(logits, TOP_K)
    gates = jax.nn.softmax(topv, axis=-1)
    a = t * TOP_K
    flat_e = idx.reshape(a)
    flat_w = gates.reshape(a)
    flat_tok = jnp.arange(a) // TOP_K
    order = jnp.argsort(flat_e)
    e_s, tok_s, w_s = flat_e[order], flat_tok[order], flat_w[order]
    counts = jnp.bincount(flat_e, length=N_EXPERTS)
    starts = jnp.cumsum(counts) - counts
    padded = ((counts + EXPERT_BLOCK - 1) // EXPERT_BLOCK) * EXPERT_BLOCK
    pends = jnp.cumsum(padded)
    pstarts = pends - padded
    dest = pstarts[e_s] + (jnp.arange(a) - starts[e_s])
    p_rows = a + N_EXPERTS * EXPERT_BLOCK
    n_blk = p_rows // EXPERT_BLOCK
    block_e = jnp.clip(jnp.searchsorted(pends, jnp.arange(n_blk) * EXPERT_BLOCK, side='right'), 0, N_EXPERTS - 1)
    xbuf = jnp.zeros((p_rows, d), h.dtype).at[dest].set(h[tok_s])
    wbuf = jnp.zeros((p_rows,), jnp.float32).at[dest].set(w_s)
    tbuf = jnp.zeros((p_rows,), jnp.int32).at[dest].set(tok_s.astype(jnp.int32))

    def expert_block(args):
        xblk, e = args
        gu = xblk @ w_gate_up[e] + b_gate_up[e]
        gate = jnp.minimum(gu[:, :D_FF], SWIGLU_LIMIT)
        up = jnp.clip(gu[:, D_FF:], -SWIGLU_LIMIT, SWIGLU_LIMIT)
        act = (up + 1.0) * (gate * jax.nn.sigmoid(SWIGLU_ALPHA * gate))
        return act @ w_down[e] + b_down[e]

    ybuf = lax.map(expert_block, (xbuf.reshape(n_blk, EXPERT_BLOCK, d), block_e)).reshape(p_rows, d)
    out = jnp.zeros((t, d), jnp.float32).at[tbuf].add(ybuf.astype(jnp.float32) * wbuf[:, None])
    return out


def setup_inputs(seed: int = 0) -> dict:
    key = jax.random.key(seed)
    ks = jax.random.split(key, 18)
    f32 = jnp.float32
    nrm = lambda k, shp: jax.random.normal(k, shp, f32)
    return {
        "x": nrm(ks[0], (BATCH, SEQ, D_MODEL)),
        "norm_mix_g": 1.0 + 0.02 * nrm(ks[1], (D_MODEL,)),
        "w_in": nrm(ks[2], (D_MODEL, IN_COLS)) * D_MODEL ** -0.5,
        "b_gate": 0.1 * nrm(ks[3], (N_BRANCHES, D_MODEL)),
        "qn_a": 1.0 + 0.02 * nrm(ks[4], (N_DIL_GROUPS, HEAD_DIM)),
        "kn_a": 1.0 + 0.02 * nrm(ks[5], (N_DIL_GROUPS, HEAD_DIM)),
        "qn_b": 1.0 + 0.02 * nrm(ks[6], (HEAD_DIM,)),
        "kn_b": 1.0 + 0.02 * nrm(ks[7], (HEAD_DIM,)),
        "w_proj_a": nrm(ks[8], (OUT_A, D_MODEL)) * OUT_A ** -0.5,
        "w_proj_b": nrm(ks[9], (OUT_B, D_MODEL)) * OUT_B ** -0.5,
        "w_out": nrm(ks[10], (D_MODEL, D_MODEL)) * D_MODEL ** -0.5,
        "norm_ffn_g": 1.0 + 0.02 * nrm(ks[11], (D_MODEL,)),
        "w_router": nrm(ks[12], (D_MODEL, N_EXPERTS)) * D_MODEL ** -0.5,
        "b_router": 0.01 * nrm(ks[13], (N_EXPERTS,)),
        "w_gate_up": nrm(ks[14], (N_EXPERTS, D_MODEL, 2 * D_FF)) * D_MODEL ** -0.5,
        "b_gate_up": 0.02 * nrm(ks[15], (N_EXPERTS, 2 * D_FF)),
        "w_down": nrm(ks[16], (N_EXPERTS, D_FF, D_MODEL)) * D_FF ** -0.5,
        "b_down": 0.02 * nrm(ks[17], (N_EXPERTS, D_MODEL)),
    }


def reference(x, norm_mix_g, w_in, b_gate, qn_a, kn_a, qn_b, kn_b, w_proj_a, w_proj_b, w_out,
              norm_ffn_g, w_router, b_router, w_gate_up, b_gate_up, w_down, b_down):
    b, s, d = x.shape
    pos = jnp.arange(s)
    rows = s // GRID_W
    row = jnp.repeat(jnp.arange(rows), GRID_W)
    col = jnp.tile(jnp.arange(GRID_W), rows)
    split_at = list(np.cumsum([3 * WIDTH_A, WIDTH_B_Q, WIDTH_B_KV, WIDTH_B_KV]))

    for _layer in range(DEPTH):
        h = rmsnorm(x, norm_mix_g)
        proj = h @ w_in
        qkv_a, q_b, k_b, v_b, gate_logits = jnp.split(proj, split_at, axis=-1)

        qkv_a = qkv_a.reshape(b, s, 3, N_DIL_GROUPS, HEADS_A, HEAD_DIM)
        outs, lses = [], []
        for gi, (window, dil) in enumerate(DIL_PAIRS):
            qg = partial_rope(rmsnorm(qkv_a[:, :, 0, gi], qn_a[gi]), pos)
            kg = partial_rope(rmsnorm(qkv_a[:, :, 1, gi], kn_a[gi]), pos)
            og, lg = dilated_window_attention(qg, kg, qkv_a[:, :, 2, gi], window, dil)
            outs.append(og)
            lses.append(lg)
        wts = jax.nn.softmax(jnp.stack(lses, axis=0), axis=0)
        o_a = jnp.sum(wts[..., None] * jnp.stack(outs, axis=0).astype(jnp.float32), axis=0)
        o_a = o_a.astype(x.dtype).reshape(b, s, OUT_A)

        qb = axial_rope(rmsnorm(q_b.reshape(b, s, HEADS_B_Q, HEAD_DIM), qn_b), row, col)
        kb = axial_rope(rmsnorm(k_b.reshape(b, s, HEADS_B_KV, HEAD_DIM), kn_b), row, col)
        vb = v_b.reshape(b, s, HEADS_B_KV, HEAD_DIM)
        o_b = gqa_blocked_attention(qb, kb, vb)

        gates = jax.nn.sigmoid(gate_logits.reshape(b, s, N_BRANCHES, d) + b_gate)
        merged = gates[:, :, 0] * (o_a @ w_proj_a) + gates[:, :, 1] * (o_b @ w_proj_b)
        x = x + (merged @ w_out).astype(x.dtype)

        h2 = rmsnorm(x, norm_ffn_g).reshape(b * s, d)
        ff = moe_ffn(h2, w_router, b_router, w_gate_up, b_gate_up, w_down, b_down)
        x = x + ff.reshape(b, s, d).astype(x.dtype)
    return x
```

```python
import functools

import jax
import jax.numpy as jnp
import numpy as np
from jax import lax
from jax.experimental import pallas as pl
from jax.experimental.pallas import tpu as pltpu

F32 = jnp.float32
BF16 = jnp.bfloat16

D_MODEL = 1024
HEAD_DIM = 64
SCALE = HEAD_DIM ** -0.5
EPS = 1e-6
NEG = -1e30
DIL_PAIRS = ((128, 1), (512, 4), (2048, 16))
HALF_WIN = 64
N_GROUPS = 3
HEADS_A = 8
OUT_A = HEADS_A * HEAD_DIM
WIDTH_A = N_GROUPS * OUT_A
ROT_DIM_A = 16
THETA_PARTIAL = 500000.0
HEADS_B_Q = 8
HEADS_B_KV = 2
WIDTH_B_Q = 512
WIDTH_B_KV = 128
GRID_W = 64
THETA_AXIAL = 10000.0
N_EXPERTS = 32
TOP_K = 4
D_FF = 1024
SWIGLU_LIMIT = 7.0
SWIGLU_ALPHA = 1.702

LANES = 128
MXU_N = 256
VMEM_LIMIT = 56 * 1024 * 1024

COL_QKA = 0
COL_VA = 2 * WIDTH_A
COL_QB = 3 * WIDTH_A
COL_KB = COL_QB + WIDTH_B_Q
COL_VB = COL_KB + WIDTH_B_KV
COL_GATE = COL_VB + WIDTH_B_KV


def _cparams(sem):
    return pltpu.CompilerParams(dimension_semantics=sem, vmem_limit_bytes=VMEM_LIMIT)


def _rope_tables(seq):
    pos = jnp.arange(seq, dtype=F32)[:, None]
    d = jnp.arange(LANES) % HEAD_DIM
    half = ROT_DIM_A // 2
    inv = THETA_PARTIAL ** (-((d % half).astype(F32) / half))
    ang = pos * inv[None, :]
    in_rot = (d < ROT_DIM_A)[None, :]
    first = (d < half)[None, :]
    second = ((d >= half) & (d < ROT_DIM_A))[None, :]
    ca = jnp.where(in_rot, jnp.cos(ang), 1.0)
    sa_up = jnp.where(first, -jnp.sin(ang), 0.0)
    sa_dn = jnp.where(second, jnp.sin(ang), 0.0)
    hb = HEAD_DIM // 4
    row = jnp.floor(pos / GRID_W)
    col = pos - row * GRID_W
    invb = THETA_AXIAL ** (-((d % hb).astype(F32) / hb))
    p2 = jnp.where((d < HEAD_DIM // 2)[None, :], row, col)
    angb = p2 * invb[None, :]
    firstb = ((d % (2 * hb)) < hb)[None, :]
    cb = jnp.cos(angb)
    sb_up = jnp.where(firstb, -jnp.sin(angb), 0.0)
    sb_dn = jnp.where(firstb, 0.0, jnp.sin(angb))
    return (ca, sa_up, sa_dn), (cb, sb_up, sb_dn)


def _inproj_kernel(x_ref, g_ref, w_ref, bd_ref, gqk_ref, gqb_ref, gkb_ref,
                   ca_ref, sau_ref, sad_ref, cb_ref, sbu_ref, sbd_ref,
                   qkva_ref, qb_ref, kvb_ref, gate_ref):
    x = x_ref[...]
    ms = jnp.mean(x * x, axis=-1, keepdims=True)
    h = (x * lax.rsqrt(ms + EPS) * g_ref[...]).astype(BF16)
    bd = bd_ref[...]
    lane = lax.broadcasted_iota(jnp.int32, (x.shape[0], LANES), 1)
    lo = lane < HEAD_DIM

    def proj(c0, width=MXU_N):
        return jnp.dot(h, w_ref[:, c0:c0 + width], preferred_element_type=F32)

    def head_norm(y, gain, bdm):
        ss = jnp.dot((y * y).astype(BF16), bdm, preferred_element_type=F32)
        return y * lax.rsqrt(ss * (1.0 / HEAD_DIM) + EPS) * gain

    def rope(z, c, su, sd, sh):
        return z * c + pltpu.roll(z, LANES - sh, 1) * su + pltpu.roll(z, sh, 1) * sd

    ca, sau, sad = ca_ref[...], sau_ref[...], sad_ref[...]
    cb, sbu, sbd = cb_ref[...], sbu_ref[...], sbd_ref[...]
    sh_a = ROT_DIM_A // 2
    sh_b = HEAD_DIM // 4

    for c in range(2 * WIDTH_A // MXU_N):
        c0 = c * MXU_N
        yn = head_norm(proj(COL_QKA + c0), gqk_ref[:, c0:c0 + MXU_N], bd)
        for hf in range(2):
            z = yn[:, hf * LANES:(hf + 1) * LANES]
            qkva_ref[:, c0 + hf * LANES:c0 + (hf + 1) * LANES] = rope(z, ca, sau, sad, sh_a).astype(BF16)
    for c in range(WIDTH_A // MXU_N):
        c0 = COL_VA + c * MXU_N
        qkva_ref[:, c0:c0 + MXU_N] = proj(c0).astype(BF16)
    for c in range(WIDTH_B_Q // MXU_N):
        c0 = c * MXU_N
        yn = head_norm(proj(COL_QB + c0), gqb_ref[:, c0:c0 + MXU_N], bd)
        for hf in range(2):
            z = yn[:, hf * LANES:(hf + 1) * LANES]
            qb_ref[:, c0 + hf * LANES:c0 + (hf + 1) * LANES] = rope(z, cb, sbu, sbd, sh_b).astype(BF16)
    ykv = proj(COL_KB)
    kb = rope(head_norm(ykv[:, :LANES], gkb_ref[...], bd[:LANES, :LANES]), cb, sbu, sbd, sh_b)
    vb = ykv[:, LANES:]
    for j, t in enumerate((kb, vb)):
        sw = pltpu.roll(t, HEAD_DIM, 1)
        kvb_ref[:, (2 * j) * LANES:(2 * j + 1) * LANES] = jnp.where(lo, t, sw).astype(BF16)
        kvb_ref[:, (2 * j + 1) * LANES:(2 * j + 2) * LANES] = jnp.where(lo, sw, t).astype(BF16)
    for c in range(2 * D_MODEL // MXU_N):
        c0 = c * MXU_N
        gate_ref[:, c0:c0 + MXU_N] = proj(COL_GATE + c0).astype(BF16)


def _inproj(x2, g, w_bf, bd, gqk, gqb, gkb, tabs_a, tabs_b, seq, tm):
    t = x2.shape[0]
    n_cols = w_bf.shape[1]
    npos = seq // tm
    row = lambda i: (i, 0)
    fixed = lambda i: (0, 0)
    tab = lambda i: (i % npos, 0)
    tab_spec = pl.BlockSpec((tm, LANES), tab)
    return pl.pallas_call(
        _inproj_kernel,
        grid=(t // tm,),
        in_specs=[
            pl.BlockSpec((tm, D_MODEL), row),
            pl.BlockSpec((1, D_MODEL), fixed),
            pl.BlockSpec((D_MODEL, n_cols), fixed, pipeline_mode=pl.Buffered(1)),
            pl.BlockSpec((MXU_N, MXU_N), fixed),
            pl.BlockSpec((1, 2 * WIDTH_A), fixed),
            pl.BlockSpec((1, WIDTH_B_Q), fixed),
            pl.BlockSpec((1, WIDTH_B_KV), fixed),
            tab_spec, tab_spec, tab_spec, tab_spec, tab_spec, tab_spec,
        ],
        out_specs=[
            pl.BlockSpec((tm, 3 * WIDTH_A), row),
            pl.BlockSpec((tm, WIDTH_B_Q), row),
            pl.BlockSpec((tm, 4 * LANES), row),
            pl.BlockSpec((tm, 2 * D_MODEL), row),
        ],
        out_shape=[
            jax.ShapeDtypeStruct((t, 3 * WIDTH_A), BF16),
            jax.ShapeDtypeStruct((t, WIDTH_B_Q), BF16),
            jax.ShapeDtypeStruct((t, 4 * LANES), BF16),
            jax.ShapeDtypeStruct((t, 2 * D_MODEL), BF16),
        ],
        compiler_params=_cparams(("arbitrary",)),
        name="inproj",
    )(x2, g, w_bf, bd, gqk, gqb, gkb, *tabs_a, *tabs_b)


QBLK = 128
NSUB = 4


def _mixa_kernel(q_ref, k_ref, v_ref, o_ref, lse_ref, *, seq_len, win, batched):
    step = pl.program_id(1)
    lane = lax.broadcasted_iota(jnp.int32, (QBLK, LANES), 1)
    lo = lane < HEAD_DIM
    qi = lax.broadcasted_iota(jnp.int32, (QBLK, win), 0)
    ki = lax.broadcasted_iota(jnp.int32, (QBLK, win), 1)
    for b in range(NSUB):
        if batched:
            sq, r0, blk = b, 0, 0
            kstart = 0
        else:
            sq, r0 = 0, b * QBLK
            blk = step * NSUB + b
            kstart = pl.multiple_of(jnp.clip(blk * QBLK - HALF_WIN, 0, seq_len - win), HALF_WIN)
        valid = jnp.abs((ki + kstart) - (qi + blk * QBLK)) <= HALF_WIN
        lse_acc = jnp.zeros((QBLK, LANES), F32)
        for p in range(HEADS_A // 2):
            cs = slice(p * LANES, (p + 1) * LANES)
            qp = q_ref[sq, r0:r0 + QBLK, cs]
            kp = k_ref[sq, pl.ds(kstart, win), cs]
            vp = v_ref[sq, pl.ds(kstart, win), cs]
            outs = []
            for hh in range(2):
                qh = jnp.where(lo if hh == 0 else jnp.logical_not(lo), qp, jnp.zeros_like(qp))
                s = lax.dot_general(qh, kp, (((1,), (1,)), ((), ())), preferred_element_type=F32)
                s = jnp.where(valid, s, NEG)
                m = jnp.max(s, axis=-1, keepdims=True)
                e = jnp.exp(s - m)
                l = jnp.sum(e, axis=-1, keepdims=True)
                o = jnp.dot(e.astype(BF16), vp, preferred_element_type=F32) * (1.0 / l)
                outs.append(o)
                lse_acc = jnp.where(lane == 2 * p + hh, m + jnp.log(l), lse_acc)
            o_ref[sq, r0:r0 + QBLK, cs] = jnp.where(lo, outs[0], outs[1]).astype(BF16)
        lse_ref[sq, r0:r0 + QBLK, :] = lse_acc


def _mixa(qkv, col_q, col_k, col_v, seq_len):
    n_seq = qkv.shape[0]
    win = min(2 * QBLK, seq_len)
    batched = seq_len == QBLK
    if batched:
        grid = (n_seq // NSUB, 1)
        qspec = lambda c: pl.BlockSpec((NSUB, QBLK, OUT_A), lambda s, i: (s, 0, c))
        kspec = qspec
        ospec = pl.BlockSpec((NSUB, QBLK, OUT_A), lambda s, i: (s, 0, 0))
        lspec = pl.BlockSpec((NSUB, QBLK, LANES), lambda s, i: (s, 0, 0))
    else:
        rows = NSUB * QBLK
        grid = (n_seq, seq_len // rows)
        qspec = lambda c: pl.BlockSpec((1, rows, OUT_A), lambda s, i: (s, i, c))
        kspec = lambda c: pl.BlockSpec((1, seq_len, OUT_A), lambda s, i: (s, 0, c))
        ospec = pl.BlockSpec((1, rows, OUT_A), lambda s, i: (s, i, 0))
        lspec = pl.BlockSpec((1, rows, LANES), lambda s, i: (s, i, 0))
    return pl.pallas_call(
        functools.partial(_mixa_kernel, seq_len=seq_len, win=win, batched=batched),
        grid=grid,
        in_specs=[qspec(col_q), kspec(col_k), kspec(col_v)],
        out_specs=[ospec, lspec],
        out_shape=[
            jax.ShapeDtypeStruct((n_seq, seq_len, OUT_A), BF16),
            jax.ShapeDtypeStruct((n_seq, seq_len, LANES), F32),
        ],
        compiler_params=_cparams(("arbitrary", "arbitrary")),
        name=f"mixa_len{seq_len}",
    )(qkv, qkv, qkv)


def _mixb_kernel(q_ref, k_ref, v_ref, o_ref):
    tq = q_ref.shape[0]
    lane = lax.broadcasted_iota(jnp.int32, (tq, LANES), 1)
    lo = lane < HEAD_DIM
    for p in range(HEADS_B_Q // 2):
        j = p // 2
        cs = slice(p * LANES, (p + 1) * LANES)
        qp = q_ref[:, cs]
        kd = k_ref[:, j * LANES:(j + 1) * LANES]
        vd = v_ref[:, j * LANES:(j + 1) * LANES]
        outs = []
        for hh in range(2):
            qh = jnp.where(lo if hh == 0 else jnp.logical_not(lo), qp, jnp.zeros_like(qp))
            s = lax.dot_general(qh, kd, (((1,), (1,)), ((), ())), preferred_element_type=F32)
            m = jnp.max(s, axis=-1, keepdims=True)
            e = jnp.exp(s - m)
            l = jnp.sum(e, axis=-1, keepdims=True)
            outs.append(jnp.dot(e.astype(BF16), vd, preferred_element_type=F32) * (1.0 / l))
        o_ref[:, cs] = jnp.where(lo, outs[0], outs[1]).astype(BF16)


def _mixb(qb, kvb, seq, tq):
    t = qb.shape[0]
    nq = seq // tq
    return pl.pallas_call(
        _mixb_kernel,
        grid=(t // seq, nq),
        in_specs=[
            pl.BlockSpec((tq, WIDTH_B_Q), lambda b, i: (b * nq + i, 0)),
            pl.BlockSpec((seq, 2 * LANES), lambda b, i: (b, 0)),
            pl.BlockSpec((seq, 2 * LANES), lambda b, i: (b, 1)),
        ],
        out_specs=pl.BlockSpec((tq, WIDTH_B_Q), lambda b, i: (b * nq + i, 0)),
        out_shape=jax.ShapeDtypeStruct((t, WIDTH_B_Q), BF16),
        compiler_params=_cparams(("arbitrary", "arbitrary")),
        name="mixb",
    )(qb, kvb, kvb)


def _merge_kernel(o1_ref, o2_ref, o3_ref, l1_ref, l2_ref, l3_ref, ob_ref, gl_ref, bg_ref, x_ref,
                  ex_ref, wpa_ref, wpb_ref, wo_ref, g2_ref, wr_ref, br_ref,
                  x1_ref, h2_ref, idx_ref, gw_ref):
    tm = x_ref.shape[0]
    lane = lax.broadcasted_iota(jnp.int32, (tm, LANES), 1)
    l1, l2, l3 = l1_ref[...], l2_ref[...], l3_ref[...]
    mx = jnp.maximum(jnp.maximum(l1, l2), l3)
    e1, e2, e3 = jnp.exp(l1 - mx), jnp.exp(l2 - mx), jnp.exp(l3 - mx)
    rden = 1.0 / (e1 + e2 + e3)
    ex = ex_ref[...]

    def expand(w):
        hi = w.astype(BF16)
        lo_ = (w - hi.astype(F32)).astype(BF16)
        return (jnp.dot(hi, ex, preferred_element_type=F32) + jnp.dot(lo_, ex, preferred_element_type=F32))

    oa = (expand(e1 * rden) * o1_ref[...].astype(F32)
          + expand(e2 * rden) * o2_ref[...].astype(F32)
          + expand(e3 * rden) * o3_ref[...].astype(F32))
    pa = jnp.dot(oa.astype(BF16), wpa_ref[...], preferred_element_type=F32)
    pb = jnp.dot(ob_ref[...], wpb_ref[...], preferred_element_type=F32)
    gl = gl_ref[...].astype(F32) + bg_ref[...]
    gates = 1.0 / (1.0 + jnp.exp(-gl))
    merged = gates[:, :D_MODEL] * pa + gates[:, D_MODEL:] * pb
    x1 = x_ref[...] + jnp.dot(merged.astype(BF16), wo_ref[...], preferred_element_type=F32)
    x1_ref[...] = x1
    ms = jnp.mean(x1 * x1, axis=-1, keepdims=True)
    h2 = x1 * lax.rsqrt(ms + EPS) * g2_ref[...]
    h2_ref[...] = h2
    h_hi = h2.astype(BF16)
    h_lo = (h2 - h_hi.astype(F32)).astype(BF16)
    w_hi = wr_ref[0]
    w_lo = wr_ref[1]
    logits = (jnp.dot(h_hi, w_hi, preferred_element_type=F32)
              + jnp.dot(h_lo, w_hi, preferred_element_type=F32)
              + jnp.dot(h_hi, w_lo, preferred_element_type=F32)) + br_ref[...]
    work = logits
    lane_f = lane.astype(F32)
    vals, idxs = [], []
    for _ in range(TOP_K):
        m = jnp.max(work, axis=-1, keepdims=True)
        ix = jnp.min(jnp.where(work == m, lane_f, float(LANES)), axis=-1, keepdims=True)
        vals.append(m)
        idxs.append(ix)
        work = jnp.where(lane_f == ix, -jnp.inf, work)
    es = [jnp.exp(v - vals[0]) for v in vals]
    rsum = 1.0 / (es[0] + es[1] + es[2] + es[3])
    idx_out = jnp.zeros((tm, LANES), F32)
    gw_out = jnp.zeros((tm, LANES), F32)
    for k in range(TOP_K):
        idx_out = jnp.where(lane == k, idxs[k], idx_out)
        gw_out = jnp.where(lane == k, es[k] * rsum, gw_out)
    idx_ref[...] = idx_out.astype(jnp.int32)
    gw_ref[...] = gw_out


def _merge(o1, o2, o3, l1, l2, l3, ob, gl, bg, x2, ex, wpa, wpb, wo, g2, wr, br, tm):
    t = x2.shape[0]
    row = lambda i: (i, 0)
    fixed = lambda i: (0, 0)
    return pl.pallas_call(
        _merge_kernel,
        grid=(t // tm,),
        in_specs=[
            pl.BlockSpec((tm, OUT_A), row), pl.BlockSpec((tm, OUT_A), row), pl.BlockSpec((tm, OUT_A), row),
            pl.BlockSpec((tm, LANES), row), pl.BlockSpec((tm, LANES), row), pl.BlockSpec((tm, LANES), row),
            pl.BlockSpec((tm, WIDTH_B_Q), row),
            pl.BlockSpec((tm, 2 * D_MODEL), row),
            pl.BlockSpec((1, 2 * D_MODEL), fixed),
            pl.BlockSpec((tm, D_MODEL), row),
            pl.BlockSpec((LANES, OUT_A), fixed),
            pl.BlockSpec((OUT_A, D_MODEL), fixed),
            pl.BlockSpec((WIDTH_B_Q, D_MODEL), fixed),
            pl.BlockSpec((D_MODEL, D_MODEL), fixed),
            pl.BlockSpec((1, D_MODEL), fixed),
            pl.BlockSpec((2, D_MODEL, LANES), lambda i: (0, 0, 0)),
            pl.BlockSpec((1, LANES), fixed),
        ],
        out_specs=[
            pl.BlockSpec((tm, D_MODEL), row), pl.BlockSpec((tm, D_MODEL), row),
            pl.BlockSpec((tm, LANES), row), pl.BlockSpec((tm, LANES), row),
        ],
        out_shape=[
            jax.ShapeDtypeStruct((t, D_MODEL), F32),
            jax.ShapeDtypeStruct((t, D_MODEL), F32),
            jax.ShapeDtypeStruct((t, LANES), jnp.int32),
            jax.ShapeDtypeStruct((t, LANES), F32),
        ],
        compiler_params=_cparams(("arbitrary",)),
        name="merge",
    )(o1, o2, o3, l1, l2, l3, ob, gl, bg, x2, ex, wpa, wpb, wo, g2, wr, br)


EXP_BM = 256


def _experts_kernel(be_ref, nused_ref, tok_ref, tokn_ref, dst_ref,
                    h2_hbm, wgu_ref, bgu_ref, wd_ref, bd_ref,
                    y_hbm, xbuf, ybuf, gsem, ssem):
    i = pl.program_id(0)
    nused = nused_ref[0]
    slot = i % 2

    def gather(tref, s):
        return [pltpu.make_async_copy(h2_hbm.at[pl.ds(tref[0, 0, r], 1)], xbuf.at[s, pl.ds(r, 1)], gsem.at[s])
                for r in range(EXP_BM)]

    def scatter():
        return [pltpu.make_async_copy(ybuf.at[pl.ds(r, 1)], y_hbm.at[pl.ds(dst_ref[0, 0, r], 1)], ssem.at[0])
                for r in range(EXP_BM)]

    @pl.when(i < nused)
    def _():
        @pl.when(i == 0)
        def _():
            for c in gather(tok_ref, 0):
                c.start()
            ybuf[...] = jnp.zeros_like(ybuf)
            dump = pltpu.make_async_copy(ybuf, y_hbm.at[pl.ds(y_hbm.shape[0] - EXP_BM, EXP_BM)], ssem.at[0])
            dump.start()
            dump.wait()

        @pl.when(i + 1 < nused)
        def _():
            for c in gather(tokn_ref, 1 - slot):
                c.start()

        for c in gather(tok_ref, slot):
            c.wait()
        xb = xbuf[slot].astype(BF16)
        gu = jnp.dot(xb, wgu_ref[0], preferred_element_type=F32) + bgu_ref[0]
        gate = jnp.minimum(gu[:, :D_FF], SWIGLU_LIMIT)
        up = jnp.clip(gu[:, D_FF:], -SWIGLU_LIMIT, SWIGLU_LIMIT)
        act = (up + 1.0) * (gate * (1.0 / (1.0 + jnp.exp(-SWIGLU_ALPHA * gate))))
        y = jnp.dot(act.astype(BF16), wd_ref[0], preferred_element_type=F32) + bd_ref[0]

        @pl.when(i > 0)
        def _():
            for c in scatter():
                c.wait()

        ybuf[...] = y
        for c in scatter():
            c.start()

        @pl.when(i == nused - 1)
        def _():
            for c in scatter():
                c.wait()


def _experts(block_e, nused, tok_rows, dst_rows, h2, wgu, bgu, wd, bd, n_out_rows):
    n_blk = block_e.shape[0]
    d = h2.shape[1]
    idx_spec = lambda f: pl.BlockSpec((1, 1, EXP_BM), f, memory_space=pltpu.SMEM)
    grid_spec = pltpu.PrefetchScalarGridSpec(
        num_scalar_prefetch=2,
        grid=(n_blk,),
        in_specs=[
            idx_spec(lambda i, be, nu: (i, 0, 0)),
            idx_spec(lambda i, be, nu: (jnp.minimum(i + 1, n_blk - 1), 0, 0)),
            idx_spec(lambda i, be, nu: (i, 0, 0)),
            pl.BlockSpec(memory_space=pl.ANY),
            pl.BlockSpec((1, d, 2 * D_FF), lambda i, be, nu: (be[i], 0, 0)),
            pl.BlockSpec((1, 1, 2 * D_FF), lambda i, be, nu: (be[i], 0, 0)),
            pl.BlockSpec((1, D_FF, d), lambda i, be, nu: (be[i], 0, 0)),
            pl.BlockSpec((1, 1, d), lambda i, be, nu: (be[i], 0, 0)),
        ],
        out_specs=pl.BlockSpec(memory_space=pl.ANY),
        scratch_shapes=[
            pltpu.VMEM((2, EXP_BM, d), F32),
            pltpu.VMEM((EXP_BM, d), F32),
            pltpu.SemaphoreType.DMA((2,)),
            pltpu.SemaphoreType.DMA((1,)),
        ],
    )
    return pl.pallas_call(
        _experts_kernel,
        grid_spec=grid_spec,
        out_shape=jax.ShapeDtypeStruct((n_out_rows, d), F32),
        compiler_params=_cparams(("arbitrary",)),
        name="experts",
    )(block_e, nused, tok_rows, tok_rows, dst_rows, h2, wgu, bgu, wd, bd)


def _combine_kernel(y_ref, gw_ref, x1_ref, o_ref):
    gw = gw_ref[...]
    acc = x1_ref[...]
    for k in range(TOP_K):
        acc = acc + gw[:, k:k + 1] * y_ref[:, k * D_MODEL:(k + 1) * D_MODEL]
    o_ref[...] = acc


def _combine(y4, gw, x1, tm):
    t = x1.shape[0]
    row = lambda i: (i, 0)
    return pl.pallas_call(
        _combine_kernel,
        grid=(t // tm,),
        in_specs=[pl.BlockSpec((tm, TOP_K * D_MODEL), row), pl.BlockSpec((tm, LANES), row),
                  pl.BlockSpec((tm, D_MODEL), row)],
        out_specs=pl.BlockSpec((tm, D_MODEL), row),
        out_shape=jax.ShapeDtypeStruct((t, D_MODEL), F32),
        compiler_params=_cparams(("arbitrary",)),
        name="combine",
    )(y4, gw, x1)


def _routing_plan(idx):
    t = idx.shape[0]
    a = t * TOP_K
    onehot = jnp.sum((idx[:, :, None] == jnp.arange(N_EXPERTS, dtype=jnp.int32)[None, None, :]).astype(jnp.int32), axis=1)
    incl = jnp.cumsum(onehot, axis=0)
    rank = incl - onehot
    counts = incl[-1]
    padded = ((counts + EXP_BM - 1) // EXP_BM) * EXP_BM
    pends = jnp.cumsum(padded)
    pstarts = pends - padded
    dest = pstarts[idx] + jnp.take_along_axis(rank, idx, axis=1)
    n_blk = a // EXP_BM + N_EXPERTS
    p_rows = n_blk * EXP_BM
    block_e = jnp.clip(jnp.searchsorted(pends, jnp.arange(n_blk, dtype=jnp.int32) * EXP_BM, side='right'),
                       0, N_EXPERTS - 1).astype(jnp.int32)
    nused = (pends[-1] // EXP_BM).astype(jnp.int32).reshape(1)
    flat = jnp.arange(a, dtype=jnp.int32)
    src_flat = jnp.full((p_rows,), -1, jnp.int32).at[dest.reshape(-1)].set(flat, unique_indices=True)
    r = jnp.arange(p_rows, dtype=jnp.int32)
    tok_rows = jnp.where(src_flat >= 0, src_flat // TOP_K, 0)
    dst_rows = jnp.where(src_flat >= 0, src_flat, a + r % EXP_BM)
    return block_e, nused, tok_rows.reshape(n_blk, 1, EXP_BM), dst_rows.reshape(n_blk, 1, EXP_BM)


def kernel(x, norm_mix_g, w_in, b_gate, qn_a, kn_a, qn_b, kn_b, w_proj_a, w_proj_b, w_out,
           norm_ffn_g, w_router, b_router, w_gate_up, b_gate_up, w_down, b_down):
    b, s, d = x.shape
    t = b * s
    x2 = x.reshape(t, d)
    tabs_a, tabs_b = _rope_tables(s)

    gq = jnp.tile(qn_a[:, None, :], (1, HEADS_A, 1)).reshape(1, WIDTH_A) * SCALE
    gk = jnp.tile(kn_a[:, None, :], (1, HEADS_A, 1)).reshape(1, WIDTH_A)
    gqk = jnp.concatenate([gq, gk], axis=1)
    gqb = jnp.tile(qn_b, HEADS_B_Q).reshape(1, WIDTH_B_Q) * SCALE
    gkb = jnp.tile(kn_b, HEADS_B_KV).reshape(1, WIDTH_B_KV)
    hid = jnp.arange(MXU_N) // HEAD_DIM
    bd = (hid[:, None] == hid[None, :]).astype(BF16)

    qkva, qb, kvb, gl = _inproj(x2, norm_mix_g.reshape(1, d), w_in.astype(BF16), bd, gqk, gqb, gkb,
                                tabs_a, tabs_b, s, 256)

    outs, lses = [], []
    for gi, (_, dil) in enumerate(DIL_PAIRS):
        ln = s // dil
        if dil == 1:
            o, lse = _mixa(qkva.reshape(b, s, 3 * WIDTH_A), gi, N_GROUPS + gi, 2 * N_GROUPS + gi, ln)
        else:
            sub = qkva.reshape(b, ln, dil, 3 * N_GROUPS, OUT_A)[:, :, :, gi::N_GROUPS, :]
            sub = sub.transpose(0, 2, 1, 3, 4).reshape(b * dil, ln, 3 * OUT_A)
            o, lse = _mixa(sub, 0, 1, 2, ln)
            o = o.reshape(b, dil, ln, OUT_A).transpose(0, 2, 1, 3)
            lse = lse.reshape(b, dil, ln, LANES).transpose(0, 2, 1, 3)
        outs.append(o.reshape(t, OUT_A))
        lses.append(lse.reshape(t, LANES))

    ob = _mixb(qb, kvb, s, 256)

    ex = (jnp.arange(LANES)[:, None] == (jnp.arange(OUT_A) // HEAD_DIM)[None, :]).astype(BF16)
    wr_pad = jnp.zeros((d, LANES), F32).at[:, :N_EXPERTS].set(w_router)
    wr_hi = wr_pad.astype(BF16)
    wr = jnp.stack([wr_hi, (wr_pad - wr_hi.astype(F32)).astype(BF16)])
    br = jnp.full((1, LANES), NEG, F32).at[0, :N_EXPERTS].set(b_router)
    x1, h2, idx_full, gw = _merge(outs[0], outs[1], outs[2], lses[0], lses[1], lses[2], ob, gl,
                                  b_gate.reshape(1, 2 * d), x2, ex, w_proj_a.astype(BF16), w_proj_b.astype(BF16),
                                  w_out.astype(BF16), norm_ffn_g.reshape(1, d), wr, br, 256)

    block_e, nused, tok_rows, dst_rows = _routing_plan(idx_full[:, :TOP_K])
    y = _experts(block_e, nused, tok_rows, dst_rows, h2, w_gate_up.astype(BF16),
                 b_gate_up.reshape(N_EXPERTS, 1, 2 * D_FF), w_down.astype(BF16),
                 b_down.reshape(N_EXPERTS, 1, d), t * TOP_K + EXP_BM)
    out = _combine(y.reshape(t + EXP_BM // TOP_K, TOP_K * d), gw, x1, 256)
    return out.reshape(b, s, d)
```

```python
import functools

import jax
import jax.numpy as jnp
import numpy as np
from jax import lax
from jax.experimental import pallas as pl
from jax.experimental.pallas import tpu as pltpu

F32 = jnp.float32
BF16 = jnp.bfloat16

D_MODEL = 1024
HEAD_DIM = 64
SCALE = HEAD_DIM ** -0.5
EPS = 1e-6
NEG = -1e30
DIL_PAIRS = ((128, 1), (512, 4), (2048, 16))
HALF_WIN = 64
N_GROUPS = 3
HEADS_A = 8
OUT_A = HEADS_A * HEAD_DIM
WIDTH_A = N_GROUPS * OUT_A
ROT_DIM_A = 16
THETA_PARTIAL = 500000.0
HEADS_B_Q = 8
HEADS_B_KV = 2
WIDTH_B_Q = 512
WIDTH_B_KV = 128
GRID_W = 64
THETA_AXIAL = 10000.0
N_EXPERTS = 32
TOP_K = 4
D_FF = 1024
SWIGLU_LIMIT = 7.0
SWIGLU_ALPHA = 1.702

LANES = 128
MXU_N = 256
VMEM_LIMIT = 56 * 1024 * 1024
TILE = 256

COL_VA = 2 * WIDTH_A
COL_QB = 3 * WIDTH_A
COL_KB = COL_QB + WIDTH_B_Q
COL_GATE = COL_KB + 2 * WIDTH_B_KV


def _cparams(sem):
    return pltpu.CompilerParams(dimension_semantics=sem, vmem_limit_bytes=VMEM_LIMIT)


def _perm_rows(dil):
    n = np.arange(TILE)
    per = TILE // dil
    return (n % per) * dil + n // per


def _perm_matrix(dil, transpose=False):
    p = np.zeros((TILE, TILE), np.float32)
    p[np.arange(TILE), _perm_rows(dil)] = 1.0
    return jnp.asarray(p.T if transpose else p, dtype=BF16)


def _rope_tables(seq):
    pos = np.arange(seq, dtype=np.float64)[:, None]
    d = np.arange(LANES) % HEAD_DIM
    half = ROT_DIM_A // 2
    inv = THETA_PARTIAL ** (-((d % half) / half))
    ang = pos * inv[None, :]
    in_rot = (d < ROT_DIM_A)[None, :]
    first = (d < half)[None, :]
    second = ((d >= half) & (d < ROT_DIM_A))[None, :]
    ta = np.stack([np.where(in_rot, np.cos(ang), 1.0), np.where(first, -np.sin(ang), 0.0),
                   np.where(second, np.sin(ang), 0.0)])
    per_group = []
    for _, dil in DIL_PAIRS:
        order = (np.arange(seq // TILE) * TILE)[:, None] + _perm_rows(dil)[None, :]
        per_group.append(ta[:, order.reshape(-1), :])
    hb = HEAD_DIM // 4
    row = np.floor(pos / GRID_W)
    col = pos - row * GRID_W
    invb = THETA_AXIAL ** (-((d % hb) / hb))
    angb = np.where((d < HEAD_DIM // 2)[None, :], row, col) * invb[None, :]
    firstb = ((d % (2 * hb)) < hb)[None, :]
    tb = np.stack([np.cos(angb), np.where(firstb, -np.sin(angb), 0.0), np.where(firstb, 0.0, np.sin(angb))])
    return jnp.asarray(np.stack(per_group), F32), jnp.asarray(tb, F32)


def _inproj_kernel(x_ref, g_ref, w_ref, bd_ref, p4_ref, p16_ref, gqk_ref, gqb_ref, gkb_ref, ta_ref, tb_ref,
                   qkv1_ref, qkv2_ref, qkv3_ref, qb_ref, kvb_ref, gate_ref):
    x = x_ref[...]
    ms = jnp.mean(x * x, axis=-1, keepdims=True)
    h = (x * lax.rsqrt(ms + EPS) * g_ref[...]).astype(BF16)
    h4 = jnp.dot(p4_ref[...], h, preferred_element_type=F32).astype(BF16)
    h16 = jnp.dot(p16_ref[...], h, preferred_element_type=F32).astype(BF16)
    hg = (h, h4, h16)
    bd = bd_ref[...]
    lane = lax.broadcasted_iota(jnp.int32, (TILE, LANES), 1)
    lo = lane < HEAD_DIM

    def proj(lhs, c0):
        return jnp.dot(lhs, w_ref[:, c0:c0 + MXU_N], preferred_element_type=F32)

    def head_norm(y, gain, bdm):
        ss = jnp.dot((y * y).astype(BF16), bdm, preferred_element_type=F32)
        return y * lax.rsqrt(ss * (1.0 / HEAD_DIM) + EPS) * gain

    def rope(z, tab, sh):
        return z * tab[0] + pltpu.roll(z, LANES - sh, 1) * tab[1] + pltpu.roll(z, sh, 1) * tab[2]

    sh_a = ROT_DIM_A // 2
    sh_b = HEAD_DIM // 4

    def store_group(gi, c_out, val):
        if gi == 0:
            qkv1_ref[:, c_out:c_out + LANES] = val
        else:
            ref, dil = ((qkv2_ref, DIL_PAIRS[1][1]), (qkv3_ref, DIL_PAIRS[2][1]))[gi - 1]
            ref[0, :, :, c_out:c_out + LANES] = val.reshape(dil, TILE // dil, LANES)

    for gi in range(N_GROUPS):
        tab = (ta_ref[gi, 0], ta_ref[gi, 1], ta_ref[gi, 2])
        for which in range(3):
            for cc in range(OUT_A // MXU_N):
                c_in = which * WIDTH_A + gi * OUT_A + cc * MXU_N
                y = proj(hg[gi], c_in)
                if which < 2:
                    y = head_norm(y, gqk_ref[:, c_in:c_in + MXU_N], bd)
                for hf in range(2):
                    z = y[:, hf * LANES:(hf + 1) * LANES]
                    if which < 2:
                        z = rope(z, tab, sh_a)
                    store_group(gi, which * OUT_A + cc * MXU_N + hf * LANES, z.astype(BF16))

    tabb = (tb_ref[0], tb_ref[1], tb_ref[2])
    for c in range(WIDTH_B_Q // MXU_N):
        c0 = c * MXU_N
        yn = head_norm(proj(h, COL_QB + c0), gqb_ref[:, c0:c0 + MXU_N], bd)
        for hf in range(2):
            z = yn[:, hf * LANES:(hf + 1) * LANES]
            qb_ref[:, c0 + hf * LANES:c0 + (hf + 1) * LANES] = rope(z, tabb, sh_b).astype(BF16)
    ykv = proj(h, COL_KB)
    kb = rope(head_norm(ykv[:, :LANES], gkb_ref[...], bd[:LANES, :LANES]), tabb, sh_b)
    vb = ykv[:, LANES:]
    for j, t in enumerate((kb, vb)):
        sw = pltpu.roll(t, HEAD_DIM, 1)
        kvb_ref[:, (2 * j) * LANES:(2 * j + 1) * LANES] = jnp.where(lo, t, sw).astype(BF16)
        kvb_ref[:, (2 * j + 1) * LANES:(2 * j + 2) * LANES] = jnp.where(lo, sw, t).astype(BF16)
    for c in range(2 * D_MODEL // MXU_N):
        c0 = c * MXU_N
        gate_ref[:, c0:c0 + MXU_N] = proj(h, COL_GATE + c0).astype(BF16)


def _inproj(x2, g, w_bf, bd, p4, p16, gqk, gqb, gkb, tabs_a, tabs_b, batch, seq):
    t = x2.shape[0]
    n_cols = w_bf.shape[1]
    npos = seq // TILE
    row = lambda i: (i, 0)
    fixed = lambda i: (0, 0)
    d4, d16 = DIL_PAIRS[1][1], DIL_PAIRS[2][1]
    sub = lambda i: (i // npos, 0, i % npos, 0)
    return pl.pallas_call(
        _inproj_kernel,
        grid=(t // TILE,),
        in_specs=[
            pl.BlockSpec((TILE, D_MODEL), row),
            pl.BlockSpec((1, D_MODEL), fixed),
            pl.BlockSpec((D_MODEL, n_cols), fixed, pipeline_mode=pl.Buffered(1)),
            pl.BlockSpec((MXU_N, MXU_N), fixed),
            pl.BlockSpec((TILE, TILE), fixed),
            pl.BlockSpec((TILE, TILE), fixed),
            pl.BlockSpec((1, 2 * WIDTH_A), fixed),
            pl.BlockSpec((1, WIDTH_B_Q), fixed),
            pl.BlockSpec((1, WIDTH_B_KV), fixed),
            pl.BlockSpec((N_GROUPS, 3, TILE, LANES), lambda i: (0, 0, i % npos, 0)),
            pl.BlockSpec((3, TILE, LANES), lambda i: (0, i % npos, 0)),
        ],
        out_specs=[
            pl.BlockSpec((TILE, 3 * OUT_A), row),
            pl.BlockSpec((1, d4, TILE // d4, 3 * OUT_A), sub),
            pl.BlockSpec((1, d16, TILE // d16, 3 * OUT_A), sub),
            pl.BlockSpec((TILE, WIDTH_B_Q), row),
            pl.BlockSpec((TILE, 4 * LANES), row),
            pl.BlockSpec((TILE, 2 * D_MODEL), row),
        ],
        out_shape=[
            jax.ShapeDtypeStruct((t, 3 * OUT_A), BF16),
            jax.ShapeDtypeStruct((batch, d4, seq // d4, 3 * OUT_A), BF16),
            jax.ShapeDtypeStruct((batch, d16, seq // d16, 3 * OUT_A), BF16),
            jax.ShapeDtypeStruct((t, WIDTH_B_Q), BF16),
            jax.ShapeDtypeStruct((t, 4 * LANES), BF16),
            jax.ShapeDtypeStruct((t, 2 * D_MODEL), BF16),
        ],
        compiler_params=_cparams(("arbitrary",)),
        name="inproj",
    )(x2, g, w_bf, bd, p4, p16, gqk, gqb, gkb, tabs_a, tabs_b)


QBLK = 128
NSUB = 4


def _mixa_kernel(q_ref, k_ref, v_ref, o_ref, lse_ref, *, seq_len, win, batched):
    step = pl.program_id(1)
    lane = lax.broadcasted_iota(jnp.int32, (QBLK, LANES), 1)
    lo = lane < HEAD_DIM
    qi = lax.broadcasted_iota(jnp.int32, (QBLK, win), 0)
    ki = lax.broadcasted_iota(jnp.int32, (QBLK, win), 1)
    for b in range(NSUB):
        if batched:
            sq, r0, blk = b, 0, 0
            kstart = 0
        else:
            sq, r0 = 0, b * QBLK
            blk = step * NSUB + b
            kstart = pl.multiple_of(jnp.clip(blk * QBLK - HALF_WIN, 0, seq_len - win), HALF_WIN)
        valid = jnp.abs((ki + kstart) - (qi + blk * QBLK)) <= HALF_WIN
        scores, vals = [], []
        for p in range(HEADS_A // 2):
            cs = slice(p * LANES, (p + 1) * LANES)
            qp = q_ref[sq, r0:r0 + QBLK, cs]
            kp = k_ref[sq, pl.ds(kstart, win), cs]
            vals.append(v_ref[sq, pl.ds(kstart, win), cs])
            for hh in range(2):
                qh = jnp.where(lo if hh == 0 else jnp.logical_not(lo), qp, jnp.zeros_like(qp))
                scores.append(lax.dot_general(qh, kp, (((1,), (1,)), ((), ())), preferred_element_type=F32))
        s = jnp.where(valid[None], jnp.stack(scores), NEG)
        m = jnp.max(s, axis=-1, keepdims=True)
        e = jnp.exp(s - m)
        l = jnp.sum(e, axis=-1, keepdims=True)
        rl = 1.0 / l
        lse = m + jnp.log(l)
        eb = e.astype(BF16)
        lse_acc = jnp.zeros((QBLK, LANES), F32)
        for p in range(HEADS_A // 2):
            outs = [jnp.dot(eb[2 * p + hh], vals[p], preferred_element_type=F32) * rl[2 * p + hh] for hh in range(2)]
            o_ref[sq, r0:r0 + QBLK, p * LANES:(p + 1) * LANES] = jnp.where(lo, outs[0], outs[1]).astype(BF16)
            for hh in range(2):
                lse_acc = jnp.where(lane == 2 * p + hh, lse[2 * p + hh], lse_acc)
        lse_ref[sq, r0:r0 + QBLK, :] = lse_acc


def _mixa(qkv, seq_len):
    n_seq = qkv.shape[0]
    win = min(2 * QBLK, seq_len)
    batched = seq_len == QBLK
    if batched:
        grid = (n_seq // NSUB, 1)
        qspec = lambda c: pl.BlockSpec((NSUB, QBLK, OUT_A), lambda s, i: (s, 0, c))
        kspec = qspec
        ospec = pl.BlockSpec((NSUB, QBLK, OUT_A), lambda s, i: (s, 0, 0))
        lspec = pl.BlockSpec((NSUB, QBLK, LANES), lambda s, i: (s, 0, 0))
    else:
        rows = NSUB * QBLK
        grid = (n_seq, seq_len // rows)
        qspec = lambda c: pl.BlockSpec((1, rows, OUT_A), lambda s, i: (s, i, c))
        kspec = lambda c: pl.BlockSpec((1, seq_len, OUT_A), lambda s, i: (s, 0, c))
        ospec = pl.BlockSpec((1, rows, OUT_A), lambda s, i: (s, i, 0))
        lspec = pl.BlockSpec((1, rows, LANES), lambda s, i: (s, i, 0))
    return pl.pallas_call(
        functools.partial(_mixa_kernel, seq_len=seq_len, win=win, batched=batched),
        grid=grid,
        in_specs=[qspec(0), kspec(1), kspec(2)],
        out_specs=[ospec, lspec],
        out_shape=[
            jax.ShapeDtypeStruct((n_seq, seq_len, OUT_A), BF16),
            jax.ShapeDtypeStruct((n_seq, seq_len, LANES), F32),
        ],
        compiler_params=_cparams(("arbitrary", "arbitrary")),
        name=f"mixa_len{seq_len}",
    )(qkv, qkv, qkv)


def _mixb_kernel(q_ref, k_ref, v_ref, o_ref):
    tq = q_ref.shape[0]
    lane = lax.broadcasted_iota(jnp.int32, (tq, LANES), 1)
    lo = lane < HEAD_DIM
    for p in range(HEADS_B_Q // 2):
        j = p // 2
        cs = slice(p * LANES, (p + 1) * LANES)
        qp = q_ref[:, cs]
        kd = k_ref[:, j * LANES:(j + 1) * LANES]
        vd = v_ref[:, j * LANES:(j + 1) * LANES]
        outs = []
        for hh in range(2):
            qh = jnp.where(lo if hh == 0 else jnp.logical_not(lo), qp, jnp.zeros_like(qp))
            s = lax.dot_general(qh, kd, (((1,), (1,)), ((), ())), preferred_element_type=F32)
            m = jnp.max(s, axis=-1, keepdims=True)
            e = jnp.exp(s - m)
            l = jnp.sum(e, axis=-1, keepdims=True)
            outs.append(jnp.dot(e.astype(BF16), vd, preferred_element_type=F32) * (1.0 / l))
        o_ref[:, cs] = jnp.where(lo, outs[0], outs[1]).astype(BF16)


def _mixb(qb, kvb, seq, tq):
    t = qb.shape[0]
    nq = seq // tq
    return pl.pallas_call(
        _mixb_kernel,
        grid=(t // seq, nq),
        in_specs=[
            pl.BlockSpec((tq, WIDTH_B_Q), lambda b, i: (b * nq + i, 0)),
            pl.BlockSpec((seq, 2 * LANES), lambda b, i: (b, 0)),
            pl.BlockSpec((seq, 2 * LANES), lambda b, i: (b, 1)),
        ],
        out_specs=pl.BlockSpec((tq, WIDTH_B_Q), lambda b, i: (b * nq + i, 0)),
        out_shape=jax.ShapeDtypeStruct((t, WIDTH_B_Q), BF16),
        compiler_params=_cparams(("arbitrary", "arbitrary")),
        name="mixb",
    )(qb, kvb, kvb)


def _merge_kernel(o1_ref, o2_ref, o3_ref, l1_ref, l2_ref, l3_ref, p4t_ref, p16t_ref, ob_ref, gl_ref, bg_ref,
                  x_ref, ex_ref, wpa_ref, wpb_ref, wo_ref, g2_ref, wr_ref, br_ref,
                  x1_ref, h2_ref, idx_ref, gw_ref):
    lane = lax.broadcasted_iota(jnp.int32, (TILE, LANES), 1)

    def split(w):
        hi = w.astype(BF16)
        return hi, (w - hi.astype(F32)).astype(BF16)

    def unperm(pt, val):
        return jnp.dot(pt, val, preferred_element_type=F32)

    def unperm_f32(pt, val):
        hi, lo_ = split(val)
        return unperm(pt, hi) + unperm(pt, lo_)

    p4t, p16t = p4t_ref[...], p16t_ref[...]
    o1 = o1_ref[...].astype(F32)
    o2 = unperm(p4t, o2_ref[0].reshape(TILE, OUT_A))
    o3 = unperm(p16t, o3_ref[0].reshape(TILE, OUT_A))
    l1 = l1_ref[...]
    l2 = unperm_f32(p4t, l2_ref[0].reshape(TILE, LANES))
    l3 = unperm_f32(p16t, l3_ref[0].reshape(TILE, LANES))
    mx = jnp.maximum(jnp.maximum(l1, l2), l3)
    e1, e2, e3 = jnp.exp(l1 - mx), jnp.exp(l2 - mx), jnp.exp(l3 - mx)
    rden = 1.0 / (e1 + e2 + e3)
    ex = ex_ref[...]

    def expand(w):
        hi, lo_ = split(w)
        return jnp.dot(hi, ex, preferred_element_type=F32) + jnp.dot(lo_, ex, preferred_element_type=F32)

    oa = expand(e1 * rden) * o1 + expand(e2 * rden) * o2 + expand(e3 * rden) * o3
    pa = jnp.dot(oa.astype(BF16), wpa_ref[...], preferred_element_type=F32)
    pb = jnp.dot(ob_ref[...], wpb_ref[...], preferred_element_type=F32)
    gl = gl_ref[...].astype(F32) + bg_ref[...]
    gates = 1.0 / (1.0 + jnp.exp(-gl))
    merged = gates[:, :D_MODEL] * pa + gates[:, D_MODEL:] * pb
    x1 = x_ref[...] + jnp.dot(merged.astype(BF16), wo_ref[...], preferred_element_type=F32)
    x1_ref[...] = x1
    ms = jnp.mean(x1 * x1, axis=-1, keepdims=True)
    h2 = x1 * lax.rsqrt(ms + EPS) * g2_ref[...]
    h2_ref[...] = h2
    h_hi, h_lo = split(h2)
    w_hi = wr_ref[0]
    w_lo = wr_ref[1]
    logits = (jnp.dot(h_hi, w_hi, preferred_element_type=F32)
              + jnp.dot(h_lo, w_hi, preferred_element_type=F32)
              + jnp.dot(h_hi, w_lo, preferred_element_type=F32)) + br_ref[...]
    work = logits
    lane_f = lane.astype(F32)
    vals, idxs = [], []
    for _ in range(TOP_K):
        m = jnp.max(work, axis=-1, keepdims=True)
        ix = jnp.min(jnp.where(work == m, lane_f, float(LANES)), axis=-1, keepdims=True)
        vals.append(m)
        idxs.append(ix)
        work = jnp.where(lane_f == ix, -jnp.inf, work)
    es = [jnp.exp(v - vals[0]) for v in vals]
    rsum = 1.0 / (es[0] + es[1] + es[2] + es[3])
    idx_out = jnp.zeros((TILE, LANES), F32)
    gw_out = jnp.zeros((TILE, LANES), F32)
    for k in range(TOP_K):
        idx_out = jnp.where(lane == k, idxs[k], idx_out)
        gw_out = jnp.where(lane == k, es[k] * rsum, gw_out)
    idx_ref[...] = idx_out.astype(jnp.int32)
    gw_ref[...] = gw_out


def _merge(o1, o2, o3, l1, l2, l3, p4t, p16t, ob, gl, bg, x2, ex, wpa, wpb, wo, g2, wr, br, seq):
    t = x2.shape[0]
    npos = seq // TILE
    row = lambda i: (i, 0)
    fixed = lambda i: (0, 0)
    d4, d16 = DIL_PAIRS[1][1], DIL_PAIRS[2][1]
    sub = lambda i: (i // npos, 0, i % npos, 0)
    return pl.pallas_call(
        _merge_kernel,
        grid=(t // TILE,),
        in_specs=[
            pl.BlockSpec((TILE, OUT_A), row),
            pl.BlockSpec((1, d4, TILE // d4, OUT_A), sub),
            pl.BlockSpec((1, d16, TILE // d16, OUT_A), sub),
            pl.BlockSpec((TILE, LANES), row),
            pl.BlockSpec((1, d4, TILE // d4, LANES), sub),
            pl.BlockSpec((1, d16, TILE // d16, LANES), sub),
            pl.BlockSpec((TILE, TILE), fixed),
            pl.BlockSpec((TILE, TILE), fixed),
            pl.BlockSpec((TILE, WIDTH_B_Q), row),
            pl.BlockSpec((TILE, 2 * D_MODEL), row),
            pl.BlockSpec((1, 2 * D_MODEL), fixed),
            pl.BlockSpec((TILE, D_MODEL), row),
            pl.BlockSpec((LANES, OUT_A), fixed),
            pl.BlockSpec((OUT_A, D_MODEL), fixed),
            pl.BlockSpec((WIDTH_B_Q, D_MODEL), fixed),
            pl.BlockSpec((D_MODEL, D_MODEL), fixed),
            pl.BlockSpec((1, D_MODEL), fixed),
            pl.BlockSpec((2, D_MODEL, LANES), lambda i: (0, 0, 0)),
            pl.BlockSpec((1, LANES), fixed),
        ],
        out_specs=[
            pl.BlockSpec((TILE, D_MODEL), row), pl.BlockSpec((TILE, D_MODEL), row),
            pl.BlockSpec((TILE, LANES), row), pl.BlockSpec((TILE, LANES), row),
        ],
        out_shape=[
            jax.ShapeDtypeStruct((t, D_MODEL), F32),
            jax.ShapeDtypeStruct((t, D_MODEL), F32),
            jax.ShapeDtypeStruct((t, LANES), jnp.int32),
            jax.ShapeDtypeStruct((t, LANES), F32),
        ],
        compiler_params=_cparams(("arbitrary",)),
        name="merge",
    )(o1, o2, o3, l1, l2, l3, p4t, p16t, ob, gl, bg, x2, ex, wpa, wpb, wo, g2, wr, br)


EXP_BM = 256


def _experts_kernel(be_ref, nused_ref, tok_ref, tokn_ref, dst_ref,
                    h2_hbm, wgu_ref, bgu_ref, wd_ref, bd_ref,
                    y_hbm, xbuf, ybuf, gsem, ssem):
    i = pl.program_id(0)
    nused = nused_ref[0]
    slot = i % 2

    def gather(tref, s):
        return [pltpu.make_async_copy(h2_hbm.at[pl.ds(tref[0, 0, r], 1)], xbuf.at[s, pl.ds(r, 1)], gsem.at[s])
                for r in range(EXP_BM)]

    def scatter():
        return [pltpu.make_async_copy(ybuf.at[pl.ds(r, 1)], y_hbm.at[pl.ds(dst_ref[0, 0, r], 1)], ssem.at[0])
                for r in range(EXP_BM)]

    @pl.when(i < nused)
    def _():
        @pl.when(i == 0)
        def _():
            for c in gather(tok_ref, 0):
                c.start()
            ybuf[...] = jnp.zeros_like(ybuf)
            dump = pltpu.make_async_copy(ybuf, y_hbm.at[pl.ds(y_hbm.shape[0] - EXP_BM, EXP_BM)], ssem.at[0])
            dump.start()
            dump.wait()

        @pl.when(i + 1 < nused)
        def _():
            for c in gather(tokn_ref, 1 - slot):
                c.start()

        for c in gather(tok_ref, slot):
            c.wait()
        xb = xbuf[slot].astype(BF16)
        gu = jnp.dot(xb, wgu_ref[0], preferred_element_type=F32) + bgu_ref[0]
        gate = jnp.minimum(gu[:, :D_FF], SWIGLU_LIMIT)
        up = jnp.clip(gu[:, D_FF:], -SWIGLU_LIMIT, SWIGLU_LIMIT)
        act = (up + 1.0) * (gate * (1.0 / (1.0 + jnp.exp(-SWIGLU_ALPHA * gate))))
        y = jnp.dot(act.astype(BF16), wd_ref[0], preferred_element_type=F32) + bd_ref[0]

        @pl.when(i > 0)
        def _():
            for c in scatter():
                c.wait()

        ybuf[...] = y
        for c in scatter():
            c.start()

        @pl.when(i == nused - 1)
        def _():
            for c in scatter():
                c.wait()


def _experts(block_e, nused, tok_rows, dst_rows, h2, wgu, bgu, wd, bd, n_out_rows):
    n_blk = block_e.shape[0]
    d = h2.shape[1]
    idx_spec = lambda f: pl.BlockSpec((1, 1, EXP_BM), f, memory_space=pltpu.SMEM)
    grid_spec = pltpu.PrefetchScalarGridSpec(
        num_scalar_prefetch=2,
        grid=(n_blk,),
        in_specs=[
            idx_spec(lambda i, be, nu: (i, 0, 0)),
            idx_spec(lambda i, be, nu: (jnp.minimum(i + 1, n_blk - 1), 0, 0)),
            idx_spec(lambda i, be, nu: (i, 0, 0)),
            pl.BlockSpec(memory_space=pl.ANY),
            pl.BlockSpec((1, d, 2 * D_FF), lambda i, be, nu: (be[i], 0, 0)),
            pl.BlockSpec((1, 1, 2 * D_FF), lambda i, be, nu: (be[i], 0, 0)),
            pl.BlockSpec((1, D_FF, d), lambda i, be, nu: (be[i], 0, 0)),
            pl.BlockSpec((1, 1, d), lambda i, be, nu: (be[i], 0, 0)),
        ],
        out_specs=pl.BlockSpec(memory_space=pl.ANY),
        scratch_shapes=[
            pltpu.VMEM((2, EXP_BM, d), F32),
            pltpu.VMEM((EXP_BM, d), F32),
            pltpu.SemaphoreType.DMA((2,)),
            pltpu.SemaphoreType.DMA((1,)),
        ],
    )
    return pl.pallas_call(
        _experts_kernel,
        grid_spec=grid_spec,
        out_shape=jax.ShapeDtypeStruct((n_out_rows, d), F32),
        compiler_params=_cparams(("arbitrary",)),
        name="experts",
    )(block_e, nused, tok_rows, tok_rows, dst_rows, h2, wgu, bgu, wd, bd)


def _combine_kernel(y_ref, gw_ref, x1_ref, o_ref):
    gw = gw_ref[...]
    acc = x1_ref[...]
    for k in range(TOP_K):
        acc = acc + gw[:, k:k + 1] * y_ref[:, k * D_MODEL:(k + 1) * D_MODEL]
    o_ref[...] = acc


def _combine(y4, gw, x1):
    t = x1.shape[0]
    row = lambda i: (i, 0)
    return pl.pallas_call(
        _combine_kernel,
        grid=(t // TILE,),
        in_specs=[pl.BlockSpec((TILE, TOP_K * D_MODEL), row), pl.BlockSpec((TILE, LANES), row),
                  pl.BlockSpec((TILE, D_MODEL), row)],
        out_specs=pl.BlockSpec((TILE, D_MODEL), row),
        out_shape=jax.ShapeDtypeStruct((t, D_MODEL), F32),
        compiler_params=_cparams(("arbitrary",)),
        name="combine",
    )(y4, gw, x1)


def _routing_plan(idx):
    t = idx.shape[0]
    a = t * TOP_K
    onehot = jnp.sum((idx[:, :, None] == jnp.arange(N_EXPERTS, dtype=jnp.int32)[None, None, :]).astype(jnp.int32), axis=1)
    incl = jnp.cumsum(onehot, axis=0)
    rank = incl - onehot
    counts = incl[-1]
    padded = ((counts + EXP_BM - 1) // EXP_BM) * EXP_BM
    pends = jnp.cumsum(padded)
    pstarts = pends - padded
    dest = pstarts[idx] + jnp.take_along_axis(rank, idx, axis=1)
    n_blk = a // EXP_BM + N_EXPERTS
    p_rows = n_blk * EXP_BM
    block_e = jnp.clip(jnp.searchsorted(pends, jnp.arange(n_blk, dtype=jnp.int32) * EXP_BM, side='right'),
                       0, N_EXPERTS - 1).astype(jnp.int32)
    nused = (pends[-1] // EXP_BM).astype(jnp.int32).reshape(1)
    flat = jnp.arange(a, dtype=jnp.int32)
    src_flat = jnp.full((p_rows,), -1, jnp.int32).at[dest.reshape(-1)].set(flat, unique_indices=True)
    r = jnp.arange(p_rows, dtype=jnp.int32)
    tok_rows = jnp.where(src_flat >= 0, src_flat // TOP_K, 0)
    dst_rows = jnp.where(src_flat >= 0, src_flat, a + r % EXP_BM)
    return block_e, nused, tok_rows.reshape(n_blk, 1, EXP_BM), dst_rows.reshape(n_blk, 1, EXP_BM)


def kernel(x, norm_mix_g, w_in, b_gate, qn_a, kn_a, qn_b, kn_b, w_proj_a, w_proj_b, w_out,
           norm_ffn_g, w_router, b_router, w_gate_up, b_gate_up, w_down, b_down):
    b, s, d = x.shape
    t = b * s
    x2 = x.reshape(t, d)
    tabs_a, tabs_b = _rope_tables(s)
    d4, d16 = DIL_PAIRS[1][1], DIL_PAIRS[2][1]

    gq = jnp.tile(qn_a[:, None, :], (1, HEADS_A, 1)).reshape(1, WIDTH_A) * SCALE
    gk = jnp.tile(kn_a[:, None, :], (1, HEADS_A, 1)).reshape(1, WIDTH_A)
    gqk = jnp.concatenate([gq, gk], axis=1)
    gqb = jnp.tile(qn_b, HEADS_B_Q).reshape(1, WIDTH_B_Q) * SCALE
    gkb = jnp.tile(kn_b, HEADS_B_KV).reshape(1, WIDTH_B_KV)
    hid = np.arange(MXU_N) // HEAD_DIM
    bd = jnp.asarray(hid[:, None] == hid[None, :], dtype=BF16)

    qkv1, qkv2, qkv3, qb, kvb, gl = _inproj(
        x2, norm_mix_g.reshape(1, d), w_in.astype(BF16), bd, _perm_matrix(d4), _perm_matrix(d16),
        gqk, gqb, gkb, tabs_a, tabs_b, b, s)

    o1, l1 = _mixa(qkv1.reshape(b, s, 3 * OUT_A), s)
    o2, l2 = _mixa(qkv2.reshape(b * d4, s // d4, 3 * OUT_A), s // d4)
    o3, l3 = _mixa(qkv3.reshape(b * d16, s // d16, 3 * OUT_A), s // d16)
    ob = _mixb(qb, kvb, s, 256)

    ex = jnp.asarray(np.arange(LANES)[:, None] == (np.arange(OUT_A) // HEAD_DIM)[None, :], dtype=BF16)
    wr_pad = jnp.zeros((d, LANES), F32).at[:, :N_EXPERTS].set(w_router)
    wr_hi = wr_pad.astype(BF16)
    wr = jnp.stack([wr_hi, (wr_pad - wr_hi.astype(F32)).astype(BF16)])
    br = jnp.full((1, LANES), NEG, F32).at[0, :N_EXPERTS].set(b_router)
    x1, h2, idx_full, gw = _merge(
        o1.reshape(t, OUT_A), o2.reshape(b, d4, s // d4, OUT_A), o3.reshape(b, d16, s // d16, OUT_A),
        l1.reshape(t, LANES), l2.reshape(b, d4, s // d4, LANES), l3.reshape(b, d16, s // d16, LANES),
        _perm_matrix(d4, True), _perm_matrix(d16, True), ob, gl, b_gate.reshape(1, 2 * d), x2, ex,
        w_proj_a.astype(BF16), w_proj_b.astype(BF16), w_out.astype(BF16), norm_ffn_g.reshape(1, d), wr, br, s)

    block_e, nused, tok_rows, dst_rows = _routing_plan(idx_full[:, :TOP_K])
    y = _experts(block_e, nused, tok_rows, dst_rows, h2, w_gate_up.astype(BF16),
                 b_gate_up.reshape(N_EXPERTS, 1, 2 * D_FF), w_down.astype(BF16),
                 b_down.reshape(N_EXPERTS, 1, d), t * TOP_K + EXP_BM)
    out = _combine(y.reshape(t + EXP_BM // TOP_K, TOP_K * d), gw, x1)
    return out.reshape(b, s, d)
```

```python
import functools

import jax
import jax.numpy as jnp
import numpy as np
from jax import lax
from jax.experimental import pallas as pl
from jax.experimental.pallas import tpu as pltpu

F32 = jnp.float32
BF16 = jnp.bfloat16

D_MODEL = 1024
HEAD_DIM = 64
SCALE = HEAD_DIM ** -0.5
EPS = 1e-6
NEG = -1e30
DIL_PAIRS = ((128, 1), (512, 4), (2048, 16))
HALF_WIN = 64
N_GROUPS = 3
HEADS_A = 8
OUT_A = HEADS_A * HEAD_DIM
WIDTH_A = N_GROUPS * OUT_A
ROT_DIM_A = 16
THETA_PARTIAL = 500000.0
HEADS_B_Q = 8
HEADS_B_KV = 2
WIDTH_B_Q = 512
WIDTH_B_KV = 128
GRID_W = 64
THETA_AXIAL = 10000.0
N_EXPERTS = 32
TOP_K = 4
D_FF = 1024
SWIGLU_LIMIT = 7.0
SWIGLU_ALPHA = 1.702

LANES = 128
MXU_N = 256
VMEM_LIMIT = 56 * 1024 * 1024
TILE = 256
INPROJ_TM = 512

COL_VA = 2 * WIDTH_A
COL_QB = 3 * WIDTH_A
COL_KB = COL_QB + WIDTH_B_Q
COL_GATE = COL_KB + 2 * WIDTH_B_KV


def _cparams(sem):
    return pltpu.CompilerParams(dimension_semantics=sem, vmem_limit_bytes=VMEM_LIMIT)


def _perm_rows(dil):
    n = np.arange(TILE)
    per = TILE // dil
    return (n % per) * dil + n // per


def _perm_matrix(dil, transpose=False):
    p = np.zeros((TILE, TILE), np.float32)
    p[np.arange(TILE), _perm_rows(dil)] = 1.0
    return jnp.asarray(p.T if transpose else p, dtype=BF16)


def _rope_tables(seq):
    pos = np.arange(seq, dtype=np.float64)[:, None]
    d = np.arange(LANES) % HEAD_DIM
    half = ROT_DIM_A // 2
    inv = THETA_PARTIAL ** (-((d % half) / half))
    ang = pos * inv[None, :]
    in_rot = (d < ROT_DIM_A)[None, :]
    first = (d < half)[None, :]
    second = ((d >= half) & (d < ROT_DIM_A))[None, :]
    ta = np.stack([np.where(in_rot, np.cos(ang), 1.0), np.where(first, -np.sin(ang), 0.0),
                   np.where(second, np.sin(ang), 0.0)])
    per_group = []
    for _, dil in DIL_PAIRS:
        order = (np.arange(seq // TILE) * TILE)[:, None] + _perm_rows(dil)[None, :]
        per_group.append(ta[:, order.reshape(-1), :])
    hb = HEAD_DIM // 4
    row = np.floor(pos / GRID_W)
    col = pos - row * GRID_W
    invb = THETA_AXIAL ** (-((d % hb) / hb))
    angb = np.where((d < HEAD_DIM // 2)[None, :], row, col) * invb[None, :]
    firstb = ((d % (2 * hb)) < hb)[None, :]
    tb = np.stack([np.cos(angb), np.where(firstb, -np.sin(angb), 0.0), np.where(firstb, 0.0, np.sin(angb))])
    return jnp.asarray(np.stack(per_group), F32), jnp.asarray(tb, F32)


def _inproj_kernel(x_ref, g_ref, w_ref, bd_ref, p4_ref, p16_ref, gqk_ref, gqb_ref, gkb_ref, ta_ref, tb_ref,
                   qkv1_ref, qkv2_ref, qkv3_ref, qb_ref, kvb_ref, gate_ref):
    x = x_ref[...]
    tm = x.shape[0]
    nsub = tm // TILE
    ms = jnp.mean(x * x, axis=-1, keepdims=True)
    h = (x * lax.rsqrt(ms + EPS) * g_ref[...]).astype(BF16)

    def regroup(p_ref):
        return jnp.concatenate(
            [jnp.dot(p_ref[...], h[s * TILE:(s + 1) * TILE], preferred_element_type=F32) for s in range(nsub)],
            axis=0).astype(BF16)

    hg = (h, regroup(p4_ref), regroup(p16_ref))
    bd = bd_ref[...]
    lane = lax.broadcasted_iota(jnp.int32, (tm, LANES), 1)
    lo = lane < HEAD_DIM

    def proj(lhs, c0):
        return jnp.dot(lhs, w_ref[:, c0:c0 + MXU_N], preferred_element_type=F32)

    def head_norm(y, gain, bdm):
        ss = jnp.dot((y * y).astype(BF16), bdm, preferred_element_type=F32)
        return y * lax.rsqrt(ss * (1.0 / HEAD_DIM) + EPS) * gain

    def rope(z, tab, sh):
        return z * tab[0] + pltpu.roll(z, LANES - sh, 1) * tab[1] + pltpu.roll(z, sh, 1) * tab[2]

    sh_a = ROT_DIM_A // 2
    sh_b = HEAD_DIM // 4

    def store_group(gi, c_out, val):
        if gi == 0:
            qkv1_ref[:, c_out:c_out + LANES] = val
        else:
            ref, dil = ((qkv2_ref, DIL_PAIRS[1][1]), (qkv3_ref, DIL_PAIRS[2][1]))[gi - 1]
            per = TILE // dil
            for s in range(nsub):
                ref[0, :, s * per:(s + 1) * per, c_out:c_out + LANES] = (
                    val[s * TILE:(s + 1) * TILE].reshape(dil, per, LANES))

    for gi in range(N_GROUPS):
        tab = (ta_ref[gi, 0], ta_ref[gi, 1], ta_ref[gi, 2])
        for which in range(3):
            for cc in range(OUT_A // MXU_N):
                c_in = which * WIDTH_A + gi * OUT_A + cc * MXU_N
                y = proj(hg[gi], c_in)
                if which < 2:
                    y = head_norm(y, gqk_ref[:, c_in:c_in + MXU_N], bd)
                for hf in range(2):
                    z = y[:, hf * LANES:(hf + 1) * LANES]
                    if which < 2:
                        z = rope(z, tab, sh_a)
                    store_group(gi, which * OUT_A + cc * MXU_N + hf * LANES, z.astype(BF16))

    tabb = (tb_ref[0], tb_ref[1], tb_ref[2])
    for c in range(WIDTH_B_Q // MXU_N):
        c0 = c * MXU_N
        yn = head_norm(proj(h, COL_QB + c0), gqb_ref[:, c0:c0 + MXU_N], bd)
        for hf in range(2):
            z = yn[:, hf * LANES:(hf + 1) * LANES]
            qb_ref[:, c0 + hf * LANES:c0 + (hf + 1) * LANES] = rope(z, tabb, sh_b).astype(BF16)
    ykv = proj(h, COL_KB)
    kb = rope(head_norm(ykv[:, :LANES], gkb_ref[...], bd[:LANES, :LANES]), tabb, sh_b)
    vb = ykv[:, LANES:]
    for j, t in enumerate((kb, vb)):
        sw = pltpu.roll(t, HEAD_DIM, 1)
        kvb_ref[:, (2 * j) * LANES:(2 * j + 1) * LANES] = jnp.where(lo, t, sw).astype(BF16)
        kvb_ref[:, (2 * j + 1) * LANES:(2 * j + 2) * LANES] = jnp.where(lo, sw, t).astype(BF16)
    for c in range(2 * D_MODEL // MXU_N):
        c0 = c * MXU_N
        gate_ref[:, c0:c0 + MXU_N] = proj(h, COL_GATE + c0).astype(BF16)


def _inproj(x2, g, w_bf, bd, p4, p16, gqk, gqb, gkb, tabs_a, tabs_b, batch, seq):
    t = x2.shape[0]
    n_cols = w_bf.shape[1]
    tm = INPROJ_TM
    npos = seq // tm
    row = lambda i: (i, 0)
    fixed = lambda i: (0, 0)
    d4, d16 = DIL_PAIRS[1][1], DIL_PAIRS[2][1]
    sub = lambda i: (i // npos, 0, i % npos, 0)
    return pl.pallas_call(
        _inproj_kernel,
        grid=(t // tm,),
        in_specs=[
            pl.BlockSpec((tm, D_MODEL), row),
            pl.BlockSpec((1, D_MODEL), fixed),
            pl.BlockSpec((D_MODEL, n_cols), fixed, pipeline_mode=pl.Buffered(1)),
            pl.BlockSpec((MXU_N, MXU_N), fixed),
            pl.BlockSpec((TILE, TILE), fixed),
            pl.BlockSpec((TILE, TILE), fixed),
            pl.BlockSpec((1, 2 * WIDTH_A), fixed),
            pl.BlockSpec((1, WIDTH_B_Q), fixed),
            pl.BlockSpec((1, WIDTH_B_KV), fixed),
            pl.BlockSpec((N_GROUPS, 3, tm, LANES), lambda i: (0, 0, i % npos, 0)),
            pl.BlockSpec((3, tm, LANES), lambda i: (0, i % npos, 0)),
        ],
        out_specs=[
            pl.BlockSpec((tm, 3 * OUT_A), row),
            pl.BlockSpec((1, d4, tm // d4, 3 * OUT_A), sub),
            pl.BlockSpec((1, d16, tm // d16, 3 * OUT_A), sub),
            pl.BlockSpec((tm, WIDTH_B_Q), row),
            pl.BlockSpec((tm, 4 * LANES), row),
            pl.BlockSpec((tm, 2 * D_MODEL), row),
        ],
        out_shape=[
            jax.ShapeDtypeStruct((t, 3 * OUT_A), BF16),
            jax.ShapeDtypeStruct((batch, d4, seq // d4, 3 * OUT_A), BF16),
            jax.ShapeDtypeStruct((batch, d16, seq // d16, 3 * OUT_A), BF16),
            jax.ShapeDtypeStruct((t, WIDTH_B_Q), BF16),
            jax.ShapeDtypeStruct((t, 4 * LANES), BF16),
            jax.ShapeDtypeStruct((t, 2 * D_MODEL), BF16),
        ],
        compiler_params=_cparams(("arbitrary",)),
        name="inproj",
    )(x2, g, w_bf, bd, p4, p16, gqk, gqb, gkb, tabs_a, tabs_b)


QBLK = 128
NSUB = 4


def _mixa_kernel(q_ref, k_ref, v_ref, o_ref, lse_ref, *, seq_len, win, batched):
    step = pl.program_id(1)
    lane = lax.broadcasted_iota(jnp.int32, (QBLK, LANES), 1)
    lo = lane < HEAD_DIM
    qi = lax.broadcasted_iota(jnp.int32, (QBLK, win), 0)
    ki = lax.broadcasted_iota(jnp.int32, (QBLK, win), 1)
    for b in range(NSUB):
        if batched:
            sq, r0, blk = b, 0, 0
            kstart = 0
        else:
            sq, r0 = 0, b * QBLK
            blk = step * NSUB + b
            kstart = pl.multiple_of(jnp.clip(blk * QBLK - HALF_WIN, 0, seq_len - win), HALF_WIN)
        valid = jnp.abs((ki + kstart) - (qi + blk * QBLK)) <= HALF_WIN
        scores, vals = [], []
        for p in range(HEADS_A // 2):
            cs = slice(p * LANES, (p + 1) * LANES)
            qp = q_ref[sq, r0:r0 + QBLK, cs]
            kp = k_ref[sq, pl.ds(kstart, win), cs]
            vals.append(v_ref[sq, pl.ds(kstart, win), cs])
            for hh in range(2):
                qh = jnp.where(lo if hh == 0 else jnp.logical_not(lo), qp, jnp.zeros_like(qp))
                scores.append(lax.dot_general(qh, kp, (((1,), (1,)), ((), ())), preferred_element_type=F32))
        s = jnp.where(valid[None], jnp.stack(scores), NEG)
        m = jnp.max(s, axis=-1, keepdims=True)
        e = jnp.exp(s - m)
        l = jnp.sum(e, axis=-1, keepdims=True)
        rl = 1.0 / l
        lse = m + jnp.log(l)
        eb = e.astype(BF16)
        lse_acc = jnp.zeros((QBLK, LANES), F32)
        for p in range(HEADS_A // 2):
            outs = [jnp.dot(eb[2 * p + hh], vals[p], preferred_element_type=F32) * rl[2 * p + hh] for hh in range(2)]
            o_ref[sq, r0:r0 + QBLK, p * LANES:(p + 1) * LANES] = jnp.where(lo, outs[0], outs[1]).astype(BF16)
            for hh in range(2):
                lse_acc = jnp.where(lane == 2 * p + hh, lse[2 * p + hh], lse_acc)
        lse_ref[sq, r0:r0 + QBLK, :] = lse_acc


def _mixa(qkv, seq_len):
    n_seq = qkv.shape[0]
    win = min(2 * QBLK, seq_len)
    batched = seq_len == QBLK
    if batched:
        grid = (n_seq // NSUB, 1)
        qspec = lambda c: pl.BlockSpec((NSUB, QBLK, OUT_A), lambda s, i: (s, 0, c))
        kspec = qspec
        ospec = pl.BlockSpec((NSUB, QBLK, OUT_A), lambda s, i: (s, 0, 0))
        lspec = pl.BlockSpec((NSUB, QBLK, LANES), lambda s, i: (s, 0, 0))
    else:
        rows = NSUB * QBLK
        grid = (n_seq, seq_len // rows)
        qspec = lambda c: pl.BlockSpec((1, rows, OUT_A), lambda s, i: (s, i, c))
        kspec = lambda c: pl.BlockSpec((1, seq_len, OUT_A), lambda s, i: (s, 0, c))
        ospec = pl.BlockSpec((1, rows, OUT_A), lambda s, i: (s, i, 0))
        lspec = pl.BlockSpec((1, rows, LANES), lambda s, i: (s, i, 0))
    return pl.pallas_call(
        functools.partial(_mixa_kernel, seq_len=seq_len, win=win, batched=batched),
        grid=grid,
        in_specs=[qspec(0), kspec(1), kspec(2)],
        out_specs=[ospec, lspec],
        out_shape=[
            jax.ShapeDtypeStruct((n_seq, seq_len, OUT_A), BF16),
            jax.ShapeDtypeStruct((n_seq, seq_len, LANES), F32),
        ],
        compiler_params=_cparams(("arbitrary", "arbitrary")),
        name=f"mixa_len{seq_len}",
    )(qkv, qkv, qkv)


def _mixb_kernel(q_ref, k_ref, v_ref, o_ref):
    tq = q_ref.shape[0]
    lane = lax.broadcasted_iota(jnp.int32, (tq, LANES), 1)
    lo = lane < HEAD_DIM
    for p in range(HEADS_B_Q // 2):
        j = p // 2
        cs = slice(p * LANES, (p + 1) * LANES)
        qp = q_ref[:, cs]
        kd = k_ref[:, j * LANES:(j + 1) * LANES]
        vd = v_ref[:, j * LANES:(j + 1) * LANES]
        outs = []
        for hh in range(2):
            qh = jnp.where(lo if hh == 0 else jnp.logical_not(lo), qp, jnp.zeros_like(qp))
            s = lax.dot_general(qh, kd, (((1,), (1,)), ((), ())), preferred_element_type=F32)
            m = jnp.max(s, axis=-1, keepdims=True)
            e = jnp.exp(s - m)
            l = jnp.sum(e, axis=-1, keepdims=True)
            outs.append(jnp.dot(e.astype(BF16), vd, preferred_element_type=F32) * (1.0 / l))
        o_ref[:, cs] = jnp.where(lo, outs[0], outs[1]).astype(BF16)


def _mixb(qb, kvb, seq, tq):
    t = qb.shape[0]
    nq = seq // tq
    return pl.pallas_call(
        _mixb_kernel,
        grid=(t // seq, nq),
        in_specs=[
            pl.BlockSpec((tq, WIDTH_B_Q), lambda b, i: (b * nq + i, 0)),
            pl.BlockSpec((seq, 2 * LANES), lambda b, i: (b, 0)),
            pl.BlockSpec((seq, 2 * LANES), lambda b, i: (b, 1)),
        ],
        out_specs=pl.BlockSpec((tq, WIDTH_B_Q), lambda b, i: (b * nq + i, 0)),
        out_shape=jax.ShapeDtypeStruct((t, WIDTH_B_Q), BF16),
        compiler_params=_cparams(("arbitrary", "arbitrary")),
        name="mixb",
    )(qb, kvb, kvb)


def _merge_kernel(o1_ref, o2_ref, o3_ref, l1_ref, l2_ref, l3_ref, p4t_ref, p16t_ref, ob_ref, gl_ref, bg_ref,
                  x_ref, ex_ref, wpa_ref, wpb_ref, wo_ref, g2_ref, wr_ref, br_ref,
                  x1_ref, h2_ref, idx_ref, gw_ref):
    lane = lax.broadcasted_iota(jnp.int32, (TILE, LANES), 1)

    def split(w):
        hi = w.astype(BF16)
        return hi, (w - hi.astype(F32)).astype(BF16)

    def unperm(pt, val):
        return jnp.dot(pt, val, preferred_element_type=F32)

    def unperm_f32(pt, val):
        hi, lo_ = split(val)
        return unperm(pt, hi) + unperm(pt, lo_)

    p4t, p16t = p4t_ref[...], p16t_ref[...]
    o1 = o1_ref[...].astype(F32)
    o2 = unperm(p4t, o2_ref[0].reshape(TILE, OUT_A))
    o3 = unperm(p16t, o3_ref[0].reshape(TILE, OUT_A))
    l1 = l1_ref[...]
    l2 = unperm_f32(p4t, l2_ref[0].reshape(TILE, LANES))
    l3 = unperm_f32(p16t, l3_ref[0].reshape(TILE, LANES))
    mx = jnp.maximum(jnp.maximum(l1, l2), l3)
    e1, e2, e3 = jnp.exp(l1 - mx), jnp.exp(l2 - mx), jnp.exp(l3 - mx)
    rden = 1.0 / (e1 + e2 + e3)
    ex = ex_ref[...]

    def expand(w):
        hi, lo_ = split(w)
        return jnp.dot(hi, ex, preferred_element_type=F32) + jnp.dot(lo_, ex, preferred_element_type=F32)

    oa = expand(e1 * rden) * o1 + expand(e2 * rden) * o2 + expand(e3 * rden) * o3
    pa = jnp.dot(oa.astype(BF16), wpa_ref[...], preferred_element_type=F32)
    pb = jnp.dot(ob_ref[...], wpb_ref[...], preferred_element_type=F32)
    gl = gl_ref[...].astype(F32) + bg_ref[...]
    gates = 1.0 / (1.0 + jnp.exp(-gl))
    merged = gates[:, :D_MODEL] * pa + gates[:, D_MODEL:] * pb
    x1 = x_ref[...] + jnp.dot(merged.astype(BF16), wo_ref[...], preferred_element_type=F32)
    x1_ref[...] = x1
    ms = jnp.mean(x1 * x1, axis=-1, keepdims=True)
    h2 = x1 * lax.rsqrt(ms + EPS) * g2_ref[...]
    h2_ref[...] = h2
    h_hi, h_lo = split(h2)
    w_hi = wr_ref[0]
    w_lo = wr_ref[1]
    logits = (jnp.dot(h_hi, w_hi, preferred_element_type=F32)
              + jnp.dot(h_lo, w_hi, preferred_element_type=F32)
              + jnp.dot(h_hi, w_lo, preferred_element_type=F32)) + br_ref[...]
    work = logits
    lane_f = lane.astype(F32)
    vals, idxs = [], []
    for _ in range(TOP_K):
        m = jnp.max(work, axis=-1, keepdims=True)
        ix = jnp.min(jnp.where(work == m, lane_f, float(LANES)), axis=-1, keepdims=True)
        vals.append(m)
        idxs.append(ix)
        work = jnp.where(lane_f == ix, -jnp.inf, work)
    es = [jnp.exp(v - vals[0]) for v in vals]
    rsum = 1.0 / (es[0] + es[1] + es[2] + es[3])
    idx_out = jnp.zeros((TILE, LANES), F32)
    gw_out = jnp.zeros((TILE, LANES), F32)
    for k in range(TOP_K):
        idx_out = jnp.where(lane == k, idxs[k], idx_out)
        gw_out = jnp.where(lane == k, es[k] * rsum, gw_out)
    idx_ref[...] = idx_out.astype(jnp.int32)
    gw_ref[...] = gw_out


def _merge(o1, o2, o3, l1, l2, l3, p4t, p16t, ob, gl, bg, x2, ex, wpa, wpb, wo, g2, wr, br, seq):
    t = x2.shape[0]
    npos = seq // TILE
    row = lambda i: (i, 0)
    fixed = lambda i: (0, 0)
    d4, d16 = DIL_PAIRS[1][1], DIL_PAIRS[2][1]
    sub = lambda i: (i // npos, 0, i % npos, 0)
    return pl.pallas_call(
        _merge_kernel,
        grid=(t // TILE,),
        in_specs=[
            pl.BlockSpec((TILE, OUT_A), row),
            pl.BlockSpec((1, d4, TILE // d4, OUT_A), sub),
            pl.BlockSpec((1, d16, TILE // d16, OUT_A), sub),
            pl.BlockSpec((TILE, LANES), row),
            pl.BlockSpec((1, d4, TILE // d4, LANES), sub),
            pl.BlockSpec((1, d16, TILE // d16, LANES), sub),
            pl.BlockSpec((TILE, TILE), fixed),
            pl.BlockSpec((TILE, TILE), fixed),
            pl.BlockSpec((TILE, WIDTH_B_Q), row),
            pl.BlockSpec((TILE, 2 * D_MODEL), row),
            pl.BlockSpec((1, 2 * D_MODEL), fixed),
            pl.BlockSpec((TILE, D_MODEL), row),
            pl.BlockSpec((LANES, OUT_A), fixed),
            pl.BlockSpec((OUT_A, D_MODEL), fixed),
            pl.BlockSpec((WIDTH_B_Q, D_MODEL), fixed),
            pl.BlockSpec((D_MODEL, D_MODEL), fixed),
            pl.BlockSpec((1, D_MODEL), fixed),
            pl.BlockSpec((2, D_MODEL, LANES), lambda i: (0, 0, 0)),
            pl.BlockSpec((1, LANES), fixed),
        ],
        out_specs=[
            pl.BlockSpec((TILE, D_MODEL), row), pl.BlockSpec((TILE, D_MODEL), row),
            pl.BlockSpec((TILE, LANES), row), pl.BlockSpec((TILE, LANES), row),
        ],
        out_shape=[
            jax.ShapeDtypeStruct((t, D_MODEL), F32),
            jax.ShapeDtypeStruct((t, D_MODEL), F32),
            jax.ShapeDtypeStruct((t, LANES), jnp.int32),
            jax.ShapeDtypeStruct((t, LANES), F32),
        ],
        compiler_params=_cparams(("arbitrary",)),
        name="merge",
    )(o1, o2, o3, l1, l2, l3, p4t, p16t, ob, gl, bg, x2, ex, wpa, wpb, wo, g2, wr, br)


EXP_BM = 256


W_CHUNK = 256


def _experts_kernel(be_ref, nused_ref, tok_ref, tokn_ref, dst_ref,
                    h2_hbm, wgu_ref, bgu_ref, wd_ref, bd_ref,
                    y_hbm, xbuf, ybuf, wgu_bf, wd_bf, gsem, ssem):
    i = pl.program_id(0)
    nused = nused_ref[0]
    slot = i % 2

    def gather(tref, s):
        return [pltpu.make_async_copy(h2_hbm.at[pl.ds(tref[0, 0, r], 1)], xbuf.at[s, pl.ds(r, 1)], gsem.at[s])
                for r in range(EXP_BM)]

    def scatter():
        return [pltpu.make_async_copy(ybuf.at[pl.ds(r, 1)], y_hbm.at[pl.ds(dst_ref[0, 0, r], 1)], ssem.at[0])
                for r in range(EXP_BM)]

    @pl.when(i < nused)
    def _():
        @pl.when(i == 0)
        def _():
            for c in gather(tok_ref, 0):
                c.start()
            ybuf[...] = jnp.zeros_like(ybuf)
            dump = pltpu.make_async_copy(ybuf, y_hbm.at[pl.ds(y_hbm.shape[0] - EXP_BM, EXP_BM)], ssem.at[0])
            dump.start()
            dump.wait()

        @pl.when(i + 1 < nused)
        def _():
            for c in gather(tokn_ref, 1 - slot):
                c.start()

        @pl.when(jnp.logical_or(i == 0, be_ref[i] != be_ref[jnp.maximum(i - 1, 0)]))
        def _():
            for c in range(2 * D_FF // W_CHUNK):
                cs = slice(c * W_CHUNK, (c + 1) * W_CHUNK)
                wgu_bf[:, cs] = wgu_ref[0, :, cs].astype(BF16)
            for c in range(D_MODEL // W_CHUNK):
                cs = slice(c * W_CHUNK, (c + 1) * W_CHUNK)
                wd_bf[:, cs] = wd_ref[0, :, cs].astype(BF16)

        for c in gather(tok_ref, slot):
            c.wait()
        xb = xbuf[slot].astype(BF16)
        gu = jnp.dot(xb, wgu_bf[...], preferred_element_type=F32) + bgu_ref[0]
        gate = jnp.minimum(gu[:, :D_FF], SWIGLU_LIMIT)
        up = jnp.clip(gu[:, D_FF:], -SWIGLU_LIMIT, SWIGLU_LIMIT)
        act = (up + 1.0) * (gate * (1.0 / (1.0 + jnp.exp(-SWIGLU_ALPHA * gate))))
        y = jnp.dot(act.astype(BF16), wd_bf[...], preferred_element_type=F32) + bd_ref[0]

        @pl.when(i > 0)
        def _():
            for c in scatter():
                c.wait()

        ybuf[...] = y
        for c in scatter():
            c.start()

        @pl.when(i == nused - 1)
        def _():
            for c in scatter():
                c.wait()


def _experts(block_e, nused, tok_rows, dst_rows, h2, wgu, bgu, wd, bd, n_out_rows):
    n_blk = block_e.shape[0]
    d = h2.shape[1]
    idx_spec = lambda f: pl.BlockSpec((1, 1, EXP_BM), f, memory_space=pltpu.SMEM)
    grid_spec = pltpu.PrefetchScalarGridSpec(
        num_scalar_prefetch=2,
        grid=(n_blk,),
        in_specs=[
            idx_spec(lambda i, be, nu: (i, 0, 0)),
            idx_spec(lambda i, be, nu: (jnp.minimum(i + 1, n_blk - 1), 0, 0)),
            idx_spec(lambda i, be, nu: (i, 0, 0)),
            pl.BlockSpec(memory_space=pl.ANY),
            pl.BlockSpec((1, d, 2 * D_FF), lambda i, be, nu: (be[i], 0, 0)),
            pl.BlockSpec((1, 1, 2 * D_FF), lambda i, be, nu: (be[i], 0, 0)),
            pl.BlockSpec((1, D_FF, d), lambda i, be, nu: (be[i], 0, 0)),
            pl.BlockSpec((1, 1, d), lambda i, be, nu: (be[i], 0, 0)),
        ],
        out_specs=pl.BlockSpec(memory_space=pl.ANY),
        scratch_shapes=[
            pltpu.VMEM((2, EXP_BM, d), F32),
            pltpu.VMEM((EXP_BM, d), F32),
            pltpu.VMEM((d, 2 * D_FF), BF16),
            pltpu.VMEM((D_FF, d), BF16),
            pltpu.SemaphoreType.DMA((2,)),
            pltpu.SemaphoreType.DMA((1,)),
        ],
    )
    return pl.pallas_call(
        _experts_kernel,
        grid_spec=grid_spec,
        out_shape=jax.ShapeDtypeStruct((n_out_rows, d), F32),
        compiler_params=_cparams(("arbitrary",)),
        name="experts",
    )(block_e, nused, tok_rows, tok_rows, dst_rows, h2, wgu, bgu, wd, bd)


def _combine_kernel(y0_ref, y1_ref, y2_ref, y3_ref, gw_ref, x1_ref, o_ref):
    gw = gw_ref[...]
    acc = x1_ref[...]
    for k, y_ref in enumerate((y0_ref, y1_ref, y2_ref, y3_ref)):
        acc = acc + gw[:, k:k + 1] * y_ref[...]
    o_ref[...] = acc


def _combine(y, gw, x1):
    t = x1.shape[0]
    nt = t // TILE
    row = lambda i: (i, 0)
    yspec = lambda k: pl.BlockSpec((TILE, D_MODEL), lambda i: (k * nt + i, 0))
    return pl.pallas_call(
        _combine_kernel,
        grid=(nt,),
        in_specs=[yspec(0), yspec(1), yspec(2), yspec(3), pl.BlockSpec((TILE, LANES), row),
                  pl.BlockSpec((TILE, D_MODEL), row)],
        out_specs=pl.BlockSpec((TILE, D_MODEL), row),
        out_shape=jax.ShapeDtypeStruct((t, D_MODEL), F32),
        compiler_params=_cparams(("arbitrary",)),
        name="combine",
    )(y, y, y, y, gw, x1)


def _routing_plan(idx):
    t = idx.shape[0]
    a = t * TOP_K
    tok_bits = (t - 1).bit_length()
    key = ((idx << (tok_bits + 2)) | (jnp.arange(t, dtype=jnp.int32)[:, None] << 2)
           | jnp.arange(TOP_K, dtype=jnp.int32)[None, :])
    skey = jnp.sort(key.reshape(-1))
    bounds = jnp.searchsorted(skey, jnp.arange(N_EXPERTS + 1, dtype=jnp.int32) << (tok_bits + 2)).astype(jnp.int32)
    starts, counts = bounds[:-1], bounds[1:] - bounds[:-1]
    padded = ((counts + EXP_BM - 1) // EXP_BM) * EXP_BM
    pends = jnp.cumsum(padded)
    pstarts = pends - padded
    n_blk = a // EXP_BM + N_EXPERTS
    block_e = jnp.clip(jnp.searchsorted(pends, jnp.arange(n_blk, dtype=jnp.int32) * EXP_BM, side='right'),
                       0, N_EXPERTS - 1).astype(jnp.int32)
    nused = (pends[-1] // EXP_BM).astype(jnp.int32).reshape(1)
    r = jnp.arange(n_blk * EXP_BM, dtype=jnp.int32)
    e_r = jnp.repeat(block_e, EXP_BM)
    off = r - pstarts[e_r]
    valid = off < counts[e_r]
    kv = skey[jnp.clip(starts[e_r] + off, 0, a - 1)]
    tok = (kv >> 2) & ((1 << tok_bits) - 1)
    tok_rows = jnp.where(valid, tok, 0)
    dst_rows = jnp.where(valid, (kv & (TOP_K - 1)) * t + tok, a + r % EXP_BM)
    return block_e, nused, tok_rows.reshape(n_blk, 1, EXP_BM), dst_rows.reshape(n_blk, 1, EXP_BM)


def kernel(x, norm_mix_g, w_in, b_gate, qn_a, kn_a, qn_b, kn_b, w_proj_a, w_proj_b, w_out,
           norm_ffn_g, w_router, b_router, w_gate_up, b_gate_up, w_down, b_down):
    b, s, d = x.shape
    t = b * s
    x2 = x.reshape(t, d)
    tabs_a, tabs_b = _rope_tables(s)
    d4, d16 = DIL_PAIRS[1][1], DIL_PAIRS[2][1]

    gq = jnp.tile(qn_a[:, None, :], (1, HEADS_A, 1)).reshape(1, WIDTH_A) * SCALE
    gk = jnp.tile(kn_a[:, None, :], (1, HEADS_A, 1)).reshape(1, WIDTH_A)
    gqk = jnp.concatenate([gq, gk], axis=1)
    gqb = jnp.tile(qn_b, HEADS_B_Q).reshape(1, WIDTH_B_Q) * SCALE
    gkb = jnp.tile(kn_b, HEADS_B_KV).reshape(1, WIDTH_B_KV)
    hid = np.arange(MXU_N) // HEAD_DIM
    bd = jnp.asarray(hid[:, None] == hid[None, :], dtype=BF16)

    qkv1, qkv2, qkv3, qb, kvb, gl = _inproj(
        x2, norm_mix_g.reshape(1, d), w_in.astype(BF16), bd, _perm_matrix(d4), _perm_matrix(d16),
        gqk, gqb, gkb, tabs_a, tabs_b, b, s)

    o1, l1 = _mixa(qkv1.reshape(b, s, 3 * OUT_A), s)
    o2, l2 = _mixa(qkv2.reshape(b * d4, s // d4, 3 * OUT_A), s // d4)
    o3, l3 = _mixa(qkv3.reshape(b * d16, s // d16, 3 * OUT_A), s // d16)
    ob = _mixb(qb, kvb, s, 256)

    ex = jnp.asarray(np.arange(LANES)[:, None] == (np.arange(OUT_A) // HEAD_DIM)[None, :], dtype=BF16)
    wr_pad = jnp.zeros((d, LANES), F32).at[:, :N_EXPERTS].set(w_router)
    wr_hi = wr_pad.astype(BF16)
    wr = jnp.stack([wr_hi, (wr_pad - wr_hi.astype(F32)).astype(BF16)])
    br = jnp.full((1, LANES), NEG, F32).at[0, :N_EXPERTS].set(b_router)
    x1, h2, idx_full, gw = _merge(
        o1.reshape(t, OUT_A), o2.reshape(b, d4, s // d4, OUT_A), o3.reshape(b, d16, s // d16, OUT_A),
        l1.reshape(t, LANES), l2.reshape(b, d4, s // d4, LANES), l3.reshape(b, d16, s // d16, LANES),
        _perm_matrix(d4, True), _perm_matrix(d16, True), ob, gl, b_gate.reshape(1, 2 * d), x2, ex,
        w_proj_a.astype(BF16), w_proj_b.astype(BF16), w_out.astype(BF16), norm_ffn_g.reshape(1, d), wr, br, s)

    block_e, nused, tok_rows, dst_rows = _routing_plan(idx_full[:, :TOP_K])
    y = _experts(block_e, nused, tok_rows, dst_rows, h2, w_gate_up,
                 b_gate_up.reshape(N_EXPERTS, 1, 2 * D_FF), w_down,
                 b_down.reshape(N_EXPERTS, 1, d), t * TOP_K + EXP_BM)
    out = _combine(y, gw, x1)
    return out.reshape(b, s, d)
```

```python
import functools

import jax
import jax.numpy as jnp
import numpy as np
from jax import lax
from jax.experimental import pallas as pl
from jax.experimental.pallas import tpu as pltpu

F32 = jnp.float32
BF16 = jnp.bfloat16

D_MODEL = 1024
HEAD_DIM = 64
SCALE = HEAD_DIM ** -0.5
EPS = 1e-6
NEG = -1e30
DIL_PAIRS = ((128, 1), (512, 4), (2048, 16))
HALF_WIN = 64
N_GROUPS = 3
HEADS_A = 8
OUT_A = HEADS_A * HEAD_DIM
WIDTH_A = N_GROUPS * OUT_A
ROT_DIM_A = 16
THETA_PARTIAL = 500000.0
HEADS_B_Q = 8
HEADS_B_KV = 2
WIDTH_B_Q = 512
WIDTH_B_KV = 128
GRID_W = 64
THETA_AXIAL = 10000.0
N_EXPERTS = 32
TOP_K = 4
D_FF = 1024
SWIGLU_LIMIT = 7.0
SWIGLU_ALPHA = 1.702

LANES = 128
MXU_N = 256
VMEM_LIMIT = 56 * 1024 * 1024
TILE = 256
INPROJ_TM = 512
ROW_SUB = D_MODEL // LANES

COL_VA = 2 * WIDTH_A
COL_QB = 3 * WIDTH_A
COL_KB = COL_QB + WIDTH_B_Q
COL_GATE = COL_KB + 2 * WIDTH_B_KV


def _cparams(sem):
    return pltpu.CompilerParams(dimension_semantics=sem, vmem_limit_bytes=VMEM_LIMIT)


def _perm_rows(dil):
    n = np.arange(TILE)
    per = TILE // dil
    return (n % per) * dil + n // per


def _perm_matrix(dil, transpose=False):
    p = np.zeros((TILE, TILE), np.float32)
    p[np.arange(TILE), _perm_rows(dil)] = 1.0
    return jnp.asarray(p.T if transpose else p, dtype=BF16)


def _rope_tables(seq):
    pos = np.arange(seq, dtype=np.float64)[:, None]
    d = np.arange(LANES) % HEAD_DIM
    half = ROT_DIM_A // 2
    inv = THETA_PARTIAL ** (-((d % half) / half))
    ang = pos * inv[None, :]
    in_rot = (d < ROT_DIM_A)[None, :]
    first = (d < half)[None, :]
    second = ((d >= half) & (d < ROT_DIM_A))[None, :]
    ta = np.stack([np.where(in_rot, np.cos(ang), 1.0), np.where(first, -np.sin(ang), 0.0),
                   np.where(second, np.sin(ang), 0.0)])
    per_group = []
    for _, dil in DIL_PAIRS:
        order = (np.arange(seq // TILE) * TILE)[:, None] + _perm_rows(dil)[None, :]
        per_group.append(ta[:, order.reshape(-1), :])
    hb = HEAD_DIM // 4
    row = np.floor(pos / GRID_W)
    col = pos - row * GRID_W
    invb = THETA_AXIAL ** (-((d % hb) / hb))
    angb = np.where((d < HEAD_DIM // 2)[None, :], row, col) * invb[None, :]
    firstb = ((d % (2 * hb)) < hb)[None, :]
    tb = np.stack([np.cos(angb), np.where(firstb, -np.sin(angb), 0.0), np.where(firstb, 0.0, np.sin(angb))])
    return jnp.asarray(np.stack(per_group), F32), jnp.asarray(tb, F32)


def _inproj_kernel(x_ref, g_ref, w_ref, bd_ref, p4_ref, p16_ref, gqk_ref, gqb_ref, gkb_ref, ta_ref, tb_ref,
                   qkv1_ref, qkv2_ref, qkv3_ref, qb_ref, kvb_ref, gate_ref):
    x = x_ref[...]
    tm = x.shape[0]
    nsub = tm // TILE
    ms = jnp.mean(x * x, axis=-1, keepdims=True)
    h = (x * lax.rsqrt(ms + EPS) * g_ref[...]).astype(BF16)

    def regroup(p_ref):
        return jnp.concatenate(
            [jnp.dot(p_ref[...], h[s * TILE:(s + 1) * TILE], preferred_element_type=F32) for s in range(nsub)],
            axis=0).astype(BF16)

    hg = (h, regroup(p4_ref), regroup(p16_ref))
    bd = bd_ref[...]
    lane = lax.broadcasted_iota(jnp.int32, (tm, LANES), 1)
    lo = lane < HEAD_DIM

    def proj(lhs, c0):
        return jnp.dot(lhs, w_ref[:, c0:c0 + MXU_N], preferred_element_type=F32)

    def head_norm(y, gain, bdm):
        ss = jnp.dot((y * y).astype(BF16), bdm, preferred_element_type=F32)
        return y * lax.rsqrt(ss * (1.0 / HEAD_DIM) + EPS) * gain

    def rope(z, tab, sh):
        return z * tab[0] + pltpu.roll(z, LANES - sh, 1) * tab[1] + pltpu.roll(z, sh, 1) * tab[2]

    sh_a = ROT_DIM_A // 2
    sh_b = HEAD_DIM // 4

    def store_group(gi, c_out, val):
        if gi == 0:
            qkv1_ref[:, c_out:c_out + LANES] = val
        else:
            ref, dil = ((qkv2_ref, DIL_PAIRS[1][1]), (qkv3_ref, DIL_PAIRS[2][1]))[gi - 1]
            per = TILE // dil
            for s in range(nsub):
                ref[0, :, s * per:(s + 1) * per, c_out:c_out + LANES] = (
                    val[s * TILE:(s + 1) * TILE].reshape(dil, per, LANES))

    for gi in range(N_GROUPS):
        tab = (ta_ref[gi, 0], ta_ref[gi, 1], ta_ref[gi, 2])
        for which in range(3):
            for cc in range(OUT_A // MXU_N):
                c_in = which * WIDTH_A + gi * OUT_A + cc * MXU_N
                y = proj(hg[gi], c_in)
                if which < 2:
                    y = head_norm(y, gqk_ref[:, c_in:c_in + MXU_N], bd)
                for hf in range(2):
                    z = y[:, hf * LANES:(hf + 1) * LANES]
                    if which < 2:
                        z = rope(z, tab, sh_a)
                    store_group(gi, which * OUT_A + cc * MXU_N + hf * LANES, z.astype(BF16))

    tabb = (tb_ref[0], tb_ref[1], tb_ref[2])
    for c in range(WIDTH_B_Q // MXU_N):
        c0 = c * MXU_N
        yn = head_norm(proj(h, COL_QB + c0), gqb_ref[:, c0:c0 + MXU_N], bd)
        for hf in range(2):
            z = yn[:, hf * LANES:(hf + 1) * LANES]
            qb_ref[:, c0 + hf * LANES:c0 + (hf + 1) * LANES] = rope(z, tabb, sh_b).astype(BF16)
    ykv = proj(h, COL_KB)
    kb = rope(head_norm(ykv[:, :LANES], gkb_ref[...], bd[:LANES, :LANES]), tabb, sh_b)
    vb = ykv[:, LANES:]
    for j, t in enumerate((kb, vb)):
        sw = pltpu.roll(t, HEAD_DIM, 1)
        kvb_ref[:, (2 * j) * LANES:(2 * j + 1) * LANES] = jnp.where(lo, t, sw).astype(BF16)
        kvb_ref[:, (2 * j + 1) * LANES:(2 * j + 2) * LANES] = jnp.where(lo, sw, t).astype(BF16)
    for c in range(2 * D_MODEL // MXU_N):
        c0 = c * MXU_N
        gate_ref[:, c0:c0 + MXU_N] = proj(h, COL_GATE + c0).astype(BF16)


def _inproj(x2, g, w_bf, bd, p4, p16, gqk, gqb, gkb, tabs_a, tabs_b, batch, seq):
    t = x2.shape[0]
    n_cols = w_bf.shape[1]
    tm = INPROJ_TM
    npos = seq // tm
    row = lambda i: (i, 0)
    fixed = lambda i: (0, 0)
    d4, d16 = DIL_PAIRS[1][1], DIL_PAIRS[2][1]
    sub = lambda i: (i // npos, 0, i % npos, 0)
    return pl.pallas_call(
        _inproj_kernel,
        grid=(t // tm,),
        in_specs=[
            pl.BlockSpec((tm, D_MODEL), row),
            pl.BlockSpec((1, D_MODEL), fixed),
            pl.BlockSpec((D_MODEL, n_cols), fixed, pipeline_mode=pl.Buffered(1)),
            pl.BlockSpec((MXU_N, MXU_N), fixed),
            pl.BlockSpec((TILE, TILE), fixed),
            pl.BlockSpec((TILE, TILE), fixed),
            pl.BlockSpec((1, 2 * WIDTH_A), fixed),
            pl.BlockSpec((1, WIDTH_B_Q), fixed),
            pl.BlockSpec((1, WIDTH_B_KV), fixed),
            pl.BlockSpec((N_GROUPS, 3, tm, LANES), lambda i: (0, 0, i % npos, 0)),
            pl.BlockSpec((3, tm, LANES), lambda i: (0, i % npos, 0)),
        ],
        out_specs=[
            pl.BlockSpec((tm, 3 * OUT_A), row),
            pl.BlockSpec((1, d4, tm // d4, 3 * OUT_A), sub),
            pl.BlockSpec((1, d16, tm // d16, 3 * OUT_A), sub),
            pl.BlockSpec((tm, WIDTH_B_Q), row),
            pl.BlockSpec((tm, 4 * LANES), row),
            pl.BlockSpec((tm, 2 * D_MODEL), row),
        ],
        out_shape=[
            jax.ShapeDtypeStruct((t, 3 * OUT_A), BF16),
            jax.ShapeDtypeStruct((batch, d4, seq // d4, 3 * OUT_A), BF16),
            jax.ShapeDtypeStruct((batch, d16, seq // d16, 3 * OUT_A), BF16),
            jax.ShapeDtypeStruct((t, WIDTH_B_Q), BF16),
            jax.ShapeDtypeStruct((t, 4 * LANES), BF16),
            jax.ShapeDtypeStruct((t, 2 * D_MODEL), BF16),
        ],
        compiler_params=_cparams(("arbitrary",)),
        name="inproj",
    )(x2, g, w_bf, bd, p4, p16, gqk, gqb, gkb, tabs_a, tabs_b)


QBLK = 128
NSUB = 4


def _mixa_kernel(q_ref, k_ref, v_ref, o_ref, lse_ref, *, seq_len, win, batched):
    step = pl.program_id(1)
    lane = lax.broadcasted_iota(jnp.int32, (QBLK, LANES), 1)
    lo = lane < HEAD_DIM
    qi = lax.broadcasted_iota(jnp.int32, (QBLK, win), 0)
    ki = lax.broadcasted_iota(jnp.int32, (QBLK, win), 1)
    for b in range(NSUB):
        if batched:
            sq, r0, blk = b, 0, 0
            kstart = 0
        else:
            sq, r0 = 0, b * QBLK
            blk = step * NSUB + b
            kstart = pl.multiple_of(jnp.clip(blk * QBLK - HALF_WIN, 0, seq_len - win), HALF_WIN)
        valid = jnp.abs((ki + kstart) - (qi + blk * QBLK)) <= HALF_WIN
        scores, vals = [], []
        for p in range(HEADS_A // 2):
            cs = slice(p * LANES, (p + 1) * LANES)
            qp = q_ref[sq, r0:r0 + QBLK, cs]
            kp = k_ref[sq, pl.ds(kstart, win), cs]
            vals.append(v_ref[sq, pl.ds(kstart, win), cs])
            for hh in range(2):
                qh = jnp.where(lo if hh == 0 else jnp.logical_not(lo), qp, jnp.zeros_like(qp))
                scores.append(lax.dot_general(qh, kp, (((1,), (1,)), ((), ())), preferred_element_type=F32))
        s = jnp.where(valid[None], jnp.stack(scores), NEG)
        m = jnp.max(s, axis=-1, keepdims=True)
        e = jnp.exp(s - m)
        l = jnp.sum(e, axis=-1, keepdims=True)
        rl = 1.0 / l
        lse = m + jnp.log(l)
        eb = e.astype(BF16)
        lse_acc = jnp.zeros((QBLK, LANES), F32)
        for p in range(HEADS_A // 2):
            outs = [jnp.dot(eb[2 * p + hh], vals[p], preferred_element_type=F32) * rl[2 * p + hh] for hh in range(2)]
            o_ref[sq, r0:r0 + QBLK, p * LANES:(p + 1) * LANES] = jnp.where(lo, outs[0], outs[1]).astype(BF16)
            for hh in range(2):
                lse_acc = jnp.where(lane == 2 * p + hh, lse[2 * p + hh], lse_acc)
        lse_ref[sq, r0:r0 + QBLK, :] = lse_acc


def _mixa(qkv, seq_len):
    n_seq = qkv.shape[0]
    win = min(2 * QBLK, seq_len)
    batched = seq_len == QBLK
    if batched:
        grid = (n_seq // NSUB, 1)
        qspec = lambda c: pl.BlockSpec((NSUB, QBLK, OUT_A), lambda s, i: (s, 0, c))
        kspec = qspec
        ospec = pl.BlockSpec((NSUB, QBLK, OUT_A), lambda s, i: (s, 0, 0))
        lspec = pl.BlockSpec((NSUB, QBLK, LANES), lambda s, i: (s, 0, 0))
    else:
        rows = NSUB * QBLK
        grid = (n_seq, seq_len // rows)
        qspec = lambda c: pl.BlockSpec((1, rows, OUT_A), lambda s, i: (s, i, c))
        kspec = lambda c: pl.BlockSpec((1, seq_len, OUT_A), lambda s, i: (s, 0, c))
        ospec = pl.BlockSpec((1, rows, OUT_A), lambda s, i: (s, i, 0))
        lspec = pl.BlockSpec((1, rows, LANES), lambda s, i: (s, i, 0))
    return pl.pallas_call(
        functools.partial(_mixa_kernel, seq_len=seq_len, win=win, batched=batched),
        grid=grid,
        in_specs=[qspec(0), kspec(1), kspec(2)],
        out_specs=[ospec, lspec],
        out_shape=[
            jax.ShapeDtypeStruct((n_seq, seq_len, OUT_A), BF16),
            jax.ShapeDtypeStruct((n_seq, seq_len, LANES), F32),
        ],
        compiler_params=_cparams(("arbitrary", "arbitrary")),
        name=f"mixa_len{seq_len}",
    )(qkv, qkv, qkv)


def _mixb_kernel(q_ref, k_ref, v_ref, o_ref):
    tq = q_ref.shape[0]
    lane = lax.broadcasted_iota(jnp.int32, (tq, LANES), 1)
    lo = lane < HEAD_DIM
    for p in range(HEADS_B_Q // 2):
        j = p // 2
        cs = slice(p * LANES, (p + 1) * LANES)
        qp = q_ref[:, cs]
        kd = k_ref[:, j * LANES:(j + 1) * LANES]
        vd = v_ref[:, j * LANES:(j + 1) * LANES]
        outs = []
        for hh in range(2):
            qh = jnp.where(lo if hh == 0 else jnp.logical_not(lo), qp, jnp.zeros_like(qp))
            s = lax.dot_general(qh, kd, (((1,), (1,)), ((), ())), preferred_element_type=F32)
            m = jnp.max(s, axis=-1, keepdims=True)
            e = jnp.exp(s - m)
            l = jnp.sum(e, axis=-1, keepdims=True)
            outs.append(jnp.dot(e.astype(BF16), vd, preferred_element_type=F32) * (1.0 / l))
        o_ref[:, cs] = jnp.where(lo, outs[0], outs[1]).astype(BF16)


def _mixb(qb, kvb, seq, tq):
    t = qb.shape[0]
    nq = seq // tq
    return pl.pallas_call(
        _mixb_kernel,
        grid=(t // seq, nq),
        in_specs=[
            pl.BlockSpec((tq, WIDTH_B_Q), lambda b, i: (b * nq + i, 0)),
            pl.BlockSpec((seq, 2 * LANES), lambda b, i: (b, 0)),
            pl.BlockSpec((seq, 2 * LANES), lambda b, i: (b, 1)),
        ],
        out_specs=pl.BlockSpec((tq, WIDTH_B_Q), lambda b, i: (b * nq + i, 0)),
        out_shape=jax.ShapeDtypeStruct((t, WIDTH_B_Q), BF16),
        compiler_params=_cparams(("arbitrary", "arbitrary")),
        name="mixb",
    )(qb, kvb, kvb)


def _merge_kernel(o1_ref, o2_ref, o3_ref, l1_ref, l2_ref, l3_ref, p4t_ref, p16t_ref, ob_ref, gl_ref, bg_ref,
                  x_ref, ex_ref, wpa_ref, wpb_ref, wo_ref, g2_ref, wr_ref, br_ref,
                  x1_ref, h2_ref, idx_ref, gw_ref):
    lane = lax.broadcasted_iota(jnp.int32, (TILE, LANES), 1)

    def split(w):
        hi = w.astype(BF16)
        return hi, (w - hi.astype(F32)).astype(BF16)

    def unperm(pt, val):
        return jnp.dot(pt, val, preferred_element_type=F32)

    def unperm_f32(pt, val):
        hi, lo_ = split(val)
        return unperm(pt, hi) + unperm(pt, lo_)

    p4t, p16t = p4t_ref[...], p16t_ref[...]
    o1 = o1_ref[...].astype(F32)
    o2 = unperm(p4t, o2_ref[0].reshape(TILE, OUT_A))
    o3 = unperm(p16t, o3_ref[0].reshape(TILE, OUT_A))
    l1 = l1_ref[...]
    l2 = unperm_f32(p4t, l2_ref[0].reshape(TILE, LANES))
    l3 = unperm_f32(p16t, l3_ref[0].reshape(TILE, LANES))
    mx = jnp.maximum(jnp.maximum(l1, l2), l3)
    e1, e2, e3 = jnp.exp(l1 - mx), jnp.exp(l2 - mx), jnp.exp(l3 - mx)
    rden = 1.0 / (e1 + e2 + e3)
    ex = ex_ref[...]

    def expand(w):
        hi, lo_ = split(w)
        return jnp.dot(hi, ex, preferred_element_type=F32) + jnp.dot(lo_, ex, preferred_element_type=F32)

    oa = expand(e1 * rden) * o1 + expand(e2 * rden) * o2 + expand(e3 * rden) * o3
    pa = jnp.dot(oa.astype(BF16), wpa_ref[...], preferred_element_type=F32)
    pb = jnp.dot(ob_ref[...], wpb_ref[...], preferred_element_type=F32)
    gl = gl_ref[...].astype(F32) + bg_ref[...]
    gates = 1.0 / (1.0 + jnp.exp(-gl))
    merged = gates[:, :D_MODEL] * pa + gates[:, D_MODEL:] * pb
    x1 = x_ref[...] + jnp.dot(merged.astype(BF16), wo_ref[...], preferred_element_type=F32)
    x1_ref[...] = x1
    ms = jnp.mean(x1 * x1, axis=-1, keepdims=True)
    h2 = x1 * lax.rsqrt(ms + EPS) * g2_ref[...]
    for c in range(D_MODEL // LANES):
        h2_ref[pl.ds(c, TILE, stride=ROW_SUB), :] = h2[:, c * LANES:(c + 1) * LANES]
    h_hi, h_lo = split(h2)
    w_hi = wr_ref[0]
    w_lo = wr_ref[1]
    logits = (jnp.dot(h_hi, w_hi, preferred_element_type=F32)
              + jnp.dot(h_lo, w_hi, preferred_element_type=F32)
              + jnp.dot(h_hi, w_lo, preferred_element_type=F32)) + br_ref[...]
    work = logits
    lane_f = lane.astype(F32)
    vals, idxs = [], []
    for _ in range(TOP_K):
        m = jnp.max(work, axis=-1, keepdims=True)
        ix = jnp.min(jnp.where(work == m, lane_f, float(LANES)), axis=-1, keepdims=True)
        vals.append(m)
        idxs.append(ix)
        work = jnp.where(lane_f == ix, -jnp.inf, work)
    es = [jnp.exp(v - vals[0]) for v in vals]
    rsum = 1.0 / (es[0] + es[1] + es[2] + es[3])
    idx_out = jnp.zeros((TILE, LANES), F32)
    gw_out = jnp.zeros((TILE, LANES), F32)
    for k in range(TOP_K):
        idx_out = jnp.where(lane == k, idxs[k], idx_out)
        gw_out = jnp.where(lane == k, es[k] * rsum, gw_out)
    idx_ref[...] = idx_out.astype(jnp.int32)
    gw_ref[...] = gw_out


def _merge(o1, o2, o3, l1, l2, l3, p4t, p16t, ob, gl, bg, x2, ex, wpa, wpb, wo, g2, wr, br, seq):
    t = x2.shape[0]
    npos = seq // TILE
    row = lambda i: (i, 0)
    fixed = lambda i: (0, 0)
    d4, d16 = DIL_PAIRS[1][1], DIL_PAIRS[2][1]
    sub = lambda i: (i // npos, 0, i % npos, 0)
    return pl.pallas_call(
        _merge_kernel,
        grid=(t // TILE,),
        in_specs=[
            pl.BlockSpec((TILE, OUT_A), row),
            pl.BlockSpec((1, d4, TILE // d4, OUT_A), sub),
            pl.BlockSpec((1, d16, TILE // d16, OUT_A), sub),
            pl.BlockSpec((TILE, LANES), row),
            pl.BlockSpec((1, d4, TILE // d4, LANES), sub),
            pl.BlockSpec((1, d16, TILE // d16, LANES), sub),
            pl.BlockSpec((TILE, TILE), fixed),
            pl.BlockSpec((TILE, TILE), fixed),
            pl.BlockSpec((TILE, WIDTH_B_Q), row),
            pl.BlockSpec((TILE, 2 * D_MODEL), row),
            pl.BlockSpec((1, 2 * D_MODEL), fixed),
            pl.BlockSpec((TILE, D_MODEL), row),
            pl.BlockSpec((LANES, OUT_A), fixed),
            pl.BlockSpec((OUT_A, D_MODEL), fixed),
            pl.BlockSpec((WIDTH_B_Q, D_MODEL), fixed),
            pl.BlockSpec((D_MODEL, D_MODEL), fixed),
            pl.BlockSpec((1, D_MODEL), fixed),
            pl.BlockSpec((2, D_MODEL, LANES), lambda i: (0, 0, 0)),
            pl.BlockSpec((1, LANES), fixed),
        ],
        out_specs=[
            pl.BlockSpec((TILE, D_MODEL), row), pl.BlockSpec((TILE * ROW_SUB, LANES), row),
            pl.BlockSpec((TILE, LANES), row), pl.BlockSpec((TILE, LANES), row),
        ],
        out_shape=[
            jax.ShapeDtypeStruct((t, D_MODEL), F32),
            jax.ShapeDtypeStruct((t * ROW_SUB, LANES), F32),
            jax.ShapeDtypeStruct((t, LANES), jnp.int32),
            jax.ShapeDtypeStruct((t, LANES), F32),
        ],
        compiler_params=_cparams(("arbitrary",)),
        name="merge",
    )(o1, o2, o3, l1, l2, l3, p4t, p16t, ob, gl, bg, x2, ex, wpa, wpb, wo, g2, wr, br)


EXP_BM = 256


W_CHUNK = 256


FF_CHUNK = 256
N_DUMP = 2


def _experts_kernel(be_ref, nused_ref, tok_ref, tokn_ref, dst_ref, dstp_ref,
                    h2_hbm, wgu_ref, bgu_ref, wd_ref, bd_ref,
                    y_hbm, xbuf, ybuf, wgu_bf, wd_bf, gsem, ssem):
    i = pl.program_id(0)
    nused = nused_ref[0]
    n_real = y_hbm.shape[0] - N_DUMP * EXP_BM

    def gather(tok, s, r):
        return pltpu.make_async_copy(h2_hbm.at[tok], xbuf.at[s, pl.ds(r * ROW_SUB, ROW_SUB)], gsem.at[s])

    def scatter(s, dst, r):
        return pltpu.make_async_copy(ybuf.at[s, pl.ds(r * ROW_SUB, ROW_SUB)], y_hbm.at[dst], ssem.at[s])

    @pl.when(i == 0)
    def _():
        ybuf[...] = jnp.zeros_like(ybuf)
        for r in range(EXP_BM):
            gather(tok_ref[0, 0, r], 0, r).start()
            scatter(0, n_real + r, r).start()

    @pl.when(jnp.logical_and(i < nused, jnp.logical_or(i == 0, be_ref[i] != be_ref[jnp.maximum(i - 1, 0)])))
    def _():
        for c in range(2 * D_FF // W_CHUNK):
            cs = slice(c * W_CHUNK, (c + 1) * W_CHUNK)
            wgu_bf[:, cs] = wgu_ref[0, :, cs].astype(BF16)
        for c in range(D_MODEL // W_CHUNK):
            cs = slice(c * W_CHUNK, (c + 1) * W_CHUNK)
            wd_bf[:, cs] = wd_ref[0, :, cs].astype(BF16)

    def block(slot):
        other = 1 - slot
        for r in range(EXP_BM):
            gather(0, slot, r).wait()
        xb = jnp.concatenate([xbuf[slot, pl.ds(c, EXP_BM, stride=ROW_SUB), :] for c in range(ROW_SUB)],
                             axis=1).astype(BF16)
        n_chunks = D_FF // FF_CHUNK
        per = EXP_BM // n_chunks
        y = None
        for f in range(n_chunks):
            gs = slice(f * FF_CHUNK, (f + 1) * FF_CHUNK)
            us = slice(D_FF + f * FF_CHUNK, D_FF + (f + 1) * FF_CHUNK)
            g = jnp.dot(xb, wgu_bf[:, gs], preferred_element_type=F32) + bgu_ref[0, :, gs]
            u = jnp.dot(xb, wgu_bf[:, us], preferred_element_type=F32) + bgu_ref[0, :, us]
            gate = jnp.minimum(g, SWIGLU_LIMIT)
            up = jnp.clip(u, -SWIGLU_LIMIT, SWIGLU_LIMIT)
            act = (up + 1.0) * (gate * (1.0 / (1.0 + jnp.exp(-SWIGLU_ALPHA * gate))))
            part = jnp.dot(act.astype(BF16), wd_bf[gs, :], preferred_element_type=F32)
            y = part if y is None else y + part
            for r in range(f * per, (f + 1) * per):
                gather(tokn_ref[0, 0, r], other, r).start()
                scatter(other, dstp_ref[0, 0, r], r).start()
        y = y + bd_ref[0]

        for r in range(EXP_BM):
            scatter(slot, 0, r).wait()
        for c in range(ROW_SUB):
            ybuf[slot, pl.ds(c, EXP_BM, stride=ROW_SUB), :] = y[:, c * LANES:(c + 1) * LANES]

        @pl.when(i == nused - 1)
        def _():
            for r in range(EXP_BM):
                scatter(slot, dst_ref[0, 0, r], r).start()
            for r in range(EXP_BM):
                scatter(other, 0, r).wait()
                scatter(slot, 0, r).wait()
                gather(0, other, r).wait()

    for parity in range(2):
        pl.when(jnp.logical_and(i < nused, i % 2 == parity))(functools.partial(block, parity))


def _experts(block_e, nused, tok_rows, dst_rows, dst_prev, h2, wgu, bgu, wd, bd, n_out_rows):
    n_blk = block_e.shape[0]
    d = D_MODEL
    idx_spec = lambda f: pl.BlockSpec((1, 1, EXP_BM), f, memory_space=pltpu.SMEM)
    grid_spec = pltpu.PrefetchScalarGridSpec(
        num_scalar_prefetch=2,
        grid=(n_blk,),
        in_specs=[
            idx_spec(lambda i, be, nu: (i, 0, 0)),
            idx_spec(lambda i, be, nu: (jnp.minimum(i + 1, n_blk - 1), 0, 0)),
            idx_spec(lambda i, be, nu: (i, 0, 0)),
            idx_spec(lambda i, be, nu: (i, 0, 0)),
            pl.BlockSpec(memory_space=pl.ANY),
            pl.BlockSpec((1, d, 2 * D_FF), lambda i, be, nu: (be[i], 0, 0)),
            pl.BlockSpec((1, 1, 2 * D_FF), lambda i, be, nu: (be[i], 0, 0)),
            pl.BlockSpec((1, D_FF, d), lambda i, be, nu: (be[i], 0, 0)),
            pl.BlockSpec((1, 1, d), lambda i, be, nu: (be[i], 0, 0)),
        ],
        out_specs=pl.BlockSpec(memory_space=pl.ANY),
        scratch_shapes=[
            pltpu.VMEM((2, EXP_BM * ROW_SUB, LANES), F32),
            pltpu.VMEM((2, EXP_BM * ROW_SUB, LANES), F32),
            pltpu.VMEM((d, 2 * D_FF), BF16),
            pltpu.VMEM((D_FF, d), BF16),
            pltpu.SemaphoreType.DMA((2,)),
            pltpu.SemaphoreType.DMA((2,)),
        ],
    )
    return pl.pallas_call(
        _experts_kernel,
        grid_spec=grid_spec,
        out_shape=jax.ShapeDtypeStruct((n_out_rows, ROW_SUB, LANES), F32),
        compiler_params=_cparams(("arbitrary",)),
        name="experts",
    )(block_e, nused, tok_rows, tok_rows, dst_rows, dst_prev, h2, wgu, bgu, wd, bd)


def _combine_kernel(y0_ref, y1_ref, y2_ref, y3_ref, gw_ref, x1_ref, o_ref):
    gw = gw_ref[...]
    for c in range(ROW_SUB):
        cs = slice(c * LANES, (c + 1) * LANES)
        acc = x1_ref[:, cs]
        for k, y_ref in enumerate((y0_ref, y1_ref, y2_ref, y3_ref)):
            acc = acc + gw[:, k:k + 1] * y_ref[pl.ds(c, TILE, stride=ROW_SUB), :]
        o_ref[:, cs] = acc


def _combine(y, gw, x1):
    t = x1.shape[0]
    nt = t // TILE
    row = lambda i: (i, 0)
    yspec = lambda k: pl.BlockSpec((TILE * ROW_SUB, LANES), lambda i: (k * nt + i, 0))
    return pl.pallas_call(
        _combine_kernel,
        grid=(nt,),
        in_specs=[yspec(0), yspec(1), yspec(2), yspec(3), pl.BlockSpec((TILE, LANES), row),
                  pl.BlockSpec((TILE, D_MODEL), row)],
        out_specs=pl.BlockSpec((TILE, D_MODEL), row),
        out_shape=jax.ShapeDtypeStruct((t, D_MODEL), F32),
        compiler_params=_cparams(("arbitrary",)),
        name="combine",
    )(y, y, y, y, gw, x1)


def _routing_plan(idx):
    t = idx.shape[0]
    a = t * TOP_K
    tok_bits = (t - 1).bit_length()
    key = ((idx << (tok_bits + 2)) | (jnp.arange(t, dtype=jnp.int32)[:, None] << 2)
           | jnp.arange(TOP_K, dtype=jnp.int32)[None, :])
    skey = jnp.sort(key.reshape(-1))
    bounds = jnp.searchsorted(skey, jnp.arange(N_EXPERTS + 1, dtype=jnp.int32) << (tok_bits + 2),
                              method='compare_all').astype(jnp.int32)
    starts, counts = bounds[:-1], bounds[1:] - bounds[:-1]
    padded = ((counts + EXP_BM - 1) // EXP_BM) * EXP_BM
    pends = jnp.cumsum(padded)
    pstarts = pends - padded
    n_blk = a // EXP_BM + N_EXPERTS
    blk = jnp.arange(n_blk, dtype=jnp.int32)
    block_e = jnp.minimum(jnp.sum((pends[None, :] <= (blk * EXP_BM)[:, None]).astype(jnp.int32), axis=1),
                          N_EXPERTS - 1)
    nused = (pends[-1] // EXP_BM).astype(jnp.int32).reshape(1)
    sel = block_e[:, None] == jnp.arange(N_EXPERTS, dtype=jnp.int32)[None, :]
    pick = lambda tab: jnp.sum(jnp.where(sel, tab[None, :], 0), axis=1)
    first = blk * EXP_BM - pick(pstarts)
    q = jnp.arange(EXP_BM, dtype=jnp.int32)[None, :]
    valid = q < (pick(counts) - first)[:, None]
    kv = skey[jnp.clip((pick(starts) + first)[:, None] + q, 0, a - 1)]
    tok = (kv >> 2) & ((1 << tok_bits) - 1)
    tok_rows = jnp.where(valid, tok, 0)
    dump = a + (blk % N_DUMP)[:, None] * EXP_BM + q
    dst_rows = jnp.where(valid, (kv & (TOP_K - 1)) * t + tok, dump)
    dst_prev = jnp.concatenate([a + EXP_BM + q, dst_rows[:-1]], axis=0)
    shape = (n_blk, 1, EXP_BM)
    return block_e, nused, tok_rows.reshape(shape), dst_rows.reshape(shape), dst_prev.reshape(shape)


def kernel(x, norm_mix_g, w_in, b_gate, qn_a, kn_a, qn_b, kn_b, w_proj_a, w_proj_b, w_out,
           norm_ffn_g, w_router, b_router, w_gate_up, b_gate_up, w_down, b_down):
    b, s, d = x.shape
    t = b * s
    x2 = x.reshape(t, d)
    tabs_a, tabs_b = _rope_tables(s)
    d4, d16 = DIL_PAIRS[1][1], DIL_PAIRS[2][1]

    gq = jnp.tile(qn_a[:, None, :], (1, HEADS_A, 1)).reshape(1, WIDTH_A) * SCALE
    gk = jnp.tile(kn_a[:, None, :], (1, HEADS_A, 1)).reshape(1, WIDTH_A)
    gqk = jnp.concatenate([gq, gk], axis=1)
    gqb = jnp.tile(qn_b, HEADS_B_Q).reshape(1, WIDTH_B_Q) * SCALE
    gkb = jnp.tile(kn_b, HEADS_B_KV).reshape(1, WIDTH_B_KV)
    hid = np.arange(MXU_N) // HEAD_DIM
    bd = jnp.asarray(hid[:, None] == hid[None, :], dtype=BF16)

    qkv1, qkv2, qkv3, qb, kvb, gl = _inproj(
        x2, norm_mix_g.reshape(1, d), w_in.astype(BF16), bd, _perm_matrix(d4), _perm_matrix(d16),
        gqk, gqb, gkb, tabs_a, tabs_b, b, s)

    o1, l1 = _mixa(qkv1.reshape(b, s, 3 * OUT_A), s)
    o2, l2 = _mixa(qkv2.reshape(b * d4, s // d4, 3 * OUT_A), s // d4)
    o3, l3 = _mixa(qkv3.reshape(b * d16, s // d16, 3 * OUT_A), s // d16)
    ob = _mixb(qb, kvb, s, 256)

    ex = jnp.asarray(np.arange(LANES)[:, None] == (np.arange(OUT_A) // HEAD_DIM)[None, :], dtype=BF16)
    wr_pad = jnp.zeros((d, LANES), F32).at[:, :N_EXPERTS].set(w_router)
    wr_hi = wr_pad.astype(BF16)
    wr = jnp.stack([wr_hi, (wr_pad - wr_hi.astype(F32)).astype(BF16)])
    br = jnp.full((1, LANES), NEG, F32).at[0, :N_EXPERTS].set(b_router)
    x1, h2, idx_full, gw = _merge(
        o1.reshape(t, OUT_A), o2.reshape(b, d4, s // d4, OUT_A), o3.reshape(b, d16, s // d16, OUT_A),
        l1.reshape(t, LANES), l2.reshape(b, d4, s // d4, LANES), l3.reshape(b, d16, s // d16, LANES),
        _perm_matrix(d4, True), _perm_matrix(d16, True), ob, gl, b_gate.reshape(1, 2 * d), x2, ex,
        w_proj_a.astype(BF16), w_proj_b.astype(BF16), w_out.astype(BF16), norm_ffn_g.reshape(1, d), wr, br, s)

    block_e, nused, tok_rows, dst_rows, dst_prev = _routing_plan(idx_full[:, :TOP_K])
    n_rows = t * TOP_K + N_DUMP * EXP_BM
    y = _experts(block_e, nused, tok_rows, dst_rows, dst_prev, h2.reshape(t, ROW_SUB, LANES), w_gate_up,
                 b_gate_up.reshape(N_EXPERTS, 1, 2 * D_FF), w_down, b_down.reshape(N_EXPERTS, 1, d), n_rows)
    out = _combine(y.reshape(n_rows * ROW_SUB, LANES), gw, x1)
    return out.reshape(b, s, d)
```

```python
import functools

import jax
import jax.numpy as jnp
import numpy as np
from jax import lax
from jax.experimental import pallas as pl
from jax.experimental.pallas import tpu as pltpu

F32 = jnp.float32
BF16 = jnp.bfloat16

D_MODEL = 1024
HEAD_DIM = 64
SCALE = HEAD_DIM ** -0.5
EPS = 1e-6
NEG = -1e30
DIL_PAIRS = ((128, 1), (512, 4), (2048, 16))
HALF_WIN = 64
N_GROUPS = 3
HEADS_A = 8
OUT_A = HEADS_A * HEAD_DIM
WIDTH_A = N_GROUPS * OUT_A
ROT_DIM_A = 16
THETA_PARTIAL = 500000.0
HEADS_B_Q = 8
HEADS_B_KV = 2
WIDTH_B_Q = 512
WIDTH_B_KV = 128
GRID_W = 64
THETA_AXIAL = 10000.0
N_EXPERTS = 32
TOP_K = 4
D_FF = 1024
SWIGLU_LIMIT = 7.0
SWIGLU_ALPHA = 1.702

LANES = 128
MXU_N = 256
VMEM_LIMIT = 56 * 1024 * 1024
TILE = 256
INPROJ_TM = 512
MERGE_TM = 512
ROW_SUB = D_MODEL // LANES

COL_VA = 2 * WIDTH_A
COL_QB = 3 * WIDTH_A
COL_KB = COL_QB + WIDTH_B_Q
COL_GATE = COL_KB + 2 * WIDTH_B_KV


def _cparams(sem):
    return pltpu.CompilerParams(dimension_semantics=sem, vmem_limit_bytes=VMEM_LIMIT)


def _perm_rows(dil):
    n = np.arange(TILE)
    per = TILE // dil
    return (n % per) * dil + n // per


def _perm_matrix(dil, transpose=False):
    p = np.zeros((TILE, TILE), np.float32)
    p[np.arange(TILE), _perm_rows(dil)] = 1.0
    return jnp.asarray(p.T if transpose else p, dtype=BF16)


def _rope_tables(seq):
    pos = np.arange(seq, dtype=np.float64)[:, None]
    d = np.arange(LANES) % HEAD_DIM
    half = ROT_DIM_A // 2
    inv = THETA_PARTIAL ** (-((d % half) / half))
    ang = pos * inv[None, :]
    in_rot = (d < ROT_DIM_A)[None, :]
    first = (d < half)[None, :]
    second = ((d >= half) & (d < ROT_DIM_A))[None, :]
    ta = np.stack([np.where(in_rot, np.cos(ang), 1.0), np.where(first, -np.sin(ang), 0.0),
                   np.where(second, np.sin(ang), 0.0)])
    per_group = []
    for _, dil in DIL_PAIRS:
        order = (np.arange(seq // TILE) * TILE)[:, None] + _perm_rows(dil)[None, :]
        per_group.append(ta[:, order.reshape(-1), :])
    hb = HEAD_DIM // 4
    row = np.floor(pos / GRID_W)
    col = pos - row * GRID_W
    invb = THETA_AXIAL ** (-((d % hb) / hb))
    angb = np.where((d < HEAD_DIM // 2)[None, :], row, col) * invb[None, :]
    firstb = ((d % (2 * hb)) < hb)[None, :]
    tb = np.stack([np.cos(angb), np.where(firstb, -np.sin(angb), 0.0), np.where(firstb, 0.0, np.sin(angb))])
    return jnp.asarray(np.stack(per_group), F32), jnp.asarray(tb, F32)


def _inproj_kernel(x_ref, g_ref, w_ref, bd_ref, p4_ref, p16_ref, gqk_ref, gqb_ref, gkb_ref, ta_ref, tb_ref,
                   qkv1_ref, qkv2_ref, qkv3_ref, qb_ref, kvb_ref, gate_ref):
    x = x_ref[...]
    tm = x.shape[0]
    nsub = tm // TILE
    ms = jnp.mean(x * x, axis=-1, keepdims=True)
    h = (x * lax.rsqrt(ms + EPS) * g_ref[...]).astype(BF16)

    def regroup(p_ref):
        return jnp.concatenate(
            [jnp.dot(p_ref[...], h[s * TILE:(s + 1) * TILE], preferred_element_type=F32) for s in range(nsub)],
            axis=0).astype(BF16)

    hg = (h, regroup(p4_ref), regroup(p16_ref))
    bd = bd_ref[...]
    lane = lax.broadcasted_iota(jnp.int32, (tm, LANES), 1)
    lo = lane < HEAD_DIM

    def proj(lhs, c0):
        return jnp.dot(lhs, w_ref[:, c0:c0 + MXU_N], preferred_element_type=F32)

    def head_norm(y, gain, bdm):
        ss = jnp.dot((y * y).astype(BF16), bdm, preferred_element_type=F32)
        return y * lax.rsqrt(ss * (1.0 / HEAD_DIM) + EPS) * gain

    def rope(z, tab, sh):
        return z * tab[0] + pltpu.roll(z, LANES - sh, 1) * tab[1] + pltpu.roll(z, sh, 1) * tab[2]

    sh_a = ROT_DIM_A // 2
    sh_b = HEAD_DIM // 4

    def store_group(gi, c_out, val):
        if gi == 0:
            qkv1_ref[:, c_out:c_out + LANES] = val
        else:
            ref, dil = ((qkv2_ref, DIL_PAIRS[1][1]), (qkv3_ref, DIL_PAIRS[2][1]))[gi - 1]
            per = TILE // dil
            for s in range(nsub):
                ref[0, :, s * per:(s + 1) * per, c_out:c_out + LANES] = (
                    val[s * TILE:(s + 1) * TILE].reshape(dil, per, LANES))

    for gi in range(N_GROUPS):
        tab = (ta_ref[gi, 0], ta_ref[gi, 1], ta_ref[gi, 2])
        for which in range(3):
            for cc in range(OUT_A // MXU_N):
                c_in = which * WIDTH_A + gi * OUT_A + cc * MXU_N
                y = proj(hg[gi], c_in)
                if which < 2:
                    y = head_norm(y, gqk_ref[:, c_in:c_in + MXU_N], bd)
                for hf in range(2):
                    z = y[:, hf * LANES:(hf + 1) * LANES]
                    if which < 2:
                        z = rope(z, tab, sh_a)
                    store_group(gi, which * OUT_A + cc * MXU_N + hf * LANES, z.astype(BF16))

    tabb = (tb_ref[0], tb_ref[1], tb_ref[2])
    for c in range(WIDTH_B_Q // MXU_N):
        c0 = c * MXU_N
        yn = head_norm(proj(h, COL_QB + c0), gqb_ref[:, c0:c0 + MXU_N], bd)
        for hf in range(2):
            z = yn[:, hf * LANES:(hf + 1) * LANES]
            qb_ref[:, c0 + hf * LANES:c0 + (hf + 1) * LANES] = rope(z, tabb, sh_b).astype(BF16)
    ykv = proj(h, COL_KB)
    kb = rope(head_norm(ykv[:, :LANES], gkb_ref[...], bd[:LANES, :LANES]), tabb, sh_b)
    vb = ykv[:, LANES:]
    for j, t in enumerate((kb, vb)):
        sw = pltpu.roll(t, HEAD_DIM, 1)
        kvb_ref[:, (2 * j) * LANES:(2 * j + 1) * LANES] = jnp.where(lo, t, sw).astype(BF16)
        kvb_ref[:, (2 * j + 1) * LANES:(2 * j + 2) * LANES] = jnp.where(lo, sw, t).astype(BF16)
    for c in range(2 * D_MODEL // MXU_N):
        c0 = c * MXU_N
        gate_ref[:, c0:c0 + MXU_N] = proj(h, COL_GATE + c0).astype(BF16)


def _inproj(x2, g, w_bf, bd, p4, p16, gqk, gqb, gkb, tabs_a, tabs_b, batch, seq):
    t = x2.shape[0]
    n_cols = w_bf.shape[1]
    tm = INPROJ_TM
    npos = seq // tm
    row = lambda i: (i, 0)
    fixed = lambda i: (0, 0)
    d4, d16 = DIL_PAIRS[1][1], DIL_PAIRS[2][1]
    sub = lambda i: (i // npos, 0, i % npos, 0)
    return pl.pallas_call(
        _inproj_kernel,
        grid=(t // tm,),
        in_specs=[
            pl.BlockSpec((tm, D_MODEL), row),
            pl.BlockSpec((1, D_MODEL), fixed),
            pl.BlockSpec((D_MODEL, n_cols), fixed, pipeline_mode=pl.Buffered(1)),
            pl.BlockSpec((MXU_N, MXU_N), fixed),
            pl.BlockSpec((TILE, TILE), fixed),
            pl.BlockSpec((TILE, TILE), fixed),
            pl.BlockSpec((1, 2 * WIDTH_A), fixed),
            pl.BlockSpec((1, WIDTH_B_Q), fixed),
            pl.BlockSpec((1, WIDTH_B_KV), fixed),
            pl.BlockSpec((N_GROUPS, 3, tm, LANES), lambda i: (0, 0, i % npos, 0)),
            pl.BlockSpec((3, tm, LANES), lambda i: (0, i % npos, 0)),
        ],
        out_specs=[
            pl.BlockSpec((tm, 3 * OUT_A), row),
            pl.BlockSpec((1, d4, tm // d4, 3 * OUT_A), sub),
            pl.BlockSpec((1, d16, tm // d16, 3 * OUT_A), sub),
            pl.BlockSpec((tm, WIDTH_B_Q), row),
            pl.BlockSpec((tm, 4 * LANES), row),
            pl.BlockSpec((tm, 2 * D_MODEL), row),
        ],
        out_shape=[
            jax.ShapeDtypeStruct((t, 3 * OUT_A), BF16),
            jax.ShapeDtypeStruct((batch, d4, seq // d4, 3 * OUT_A), BF16),
            jax.ShapeDtypeStruct((batch, d16, seq // d16, 3 * OUT_A), BF16),
            jax.ShapeDtypeStruct((t, WIDTH_B_Q), BF16),
            jax.ShapeDtypeStruct((t, 4 * LANES), BF16),
            jax.ShapeDtypeStruct((t, 2 * D_MODEL), BF16),
        ],
        compiler_params=_cparams(("arbitrary",)),
        name="inproj",
    )(x2, g, w_bf, bd, p4, p16, gqk, gqb, gkb, tabs_a, tabs_b)


QBLK = 128
NSUB = 4


def _mixa_kernel(q_ref, k_ref, v_ref, o_ref, lse_ref, *, seq_len, win, batched):
    step = pl.program_id(1)
    lane = lax.broadcasted_iota(jnp.int32, (QBLK, LANES), 1)
    lo = lane < HEAD_DIM
    qi = lax.broadcasted_iota(jnp.int32, (QBLK, win), 0)
    ki = lax.broadcasted_iota(jnp.int32, (QBLK, win), 1)
    for b in range(NSUB):
        if batched:
            sq, r0, blk = b, 0, 0
            kstart = 0
        else:
            sq, r0 = 0, b * QBLK
            blk = step * NSUB + b
            kstart = pl.multiple_of(jnp.clip(blk * QBLK - HALF_WIN, 0, seq_len - win), HALF_WIN)
        valid = jnp.abs((ki + kstart) - (qi + blk * QBLK)) <= HALF_WIN
        scores, vals = [], []
        for p in range(HEADS_A // 2):
            cs = slice(p * LANES, (p + 1) * LANES)
            qp = q_ref[sq, r0:r0 + QBLK, cs]
            kp = k_ref[sq, pl.ds(kstart, win), cs]
            vals.append(v_ref[sq, pl.ds(kstart, win), cs])
            for hh in range(2):
                qh = jnp.where(lo if hh == 0 else jnp.logical_not(lo), qp, jnp.zeros_like(qp))
                scores.append(lax.dot_general(qh, kp, (((1,), (1,)), ((), ())), preferred_element_type=F32))
        s = jnp.where(valid[None], jnp.stack(scores), NEG)
        m = jnp.max(s, axis=-1, keepdims=True)
        e = jnp.exp(s - m)
        l = jnp.sum(e, axis=-1, keepdims=True)
        rl = 1.0 / l
        lse = m + jnp.log(l)
        eb = e.astype(BF16)
        lse_acc = jnp.zeros((QBLK, LANES), F32)
        for p in range(HEADS_A // 2):
            outs = [jnp.dot(eb[2 * p + hh], vals[p], preferred_element_type=F32) * rl[2 * p + hh] for hh in range(2)]
            o_ref[sq, r0:r0 + QBLK, p * LANES:(p + 1) * LANES] = jnp.where(lo, outs[0], outs[1]).astype(BF16)
            for hh in range(2):
                lse_acc = jnp.where(lane == 2 * p + hh, lse[2 * p + hh], lse_acc)
        lse_ref[sq, r0:r0 + QBLK, :] = lse_acc


def _mixa(qkv, seq_len):
    n_seq = qkv.shape[0]
    win = min(2 * QBLK, seq_len)
    batched = seq_len == QBLK
    if batched:
        grid = (n_seq // NSUB, 1)
        qspec = lambda c: pl.BlockSpec((NSUB, QBLK, OUT_A), lambda s, i: (s, 0, c))
        kspec = qspec
        ospec = pl.BlockSpec((NSUB, QBLK, OUT_A), lambda s, i: (s, 0, 0))
        lspec = pl.BlockSpec((NSUB, QBLK, LANES), lambda s, i: (s, 0, 0))
    else:
        rows = NSUB * QBLK
        grid = (n_seq, seq_len // rows)
        qspec = lambda c: pl.BlockSpec((1, rows, OUT_A), lambda s, i: (s, i, c))
        kspec = lambda c: pl.BlockSpec((1, seq_len, OUT_A), lambda s, i: (s, 0, c))
        ospec = pl.BlockSpec((1, rows, OUT_A), lambda s, i: (s, i, 0))
        lspec = pl.BlockSpec((1, rows, LANES), lambda s, i: (s, i, 0))
    return pl.pallas_call(
        functools.partial(_mixa_kernel, seq_len=seq_len, win=win, batched=batched),
        grid=grid,
        in_specs=[qspec(0), kspec(1), kspec(2)],
        out_specs=[ospec, lspec],
        out_shape=[
            jax.ShapeDtypeStruct((n_seq, seq_len, OUT_A), BF16),
            jax.ShapeDtypeStruct((n_seq, seq_len, LANES), F32),
        ],
        compiler_params=_cparams(("arbitrary", "arbitrary")),
        name=f"mixa_len{seq_len}",
    )(qkv, qkv, qkv)


def _mixb_kernel(q_ref, k_ref, v_ref, o_ref):
    tq = q_ref.shape[0]
    lane = lax.broadcasted_iota(jnp.int32, (tq, LANES), 1)
    lo = lane < HEAD_DIM
    group = HEADS_B_Q // HEADS_B_KV
    for j in range(HEADS_B_KV):
        kd = k_ref[:, j * LANES:(j + 1) * LANES]
        vd = v_ref[:, j * LANES:(j + 1) * LANES]
        parts = []
        for h in range(j * group, (j + 1) * group):
            qp = q_ref[:, (h // 2) * LANES:(h // 2 + 1) * LANES]
            parts.append(jnp.where(lo if h % 2 == 0 else jnp.logical_not(lo), qp, jnp.zeros_like(qp)))
        q4 = jnp.concatenate(parts, axis=0)
        s = lax.dot_general(q4, kd, (((1,), (1,)), ((), ())), preferred_element_type=F32)
        m = jnp.max(s, axis=-1, keepdims=True)
        e = jnp.exp(s - m)
        l = jnp.sum(e, axis=-1, keepdims=True)
        o = jnp.dot(e.astype(BF16), vd, preferred_element_type=F32) * (1.0 / l)
        for p in range(group // 2):
            even = o[(2 * p) * tq:(2 * p + 1) * tq]
            odd = o[(2 * p + 1) * tq:(2 * p + 2) * tq]
            c0 = (j * group // 2 + p) * LANES
            o_ref[:, c0:c0 + LANES] = jnp.where(lo, even, odd).astype(BF16)


def _mixb(qb, kvb, seq, tq):
    t = qb.shape[0]
    nq = seq // tq
    return pl.pallas_call(
        _mixb_kernel,
        grid=(t // seq, nq),
        in_specs=[
            pl.BlockSpec((tq, WIDTH_B_Q), lambda b, i: (b * nq + i, 0)),
            pl.BlockSpec((seq, 2 * LANES), lambda b, i: (b, 0)),
            pl.BlockSpec((seq, 2 * LANES), lambda b, i: (b, 1)),
        ],
        out_specs=pl.BlockSpec((tq, WIDTH_B_Q), lambda b, i: (b * nq + i, 0)),
        out_shape=jax.ShapeDtypeStruct((t, WIDTH_B_Q), BF16),
        compiler_params=_cparams(("arbitrary", "arbitrary")),
        name="mixb",
    )(qb, kvb, kvb)


def _merge_kernel(o1_ref, o2_ref, o3_ref, l1_ref, l2_ref, l3_ref, p4t_ref, p16t_ref, ob_ref, gl_ref, bg_ref,
                  x_ref, ex_ref, wpa_ref, wpb_ref, wo_ref, g2_ref, wr_ref, br_ref,
                  x1_ref, h2_ref, idx_ref, gw_ref):
    tm = x_ref.shape[0]
    nsub = tm // TILE
    lane = lax.broadcasted_iota(jnp.int32, (tm, LANES), 1)

    def split(w):
        hi = w.astype(BF16)
        return hi, (w - hi.astype(F32)).astype(BF16)

    def unperm(pt, ref):
        dil = ref.shape[1]
        per = TILE // dil
        outs = []
        for s in range(nsub):
            val = ref[0, :, s * per:(s + 1) * per, :].reshape(TILE, ref.shape[3])
            if val.dtype == BF16:
                outs.append(jnp.dot(pt, val, preferred_element_type=F32))
            else:
                hi, lo_ = split(val)
                outs.append(jnp.dot(pt, hi, preferred_element_type=F32) + jnp.dot(pt, lo_, preferred_element_type=F32))
        return jnp.concatenate(outs, axis=0)

    p4t, p16t = p4t_ref[...], p16t_ref[...]
    o1 = o1_ref[...].astype(F32)
    o2 = unperm(p4t, o2_ref)
    o3 = unperm(p16t, o3_ref)
    l1 = l1_ref[...]
    l2 = unperm(p4t, l2_ref)
    l3 = unperm(p16t, l3_ref)
    mx = jnp.maximum(jnp.maximum(l1, l2), l3)
    e1, e2, e3 = jnp.exp(l1 - mx), jnp.exp(l2 - mx), jnp.exp(l3 - mx)
    rden = 1.0 / (e1 + e2 + e3)
    ex = ex_ref[...]

    def expand(w):
        hi, lo_ = split(w)
        return jnp.dot(hi, ex, preferred_element_type=F32) + jnp.dot(lo_, ex, preferred_element_type=F32)

    oa = expand(e1 * rden) * o1 + expand(e2 * rden) * o2 + expand(e3 * rden) * o3
    pa = jnp.dot(oa.astype(BF16), wpa_ref[...], preferred_element_type=F32)
    pb = jnp.dot(ob_ref[...], wpb_ref[...], preferred_element_type=F32)
    gl = gl_ref[...].astype(F32) + bg_ref[...]
    gates = 1.0 / (1.0 + jnp.exp(-gl))
    merged = gates[:, :D_MODEL] * pa + gates[:, D_MODEL:] * pb
    x1 = x_ref[...] + jnp.dot(merged.astype(BF16), wo_ref[...], preferred_element_type=F32)
    x1_ref[...] = x1
    ms = jnp.mean(x1 * x1, axis=-1, keepdims=True)
    h2 = x1 * lax.rsqrt(ms + EPS) * g2_ref[...]
    for c in range(D_MODEL // LANES):
        h2_ref[pl.ds(c, tm, stride=ROW_SUB), :] = h2[:, c * LANES:(c + 1) * LANES]
    h_hi, h_lo = split(h2)
    w_hi = wr_ref[0]
    w_lo = wr_ref[1]
    logits = (jnp.dot(h_hi, w_hi, preferred_element_type=F32)
              + jnp.dot(h_lo, w_hi, preferred_element_type=F32)
              + jnp.dot(h_hi, w_lo, preferred_element_type=F32)) + br_ref[...]
    work = logits
    lane_f = lane.astype(F32)
    vals, idxs = [], []
    for _ in range(TOP_K):
        m = jnp.max(work, axis=-1, keepdims=True)
        ix = jnp.min(jnp.where(work == m, lane_f, float(LANES)), axis=-1, keepdims=True)
        vals.append(m)
        idxs.append(ix)
        work = jnp.where(lane_f == ix, -jnp.inf, work)
    es = [jnp.exp(v - vals[0]) for v in vals]
    rsum = 1.0 / (es[0] + es[1] + es[2] + es[3])
    idx_out = jnp.zeros((tm, LANES), F32)
    gw_out = jnp.zeros((tm, LANES), F32)
    for k in range(TOP_K):
        idx_out = jnp.where(lane == k, idxs[k], idx_out)
        gw_out = jnp.where(lane == k, es[k] * rsum, gw_out)
    idx_ref[...] = idx_out.astype(jnp.int32)
    gw_ref[...] = gw_out


def _merge(o1, o2, o3, l1, l2, l3, p4t, p16t, ob, gl, bg, x2, ex, wpa, wpb, wo, g2, wr, br, seq):
    t = x2.shape[0]
    tm = MERGE_TM
    npos = seq // tm
    row = lambda i: (i, 0)
    fixed = lambda i: (0, 0)
    d4, d16 = DIL_PAIRS[1][1], DIL_PAIRS[2][1]
    sub = lambda i: (i // npos, 0, i % npos, 0)
    return pl.pallas_call(
        _merge_kernel,
        grid=(t // tm,),
        in_specs=[
            pl.BlockSpec((tm, OUT_A), row),
            pl.BlockSpec((1, d4, tm // d4, OUT_A), sub),
            pl.BlockSpec((1, d16, tm // d16, OUT_A), sub),
            pl.BlockSpec((tm, LANES), row),
            pl.BlockSpec((1, d4, tm // d4, LANES), sub),
            pl.BlockSpec((1, d16, tm // d16, LANES), sub),
            pl.BlockSpec((TILE, TILE), fixed),
            pl.BlockSpec((TILE, TILE), fixed),
            pl.BlockSpec((tm, WIDTH_B_Q), row),
            pl.BlockSpec((tm, 2 * D_MODEL), row),
            pl.BlockSpec((1, 2 * D_MODEL), fixed),
            pl.BlockSpec((tm, D_MODEL), row),
            pl.BlockSpec((LANES, OUT_A), fixed),
            pl.BlockSpec((OUT_A, D_MODEL), fixed),
            pl.BlockSpec((WIDTH_B_Q, D_MODEL), fixed),
            pl.BlockSpec((D_MODEL, D_MODEL), fixed),
            pl.BlockSpec((1, D_MODEL), fixed),
            pl.BlockSpec((2, D_MODEL, LANES), lambda i: (0, 0, 0)),
            pl.BlockSpec((1, LANES), fixed),
        ],
        out_specs=[
            pl.BlockSpec((tm, D_MODEL), row), pl.BlockSpec((tm * ROW_SUB, LANES), row),
            pl.BlockSpec((tm, LANES), row), pl.BlockSpec((tm, LANES), row),
        ],
        out_shape=[
            jax.ShapeDtypeStruct((t, D_MODEL), F32),
            jax.ShapeDtypeStruct((t * ROW_SUB, LANES), F32),
            jax.ShapeDtypeStruct((t, LANES), jnp.int32),
            jax.ShapeDtypeStruct((t, LANES), F32),
        ],
        compiler_params=_cparams(("arbitrary",)),
        name="merge",
    )(o1, o2, o3, l1, l2, l3, p4t, p16t, ob, gl, bg, x2, ex, wpa, wpb, wo, g2, wr, br)


EXP_BM = 256


W_CHUNK = 256


FF_CHUNK = 256
N_DUMP = 2


def _experts_kernel(be_ref, nused_ref, tok_ref, tokn_ref, dst_ref, dstp_ref,
                    h2_hbm, wgu_ref, bgu_ref, wd_ref, bd_ref,
                    y_hbm, xbuf, ybuf, wgu_bf, wd_bf, gsem, ssem):
    i = pl.program_id(0)
    nused = nused_ref[0]
    n_real = y_hbm.shape[0] - N_DUMP * EXP_BM

    def gather(tok, s, r):
        return pltpu.make_async_copy(h2_hbm.at[tok], xbuf.at[s, pl.ds(r * ROW_SUB, ROW_SUB)], gsem.at[s])

    def scatter(s, dst, r):
        return pltpu.make_async_copy(ybuf.at[s, pl.ds(r * ROW_SUB, ROW_SUB)], y_hbm.at[dst], ssem.at[s])

    @pl.when(i == 0)
    def _():
        ybuf[...] = jnp.zeros_like(ybuf)
        for r in range(EXP_BM):
            gather(tok_ref[0, 0, r], 0, r).start()
            scatter(0, n_real + r, r).start()

    @pl.when(jnp.logical_and(i < nused, jnp.logical_or(i == 0, be_ref[i] != be_ref[jnp.maximum(i - 1, 0)])))
    def _():
        for c in range(2 * D_FF // W_CHUNK):
            cs = slice(c * W_CHUNK, (c + 1) * W_CHUNK)
            wgu_bf[:, cs] = wgu_ref[0, :, cs].astype(BF16)
        for c in range(D_MODEL // W_CHUNK):
            cs = slice(c * W_CHUNK, (c + 1) * W_CHUNK)
            wd_bf[:, cs] = wd_ref[0, :, cs].astype(BF16)

    def block(slot):
        other = 1 - slot

        @pl.when(nused > 0)
        def _():
            for r in range(EXP_BM):
                gather(tokn_ref[0, 0, r], other, r).start()

        for r in range(EXP_BM):
            gather(0, slot, r).wait()
        xb = jnp.concatenate([xbuf[slot, pl.ds(c, EXP_BM, stride=ROW_SUB), :] for c in range(ROW_SUB)],
                             axis=1).astype(BF16)
        n_chunks = D_FF // FF_CHUNK
        per = EXP_BM // n_chunks
        y = None
        for f in range(n_chunks):
            gs = slice(f * FF_CHUNK, (f + 1) * FF_CHUNK)
            us = slice(D_FF + f * FF_CHUNK, D_FF + (f + 1) * FF_CHUNK)
            g = jnp.dot(xb, wgu_bf[:, gs], preferred_element_type=F32) + bgu_ref[0, :, gs]
            u = jnp.dot(xb, wgu_bf[:, us], preferred_element_type=F32) + bgu_ref[0, :, us]
            gate = jnp.minimum(g, SWIGLU_LIMIT)
            up = jnp.clip(u, -SWIGLU_LIMIT, SWIGLU_LIMIT)
            act = (up + 1.0) * (gate * (1.0 / (1.0 + jnp.exp(-SWIGLU_ALPHA * gate))))
            part = jnp.dot(act.astype(BF16), wd_bf[gs, :], preferred_element_type=F32)
            y = part if y is None else y + part
            for r in range(f * per, (f + 1) * per):
                scatter(other, dstp_ref[0, 0, r], r).start()
        y = y + bd_ref[0]

        for r in range(EXP_BM):
            scatter(slot, 0, r).wait()
        for c in range(ROW_SUB):
            ybuf[slot, pl.ds(c, EXP_BM, stride=ROW_SUB), :] = y[:, c * LANES:(c + 1) * LANES]

        @pl.when(i == nused - 1)
        def _():
            for r in range(EXP_BM):
                scatter(slot, dst_ref[0, 0, r], r).start()
            for r in range(EXP_BM):
                scatter(other, 0, r).wait()
                scatter(slot, 0, r).wait()
                gather(0, other, r).wait()

    for parity in range(2):
        pl.when(jnp.logical_and(i < nused, i % 2 == parity))(functools.partial(block, parity))


def _experts(block_e, nused, tok_rows, dst_rows, dst_prev, h2, wgu, bgu, wd, bd, n_out_rows):
    n_blk = block_e.shape[0]
    d = D_MODEL
    idx_spec = lambda f: pl.BlockSpec((1, 1, EXP_BM), f, memory_space=pltpu.SMEM)
    grid_spec = pltpu.PrefetchScalarGridSpec(
        num_scalar_prefetch=2,
        grid=(n_blk,),
        in_specs=[
            idx_spec(lambda i, be, nu: (i, 0, 0)),
            idx_spec(lambda i, be, nu: (jnp.minimum(i + 1, n_blk - 1), 0, 0)),
            idx_spec(lambda i, be, nu: (i, 0, 0)),
            idx_spec(lambda i, be, nu: (i, 0, 0)),
            pl.BlockSpec(memory_space=pl.ANY),
            pl.BlockSpec((1, d, 2 * D_FF), lambda i, be, nu: (be[i], 0, 0)),
            pl.BlockSpec((1, 1, 2 * D_FF), lambda i, be, nu: (be[i], 0, 0)),
            pl.BlockSpec((1, D_FF, d), lambda i, be, nu: (be[i], 0, 0)),
            pl.BlockSpec((1, 1, d), lambda i, be, nu: (be[i], 0, 0)),
        ],
        out_specs=pl.BlockSpec(memory_space=pl.ANY),
        scratch_shapes=[
            pltpu.VMEM((2, EXP_BM * ROW_SUB, LANES), F32),
            pltpu.VMEM((2, EXP_BM * ROW_SUB, LANES), F32),
            pltpu.VMEM((d, 2 * D_FF), BF16),
            pltpu.VMEM((D_FF, d), BF16),
            pltpu.SemaphoreType.DMA((2,)),
            pltpu.SemaphoreType.DMA((2,)),
        ],
    )
    return pl.pallas_call(
        _experts_kernel,
        grid_spec=grid_spec,
        out_shape=jax.ShapeDtypeStruct((n_out_rows, ROW_SUB, LANES), F32),
        compiler_params=_cparams(("arbitrary",)),
        name="experts",
    )(block_e, nused, tok_rows, tok_rows, dst_rows, dst_prev, h2, wgu, bgu, wd, bd)


def _combine_kernel(y0_ref, y1_ref, y2_ref, y3_ref, gw_ref, x1_ref, o_ref):
    gw = gw_ref[...]
    for c in range(ROW_SUB):
        cs = slice(c * LANES, (c + 1) * LANES)
        acc = x1_ref[:, cs]
        for k, y_ref in enumerate((y0_ref, y1_ref, y2_ref, y3_ref)):
            acc = acc + gw[:, k:k + 1] * y_ref[pl.ds(c, TILE, stride=ROW_SUB), :]
        o_ref[:, cs] = acc


def _combine(y, gw, x1):
    t = x1.shape[0]
    nt = t // TILE
    row = lambda i: (i, 0)
    yspec = lambda k: pl.BlockSpec((TILE * ROW_SUB, LANES), lambda i: (k * nt + i, 0))
    return pl.pallas_call(
        _combine_kernel,
        grid=(nt,),
        in_specs=[yspec(0), yspec(1), yspec(2), yspec(3), pl.BlockSpec((TILE, LANES), row),
                  pl.BlockSpec((TILE, D_MODEL), row)],
        out_specs=pl.BlockSpec((TILE, D_MODEL), row),
        out_shape=jax.ShapeDtypeStruct((t, D_MODEL), F32),
        compiler_params=_cparams(("arbitrary",)),
        name="combine",
    )(y, y, y, y, gw, x1)


def _routing_plan(idx):
    t = idx.shape[0]
    a = t * TOP_K
    tok_bits = (t - 1).bit_length()
    key = ((idx << (tok_bits + 2)) | (jnp.arange(t, dtype=jnp.int32)[:, None] << 2)
           | jnp.arange(TOP_K, dtype=jnp.int32)[None, :])
    skey = jnp.sort(key.reshape(-1))
    bounds = jnp.searchsorted(skey, jnp.arange(N_EXPERTS + 1, dtype=jnp.int32) << (tok_bits + 2),
                              method='compare_all').astype(jnp.int32)
    starts, counts = bounds[:-1], bounds[1:] - bounds[:-1]
    padded = ((counts + EXP_BM - 1) // EXP_BM) * EXP_BM
    pends = jnp.cumsum(padded)
    pstarts = pends - padded
    n_blk = a // EXP_BM + N_EXPERTS
    blk = jnp.arange(n_blk, dtype=jnp.int32)
    block_e = jnp.minimum(jnp.sum((pends[None, :] <= (blk * EXP_BM)[:, None]).astype(jnp.int32), axis=1),
                          N_EXPERTS - 1)
    nused = (pends[-1] // EXP_BM).astype(jnp.int32).reshape(1)
    sel = block_e[:, None] == jnp.arange(N_EXPERTS, dtype=jnp.int32)[None, :]
    pick = lambda tab: jnp.sum(jnp.where(sel, tab[None, :], 0), axis=1)
    first = blk * EXP_BM - pick(pstarts)
    q = jnp.arange(EXP_BM, dtype=jnp.int32)[None, :]
    valid = q < (pick(counts) - first)[:, None]
    kv = skey[jnp.clip((pick(starts) + first)[:, None] + q, 0, a - 1)]
    tok = (kv >> 2) & ((1 << tok_bits) - 1)
    tok_rows = jnp.where(valid, tok, 0)
    dump = a + (blk % N_DUMP)[:, None] * EXP_BM + q
    dst_rows = jnp.where(valid, (kv & (TOP_K - 1)) * t + tok, dump)
    dst_prev = jnp.concatenate([a + EXP_BM + q, dst_rows[:-1]], axis=0)
    shape = (n_blk, 1, EXP_BM)
    return block_e, nused, tok_rows.reshape(shape), dst_rows.reshape(shape), dst_prev.reshape(shape)


def kernel(x, norm_mix_g, w_in, b_gate, qn_a, kn_a, qn_b, kn_b, w_proj_a, w_proj_b, w_out,
           norm_ffn_g, w_router, b_router, w_gate_up, b_gate_up, w_down, b_down):
    b, s, d = x.shape
    t = b * s
    x2 = x.reshape(t, d)
    tabs_a, tabs_b = _rope_tables(s)
    d4, d16 = DIL_PAIRS[1][1], DIL_PAIRS[2][1]

    gq = jnp.tile(qn_a[:, None, :], (1, HEADS_A, 1)).reshape(1, WIDTH_A) * SCALE
    gk = jnp.tile(kn_a[:, None, :], (1, HEADS_A, 1)).reshape(1, WIDTH_A)
    gqk = jnp.concatenate([gq, gk], axis=1)
    gqb = jnp.tile(qn_b, HEADS_B_Q).reshape(1, WIDTH_B_Q) * SCALE
    gkb = jnp.tile(kn_b, HEADS_B_KV).reshape(1, WIDTH_B_KV)
    hid = np.arange(MXU_N) // HEAD_DIM
    bd = jnp.asarray(hid[:, None] == hid[None, :], dtype=BF16)

    qkv1, qkv2, qkv3, qb, kvb, gl = _inproj(
        x2, norm_mix_g.reshape(1, d), w_in.astype(BF16), bd, _perm_matrix(d4), _perm_matrix(d16),
        gqk, gqb, gkb, tabs_a, tabs_b, b, s)

    o1, l1 = _mixa(qkv1.reshape(b, s, 3 * OUT_A), s)
    o2, l2 = _mixa(qkv2.reshape(b * d4, s // d4, 3 * OUT_A), s // d4)
    o3, l3 = _mixa(qkv3.reshape(b * d16, s // d16, 3 * OUT_A), s // d16)
    ob = _mixb(qb, kvb, s, 256)

    ex = jnp.asarray(np.arange(LANES)[:, None] == (np.arange(OUT_A) // HEAD_DIM)[None, :], dtype=BF16)
    wr_pad = jnp.zeros((d, LANES), F32).at[:, :N_EXPERTS].set(w_router)
    wr_hi = wr_pad.astype(BF16)
    wr = jnp.stack([wr_hi, (wr_pad - wr_hi.astype(F32)).astype(BF16)])
    br = jnp.full((1, LANES), NEG, F32).at[0, :N_EXPERTS].set(b_router)
    x1, h2, idx_full, gw = _merge(
        o1.reshape(t, OUT_A), o2.reshape(b, d4, s // d4, OUT_A), o3.reshape(b, d16, s // d16, OUT_A),
        l1.reshape(t, LANES), l2.reshape(b, d4, s // d4, LANES), l3.reshape(b, d16, s // d16, LANES),
        _perm_matrix(d4, True), _perm_matrix(d16, True), ob, gl, b_gate.reshape(1, 2 * d), x2, ex,
        w_proj_a.astype(BF16), w_proj_b.astype(BF16), w_out.astype(BF16), norm_ffn_g.reshape(1, d), wr, br, s)

    block_e, nused, tok_rows, dst_rows, dst_prev = _routing_plan(idx_full[:, :TOP_K])
    n_rows = t * TOP_K + N_DUMP * EXP_BM
    y = _experts(block_e, nused, tok_rows, dst_rows, dst_prev, h2.reshape(t, ROW_SUB, LANES), w_gate_up,
                 b_gate_up.reshape(N_EXPERTS, 1, 2 * D_FF), w_down, b_down.reshape(N_EXPERTS, 1, d), n_rows)
    out = _combine(y.reshape(n_rows * ROW_SUB, LANES), gw, x1)
    return out.reshape(b, s, d)
```

```python
import functools

import jax
import jax.numpy as jnp
import numpy as np
from jax import lax
from jax.experimental import pallas as pl
from jax.experimental.pallas import tpu as pltpu

F32 = jnp.float32
BF16 = jnp.bfloat16

D_MODEL = 1024
HEAD_DIM = 64
SCALE = HEAD_DIM ** -0.5
EPS = 1e-6
NEG = -1e30
DIL_PAIRS = ((128, 1), (512, 4), (2048, 16))
HALF_WIN = 64
N_GROUPS = 3
HEADS_A = 8
OUT_A = HEADS_A * HEAD_DIM
WIDTH_A = N_GROUPS * OUT_A
ROT_DIM_A = 16
THETA_PARTIAL = 500000.0
HEADS_B_Q = 8
HEADS_B_KV = 2
WIDTH_B_Q = 512
WIDTH_B_KV = 128
GRID_W = 64
THETA_AXIAL = 10000.0
N_EXPERTS = 32
TOP_K = 4
D_FF = 1024
SWIGLU_LIMIT = 7.0
SWIGLU_ALPHA = 1.702

LANES = 128
MXU_N = 256
VMEM_LIMIT = 56 * 1024 * 1024
TILE = 256
INPROJ_TM = 512
MERGE_TM = 512
MIXB_TQ = 512
ROW_SUB = D_MODEL // LANES

COL_VA = 2 * WIDTH_A
COL_QB = 3 * WIDTH_A
COL_KB = COL_QB + WIDTH_B_Q
COL_GATE = COL_KB + 2 * WIDTH_B_KV


def _cparams(sem):
    return pltpu.CompilerParams(dimension_semantics=sem, vmem_limit_bytes=VMEM_LIMIT)


def _perm_rows(dil):
    n = np.arange(TILE)
    per = TILE // dil
    return (n % per) * dil + n // per


def _perm_matrix(dil, transpose=False):
    p = np.zeros((TILE, TILE), np.float32)
    p[np.arange(TILE), _perm_rows(dil)] = 1.0
    return jnp.asarray(p.T if transpose else p, dtype=BF16)


def _rope_tables(seq):
    pos = np.arange(seq, dtype=np.float64)[:, None]
    d = np.arange(LANES) % HEAD_DIM
    half = ROT_DIM_A // 2
    inv = THETA_PARTIAL ** (-((d % half) / half))
    ang = pos * inv[None, :]
    in_rot = (d < ROT_DIM_A)[None, :]
    first = (d < half)[None, :]
    second = ((d >= half) & (d < ROT_DIM_A))[None, :]
    ta = np.stack([np.where(in_rot, np.cos(ang), 1.0), np.where(first, -np.sin(ang), 0.0),
                   np.where(second, np.sin(ang), 0.0)])
    per_group = []
    for _, dil in DIL_PAIRS:
        order = (np.arange(seq // TILE) * TILE)[:, None] + _perm_rows(dil)[None, :]
        per_group.append(ta[:, order.reshape(-1), :])
    hb = HEAD_DIM // 4
    row = np.floor(pos / GRID_W)
    col = pos - row * GRID_W
    invb = THETA_AXIAL ** (-((d % hb) / hb))
    angb = np.where((d < HEAD_DIM // 2)[None, :], row, col) * invb[None, :]
    firstb = ((d % (2 * hb)) < hb)[None, :]
    tb = np.stack([np.cos(angb), np.where(firstb, -np.sin(angb), 0.0), np.where(firstb, 0.0, np.sin(angb))])
    return jnp.asarray(np.stack(per_group), F32), jnp.asarray(tb, F32)


def _inproj_kernel(x_ref, g_ref, w_ref, bd_ref, p4_ref, p16_ref, gqk_ref, gqb_ref, gkb_ref, ta_ref, tb_ref,
                   qkv1_ref, qkv2_ref, qkv3_ref, qb_ref, kvb_ref, gate_ref):
    x = x_ref[...]
    tm = x.shape[0]
    nsub = tm // TILE
    ms = jnp.mean(x * x, axis=-1, keepdims=True)
    h = (x * lax.rsqrt(ms + EPS) * g_ref[...]).astype(BF16)

    def regroup(p_ref):
        return jnp.concatenate(
            [jnp.dot(p_ref[...], h[s * TILE:(s + 1) * TILE], preferred_element_type=F32) for s in range(nsub)],
            axis=0).astype(BF16)

    hg = (h, regroup(p4_ref), regroup(p16_ref))
    bd = bd_ref[...]
    lane = lax.broadcasted_iota(jnp.int32, (tm, LANES), 1)
    lo = lane < HEAD_DIM

    def proj(lhs, c0):
        return jnp.dot(lhs, w_ref[:, c0:c0 + MXU_N], preferred_element_type=F32)

    def head_norm(y, gain, bdm):
        ss = jnp.dot((y * y).astype(BF16), bdm, preferred_element_type=F32)
        return y * lax.rsqrt(ss * (1.0 / HEAD_DIM) + EPS) * gain

    def rope(z, tab, sh):
        return z * tab[0] + pltpu.roll(z, LANES - sh, 1) * tab[1] + pltpu.roll(z, sh, 1) * tab[2]

    sh_a = ROT_DIM_A // 2
    sh_b = HEAD_DIM // 4

    def store_group(gi, c_out, val):
        if gi == 0:
            qkv1_ref[:, c_out:c_out + LANES] = val
        else:
            ref, dil = ((qkv2_ref, DIL_PAIRS[1][1]), (qkv3_ref, DIL_PAIRS[2][1]))[gi - 1]
            per = TILE // dil
            for s in range(nsub):
                ref[0, :, s * per:(s + 1) * per, c_out:c_out + LANES] = (
                    val[s * TILE:(s + 1) * TILE].reshape(dil, per, LANES))

    for gi in range(N_GROUPS):
        tab = (ta_ref[gi, 0], ta_ref[gi, 1], ta_ref[gi, 2])
        for which in range(3):
            for cc in range(OUT_A // MXU_N):
                c_in = which * WIDTH_A + gi * OUT_A + cc * MXU_N
                y = proj(hg[gi], c_in)
                if which < 2:
                    y = head_norm(y, gqk_ref[:, c_in:c_in + MXU_N], bd)
                for hf in range(2):
                    z = y[:, hf * LANES:(hf + 1) * LANES]
                    if which < 2:
                        z = rope(z, tab, sh_a)
                    store_group(gi, which * OUT_A + cc * MXU_N + hf * LANES, z.astype(BF16))

    tabb = (tb_ref[0], tb_ref[1], tb_ref[2])
    for c in range(WIDTH_B_Q // MXU_N):
        c0 = c * MXU_N
        yn = head_norm(proj(h, COL_QB + c0), gqb_ref[:, c0:c0 + MXU_N], bd)
        for hf in range(2):
            z = yn[:, hf * LANES:(hf + 1) * LANES]
            qb_ref[:, c0 + hf * LANES:c0 + (hf + 1) * LANES] = rope(z, tabb, sh_b).astype(BF16)
    ykv = proj(h, COL_KB)
    kb = rope(head_norm(ykv[:, :LANES], gkb_ref[...], bd[:LANES, :LANES]), tabb, sh_b)
    vb = ykv[:, LANES:]
    for j, t in enumerate((kb, vb)):
        sw = pltpu.roll(t, HEAD_DIM, 1)
        kvb_ref[:, (2 * j) * LANES:(2 * j + 1) * LANES] = jnp.where(lo, t, sw).astype(BF16)
        kvb_ref[:, (2 * j + 1) * LANES:(2 * j + 2) * LANES] = jnp.where(lo, sw, t).astype(BF16)
    for c in range(2 * D_MODEL // MXU_N):
        c0 = c * MXU_N
        gate_ref[:, c0:c0 + MXU_N] = proj(h, COL_GATE + c0).astype(BF16)


def _inproj(x2, g, w_bf, bd, p4, p16, gqk, gqb, gkb, tabs_a, tabs_b, batch, seq):
    t = x2.shape[0]
    n_cols = w_bf.shape[1]
    tm = INPROJ_TM
    npos = seq // tm
    row = lambda i: (i, 0)
    fixed = lambda i: (0, 0)
    d4, d16 = DIL_PAIRS[1][1], DIL_PAIRS[2][1]
    sub = lambda i: (i // npos, 0, i % npos, 0)
    return pl.pallas_call(
        _inproj_kernel,
        grid=(t // tm,),
        in_specs=[
            pl.BlockSpec((tm, D_MODEL), row),
            pl.BlockSpec((1, D_MODEL), fixed),
            pl.BlockSpec((D_MODEL, n_cols), fixed, pipeline_mode=pl.Buffered(1)),
            pl.BlockSpec((MXU_N, MXU_N), fixed),
            pl.BlockSpec((TILE, TILE), fixed),
            pl.BlockSpec((TILE, TILE), fixed),
            pl.BlockSpec((1, 2 * WIDTH_A), fixed),
            pl.BlockSpec((1, WIDTH_B_Q), fixed),
            pl.BlockSpec((1, WIDTH_B_KV), fixed),
            pl.BlockSpec((N_GROUPS, 3, tm, LANES), lambda i: (0, 0, i % npos, 0)),
            pl.BlockSpec((3, tm, LANES), lambda i: (0, i % npos, 0)),
        ],
        out_specs=[
            pl.BlockSpec((tm, 3 * OUT_A), row),
            pl.BlockSpec((1, d4, tm // d4, 3 * OUT_A), sub),
            pl.BlockSpec((1, d16, tm // d16, 3 * OUT_A), sub),
            pl.BlockSpec((tm, WIDTH_B_Q), row),
            pl.BlockSpec((tm, 4 * LANES), row),
            pl.BlockSpec((tm, 2 * D_MODEL), row),
        ],
        out_shape=[
            jax.ShapeDtypeStruct((t, 3 * OUT_A), BF16),
            jax.ShapeDtypeStruct((batch, d4, seq // d4, 3 * OUT_A), BF16),
            jax.ShapeDtypeStruct((batch, d16, seq // d16, 3 * OUT_A), BF16),
            jax.ShapeDtypeStruct((t, WIDTH_B_Q), BF16),
            jax.ShapeDtypeStruct((t, 4 * LANES), BF16),
            jax.ShapeDtypeStruct((t, 2 * D_MODEL), BF16),
        ],
        compiler_params=_cparams(("arbitrary",)),
        name="inproj",
    )(x2, g, w_bf, bd, p4, p16, gqk, gqb, gkb, tabs_a, tabs_b)


QBLK = 128
NSUB = 4


def _mixa_kernel(q_ref, k_ref, v_ref, o_ref, lse_ref, *, seq_len, win, batched):
    step = pl.program_id(1)
    lane = lax.broadcasted_iota(jnp.int32, (QBLK, LANES), 1)
    lo = lane < HEAD_DIM
    qi = lax.broadcasted_iota(jnp.int32, (QBLK, win), 0)
    ki = lax.broadcasted_iota(jnp.int32, (QBLK, win), 1)
    for b in range(NSUB):
        if batched:
            sq, r0, blk = b, 0, 0
            kstart = 0
        else:
            sq, r0 = 0, b * QBLK
            blk = step * NSUB + b
            kstart = pl.multiple_of(jnp.clip(blk * QBLK - HALF_WIN, 0, seq_len - win), HALF_WIN)
        valid = jnp.abs((ki + kstart) - (qi + blk * QBLK)) <= HALF_WIN
        scores, vals = [], []
        for p in range(HEADS_A // 2):
            cs = slice(p * LANES, (p + 1) * LANES)
            qp = q_ref[sq, r0:r0 + QBLK, cs]
            kp = k_ref[sq, pl.ds(kstart, win), cs]
            vals.append(v_ref[sq, pl.ds(kstart, win), cs])
            for hh in range(2):
                qh = jnp.where(lo if hh == 0 else jnp.logical_not(lo), qp, jnp.zeros_like(qp))
                scores.append(lax.dot_general(qh, kp, (((1,), (1,)), ((), ())), preferred_element_type=F32))
        s = jnp.where(valid[None], jnp.stack(scores), NEG)
        m = jnp.max(s, axis=-1, keepdims=True)
        e = jnp.exp(s - m)
        l = jnp.sum(e, axis=-1, keepdims=True)
        rl = 1.0 / l
        lse = m + jnp.log(l)
        eb = e.astype(BF16)
        lse_acc = jnp.zeros((QBLK, LANES), F32)
        for p in range(HEADS_A // 2):
            outs = [jnp.dot(eb[2 * p + hh], vals[p], preferred_element_type=F32) * rl[2 * p + hh] for hh in range(2)]
            o_ref[sq, r0:r0 + QBLK, p * LANES:(p + 1) * LANES] = jnp.where(lo, outs[0], outs[1]).astype(BF16)
            for hh in range(2):
                lse_acc = jnp.where(lane == 2 * p + hh, lse[2 * p + hh], lse_acc)
        lse_ref[sq, r0:r0 + QBLK, :] = lse_acc


def _mixa(qkv, seq_len):
    n_seq = qkv.shape[0]
    win = min(2 * QBLK, seq_len)
    batched = seq_len == QBLK
    if batched:
        grid = (n_seq // NSUB, 1)
        qspec = lambda c: pl.BlockSpec((NSUB, QBLK, OUT_A), lambda s, i: (s, 0, c))
        kspec = qspec
        ospec = pl.BlockSpec((NSUB, QBLK, OUT_A), lambda s, i: (s, 0, 0))
        lspec = pl.BlockSpec((NSUB, QBLK, LANES), lambda s, i: (s, 0, 0))
    else:
        rows = NSUB * QBLK
        grid = (n_seq, seq_len // rows)
        qspec = lambda c: pl.BlockSpec((1, rows, OUT_A), lambda s, i: (s, i, c))
        kspec = lambda c: pl.BlockSpec((1, seq_len, OUT_A), lambda s, i: (s, 0, c))
        ospec = pl.BlockSpec((1, rows, OUT_A), lambda s, i: (s, i, 0))
        lspec = pl.BlockSpec((1, rows, LANES), lambda s, i: (s, i, 0))
    return pl.pallas_call(
        functools.partial(_mixa_kernel, seq_len=seq_len, win=win, batched=batched),
        grid=grid,
        in_specs=[qspec(0), kspec(1), kspec(2)],
        out_specs=[ospec, lspec],
        out_shape=[
            jax.ShapeDtypeStruct((n_seq, seq_len, OUT_A), BF16),
            jax.ShapeDtypeStruct((n_seq, seq_len, LANES), F32),
        ],
        compiler_params=_cparams(("arbitrary", "arbitrary")),
        name=f"mixa_len{seq_len}",
    )(qkv, qkv, qkv)


def _mixb_kernel(q_ref, k_ref, v_ref, o_ref):
    tq = q_ref.shape[0]
    lane = lax.broadcasted_iota(jnp.int32, (tq, LANES), 1)
    lo = lane < HEAD_DIM
    for p in range(HEADS_B_Q // 2):
        j = p // 2
        cs = slice(p * LANES, (p + 1) * LANES)
        qp = q_ref[:, cs]
        kd = k_ref[:, j * LANES:(j + 1) * LANES]
        vd = v_ref[:, j * LANES:(j + 1) * LANES]
        outs = []
        for hh in range(2):
            qh = jnp.where(lo if hh == 0 else jnp.logical_not(lo), qp, jnp.zeros_like(qp))
            s = lax.dot_general(qh, kd, (((1,), (1,)), ((), ())), preferred_element_type=F32)
            m = jnp.max(s, axis=-1, keepdims=True)
            e = jnp.exp(s - m)
            l = jnp.sum(e, axis=-1, keepdims=True)
            outs.append(jnp.dot(e.astype(BF16), vd, preferred_element_type=F32) * (1.0 / l))
        o_ref[:, cs] = jnp.where(lo, outs[0], outs[1]).astype(BF16)


def _mixb(qb, kvb, seq, tq):
    t = qb.shape[0]
    nq = seq // tq
    return pl.pallas_call(
        _mixb_kernel,
        grid=(t // seq, nq),
        in_specs=[
            pl.BlockSpec((tq, WIDTH_B_Q), lambda b, i: (b * nq + i, 0)),
            pl.BlockSpec((seq, 2 * LANES), lambda b, i: (b, 0)),
            pl.BlockSpec((seq, 2 * LANES), lambda b, i: (b, 1)),
        ],
        out_specs=pl.BlockSpec((tq, WIDTH_B_Q), lambda b, i: (b * nq + i, 0)),
        out_shape=jax.ShapeDtypeStruct((t, WIDTH_B_Q), BF16),
        compiler_params=_cparams(("arbitrary", "arbitrary")),
        name="mixb",
    )(qb, kvb, kvb)


def _merge_kernel(o1_ref, o2_ref, o3_ref, l1_ref, l2_ref, l3_ref, p4t_ref, p16t_ref, ob_ref, gl_ref, bg_ref,
                  x_ref, ex_ref, wpa_ref, wpb_ref, wo_ref, g2_ref, wr_ref, br_ref,
                  x1_ref, h2_ref, idx_ref, gw_ref):
    tm = x_ref.shape[0]
    nsub = tm // TILE
    lane = lax.broadcasted_iota(jnp.int32, (tm, LANES), 1)

    def split(w):
        hi = w.astype(BF16)
        return hi, (w - hi.astype(F32)).astype(BF16)

    def unperm(pt, ref):
        dil = ref.shape[1]
        per = TILE // dil
        outs = []
        for s in range(nsub):
            val = ref[0, :, s * per:(s + 1) * per, :].reshape(TILE, ref.shape[3])
            if val.dtype == BF16:
                outs.append(jnp.dot(pt, val, preferred_element_type=F32))
            else:
                hi, lo_ = split(val)
                outs.append(jnp.dot(pt, hi, preferred_element_type=F32) + jnp.dot(pt, lo_, preferred_element_type=F32))
        return jnp.concatenate(outs, axis=0)

    p4t, p16t = p4t_ref[...], p16t_ref[...]
    o1 = o1_ref[...].astype(F32)
    o2 = unperm(p4t, o2_ref)
    o3 = unperm(p16t, o3_ref)
    l1 = l1_ref[...]
    l2 = unperm(p4t, l2_ref)
    l3 = unperm(p16t, l3_ref)
    mx = jnp.maximum(jnp.maximum(l1, l2), l3)
    e1, e2, e3 = jnp.exp(l1 - mx), jnp.exp(l2 - mx), jnp.exp(l3 - mx)
    rden = 1.0 / (e1 + e2 + e3)
    ex = ex_ref[...]

    def expand(w):
        hi, lo_ = split(w)
        return jnp.dot(hi, ex, preferred_element_type=F32) + jnp.dot(lo_, ex, preferred_element_type=F32)

    oa = expand(e1 * rden) * o1 + expand(e2 * rden) * o2 + expand(e3 * rden) * o3
    pa = jnp.dot(oa.astype(BF16), wpa_ref[...], preferred_element_type=F32)
    pb = jnp.dot(ob_ref[...], wpb_ref[...], preferred_element_type=F32)
    gl = gl_ref[...].astype(F32) + bg_ref[...]
    gates = 1.0 / (1.0 + jnp.exp(-gl))
    merged = gates[:, :D_MODEL] * pa + gates[:, D_MODEL:] * pb
    x1 = x_ref[...] + jnp.dot(merged.astype(BF16), wo_ref[...], preferred_element_type=F32)
    x1_ref[...] = x1
    ms = jnp.mean(x1 * x1, axis=-1, keepdims=True)
    h2 = x1 * lax.rsqrt(ms + EPS) * g2_ref[...]
    for c in range(D_MODEL // LANES):
        h2_ref[pl.ds(c, tm, stride=ROW_SUB), :] = h2[:, c * LANES:(c + 1) * LANES]
    h_hi, h_lo = split(h2)
    w_hi = wr_ref[0]
    w_lo = wr_ref[1]
    logits = (jnp.dot(h_hi, w_hi, preferred_element_type=F32)
              + jnp.dot(h_lo, w_hi, preferred_element_type=F32)
              + jnp.dot(h_hi, w_lo, preferred_element_type=F32)) + br_ref[...]
    work = logits
    lane_f = lane.astype(F32)
    vals, idxs = [], []
    for _ in range(TOP_K):
        m = jnp.max(work, axis=-1, keepdims=True)
        ix = jnp.min(jnp.where(work == m, lane_f, float(LANES)), axis=-1, keepdims=True)
        vals.append(m)
        idxs.append(ix)
        work = jnp.where(lane_f == ix, -jnp.inf, work)
    es = [jnp.exp(v - vals[0]) for v in vals]
    rsum = 1.0 / (es[0] + es[1] + es[2] + es[3])
    idx_out = jnp.zeros((tm, LANES), F32)
    gw_out = jnp.zeros((tm, LANES), F32)
    for k in range(TOP_K):
        idx_out = jnp.where(lane == k, idxs[k], idx_out)
        gw_out = jnp.where(lane == k, es[k] * rsum, gw_out)
    idx_ref[...] = idx_out.astype(jnp.int32)
    gw_ref[...] = gw_out


def _merge(o1, o2, o3, l1, l2, l3, p4t, p16t, ob, gl, bg, x2, ex, wpa, wpb, wo, g2, wr, br, seq):
    t = x2.shape[0]
    tm = MERGE_TM
    npos = seq // tm
    row = lambda i: (i, 0)
    fixed = lambda i: (0, 0)
    d4, d16 = DIL_PAIRS[1][1], DIL_PAIRS[2][1]
    sub = lambda i: (i // npos, 0, i % npos, 0)
    return pl.pallas_call(
        _merge_kernel,
        grid=(t // tm,),
        in_specs=[
            pl.BlockSpec((tm, OUT_A), row),
            pl.BlockSpec((1, d4, tm // d4, OUT_A), sub),
            pl.BlockSpec((1, d16, tm // d16, OUT_A), sub),
            pl.BlockSpec((tm, LANES), row),
            pl.BlockSpec((1, d4, tm // d4, LANES), sub),
            pl.BlockSpec((1, d16, tm // d16, LANES), sub),
            pl.BlockSpec((TILE, TILE), fixed),
            pl.BlockSpec((TILE, TILE), fixed),
            pl.BlockSpec((tm, WIDTH_B_Q), row),
            pl.BlockSpec((tm, 2 * D_MODEL), row),
            pl.BlockSpec((1, 2 * D_MODEL), fixed),
            pl.BlockSpec((tm, D_MODEL), row),
            pl.BlockSpec((LANES, OUT_A), fixed),
            pl.BlockSpec((OUT_A, D_MODEL), fixed),
            pl.BlockSpec((WIDTH_B_Q, D_MODEL), fixed),
            pl.BlockSpec((D_MODEL, D_MODEL), fixed),
            pl.BlockSpec((1, D_MODEL), fixed),
            pl.BlockSpec((2, D_MODEL, LANES), lambda i: (0, 0, 0)),
            pl.BlockSpec((1, LANES), fixed),
        ],
        out_specs=[
            pl.BlockSpec((tm, D_MODEL), row), pl.BlockSpec((tm * ROW_SUB, LANES), row),
            pl.BlockSpec((tm, LANES), row), pl.BlockSpec((tm, LANES), row),
        ],
        out_shape=[
            jax.ShapeDtypeStruct((t, D_MODEL), F32),
            jax.ShapeDtypeStruct((t * ROW_SUB, LANES), F32),
            jax.ShapeDtypeStruct((t, LANES), jnp.int32),
            jax.ShapeDtypeStruct((t, LANES), F32),
        ],
        compiler_params=_cparams(("arbitrary",)),
        name="merge",
    )(o1, o2, o3, l1, l2, l3, p4t, p16t, ob, gl, bg, x2, ex, wpa, wpb, wo, g2, wr, br)


EXP_BM = 256


W_CHUNK = 256


FF_CHUNK = 256
N_DUMP = 2
N_DMA_THREADS = 2


def _experts_kernel(be_ref, nused_ref, tok_ref, tokn_ref, dst_ref, dstp_ref,
                    h2_hbm, wgu_ref, bgu_ref, wd_ref, bd_ref,
                    y_hbm, xbuf, ybuf, wgu_bf, wd_bf, gsem, ssem):
    i = pl.program_id(0)
    nused = nused_ref[0]
    n_real = y_hbm.shape[0] - N_DUMP * EXP_BM

    def gather(tok, s, r):
        return pltpu.make_async_copy(h2_hbm.at[tok], xbuf.at[s, pl.ds(r * ROW_SUB, ROW_SUB)], gsem.at[s])

    def scatter(s, dst, r):
        return pltpu.make_async_copy(ybuf.at[s, pl.ds(r * ROW_SUB, ROW_SUB)], y_hbm.at[dst], ssem.at[s])

    @pl.when(i == 0)
    def _():
        ybuf[...] = jnp.zeros_like(ybuf)
        for r in range(EXP_BM):
            gather(tok_ref[0, 0, r], 0, r).start()
            scatter(0, n_real + r, r).start()

    @pl.when(jnp.logical_and(i < nused, jnp.logical_or(i == 0, be_ref[i] != be_ref[jnp.maximum(i - 1, 0)])))
    def _():
        for c in range(2 * D_FF // W_CHUNK):
            cs = slice(c * W_CHUNK, (c + 1) * W_CHUNK)
            wgu_bf[:, cs] = wgu_ref[0, :, cs].astype(BF16)
        for c in range(D_MODEL // W_CHUNK):
            cs = slice(c * W_CHUNK, (c + 1) * W_CHUNK)
            wd_bf[:, cs] = wd_ref[0, :, cs].astype(BF16)

    def block(slot):
        other = 1 - slot

        @pl.when(nused > 0)
        def _():
            for r in range(EXP_BM):
                gather(tokn_ref[0, 0, r], other, r).start(priority=r % N_DMA_THREADS)

        for r in range(EXP_BM):
            gather(0, slot, r).wait()
        xb = jnp.concatenate([xbuf[slot, pl.ds(c, EXP_BM, stride=ROW_SUB), :] for c in range(ROW_SUB)],
                             axis=1).astype(BF16)
        n_chunks = D_FF // FF_CHUNK
        per = EXP_BM // n_chunks
        y = None
        for f in range(n_chunks):
            gs = slice(f * FF_CHUNK, (f + 1) * FF_CHUNK)
            us = slice(D_FF + f * FF_CHUNK, D_FF + (f + 1) * FF_CHUNK)
            g = jnp.dot(xb, wgu_bf[:, gs], preferred_element_type=F32) + bgu_ref[0, :, gs]
            u = jnp.dot(xb, wgu_bf[:, us], preferred_element_type=F32) + bgu_ref[0, :, us]
            gate = jnp.minimum(g, SWIGLU_LIMIT)
            up = jnp.clip(u, -SWIGLU_LIMIT, SWIGLU_LIMIT)
            act = (up + 1.0) * (gate * (1.0 / (1.0 + jnp.exp(-SWIGLU_ALPHA * gate))))
            part = jnp.dot(act.astype(BF16), wd_bf[gs, :], preferred_element_type=F32)
            y = part if y is None else y + part
            for r in range(f * per, (f + 1) * per):
                scatter(other, dstp_ref[0, 0, r], r).start(priority=r % N_DMA_THREADS)
        y = y + bd_ref[0]

        for r in range(EXP_BM):
            scatter(slot, 0, r).wait()
        for c in range(ROW_SUB):
            ybuf[slot, pl.ds(c, EXP_BM, stride=ROW_SUB), :] = y[:, c * LANES:(c + 1) * LANES]

        @pl.when(i == nused - 1)
        def _():
            for r in range(EXP_BM):
                scatter(slot, dst_ref[0, 0, r], r).start()
            for r in range(EXP_BM):
                scatter(other, 0, r).wait()
                scatter(slot, 0, r).wait()
                gather(0, other, r).wait()

    for parity in range(2):
        pl.when(jnp.logical_and(i < nused, i % 2 == parity))(functools.partial(block, parity))


def _experts(block_e, nused, tok_rows, dst_rows, dst_prev, h2, wgu, bgu, wd, bd, n_out_rows):
    n_blk = block_e.shape[0]
    d = D_MODEL
    idx_spec = lambda f: pl.BlockSpec((1, 1, EXP_BM), f, memory_space=pltpu.SMEM)
    grid_spec = pltpu.PrefetchScalarGridSpec(
        num_scalar_prefetch=2,
        grid=(n_blk,),
        in_specs=[
            idx_spec(lambda i, be, nu: (i, 0, 0)),
            idx_spec(lambda i, be, nu: (jnp.minimum(i + 1, n_blk - 1), 0, 0)),
            idx_spec(lambda i, be, nu: (i, 0, 0)),
            idx_spec(lambda i, be, nu: (i, 0, 0)),
            pl.BlockSpec(memory_space=pl.ANY),
            pl.BlockSpec((1, d, 2 * D_FF), lambda i, be, nu: (be[i], 0, 0)),
            pl.BlockSpec((1, 1, 2 * D_FF), lambda i, be, nu: (be[i], 0, 0)),
            pl.BlockSpec((1, D_FF, d), lambda i, be, nu: (be[i], 0, 0)),
            pl.BlockSpec((1, 1, d), lambda i, be, nu: (be[i], 0, 0)),
        ],
        out_specs=pl.BlockSpec(memory_space=pl.ANY),
        scratch_shapes=[
            pltpu.VMEM((2, EXP_BM * ROW_SUB, LANES), F32),
            pltpu.VMEM((2, EXP_BM * ROW_SUB, LANES), F32),
            pltpu.VMEM((d, 2 * D_FF), BF16),
            pltpu.VMEM((D_FF, d), BF16),
            pltpu.SemaphoreType.DMA((2,)),
            pltpu.SemaphoreType.DMA((2,)),
        ],
    )
    return pl.pallas_call(
        _experts_kernel,
        grid_spec=grid_spec,
        out_shape=jax.ShapeDtypeStruct((n_out_rows, ROW_SUB, LANES), F32),
        compiler_params=_cparams(("arbitrary",)),
        name="experts",
    )(block_e, nused, tok_rows, tok_rows, dst_rows, dst_prev, h2, wgu, bgu, wd, bd)


def _combine_kernel(y0_ref, y1_ref, y2_ref, y3_ref, gw_ref, x1_ref, o_ref):
    gw = gw_ref[...]
    for c in range(ROW_SUB):
        cs = slice(c * LANES, (c + 1) * LANES)
        acc = x1_ref[:, cs]
        for k, y_ref in enumerate((y0_ref, y1_ref, y2_ref, y3_ref)):
            acc = acc + gw[:, k:k + 1] * y_ref[pl.ds(c, TILE, stride=ROW_SUB), :]
        o_ref[:, cs] = acc


def _combine(y, gw, x1):
    t = x1.shape[0]
    nt = t // TILE
    row = lambda i: (i, 0)
    yspec = lambda k: pl.BlockSpec((TILE * ROW_SUB, LANES), lambda i: (k * nt + i, 0))
    return pl.pallas_call(
        _combine_kernel,
        grid=(nt,),
        in_specs=[yspec(0), yspec(1), yspec(2), yspec(3), pl.BlockSpec((TILE, LANES), row),
                  pl.BlockSpec((TILE, D_MODEL), row)],
        out_specs=pl.BlockSpec((TILE, D_MODEL), row),
        out_shape=jax.ShapeDtypeStruct((t, D_MODEL), F32),
        compiler_params=_cparams(("arbitrary",)),
        name="combine",
    )(y, y, y, y, gw, x1)


def _routing_plan(idx):
    t = idx.shape[0]
    a = t * TOP_K
    tok_bits = (t - 1).bit_length()
    key = ((idx << (tok_bits + 2)) | (jnp.arange(t, dtype=jnp.int32)[:, None] << 2)
           | jnp.arange(TOP_K, dtype=jnp.int32)[None, :])
    skey = jnp.sort(key.reshape(-1))
    bounds = jnp.searchsorted(skey, jnp.arange(N_EXPERTS + 1, dtype=jnp.int32) << (tok_bits + 2),
                              method='compare_all').astype(jnp.int32)
    starts, counts = bounds[:-1], bounds[1:] - bounds[:-1]
    padded = ((counts + EXP_BM - 1) // EXP_BM) * EXP_BM
    pends = jnp.cumsum(padded)
    pstarts = pends - padded
    n_blk = a // EXP_BM + N_EXPERTS
    blk = jnp.arange(n_blk, dtype=jnp.int32)
    block_e = jnp.minimum(jnp.sum((pends[None, :] <= (blk * EXP_BM)[:, None]).astype(jnp.int32), axis=1),
                          N_EXPERTS - 1)
    nused = (pends[-1] // EXP_BM).astype(jnp.int32).reshape(1)
    sel = block_e[:, None] == jnp.arange(N_EXPERTS, dtype=jnp.int32)[None, :]
    pick = lambda tab: jnp.sum(jnp.where(sel, tab[None, :], 0), axis=1)
    first = blk * EXP_BM - pick(pstarts)
    q = jnp.arange(EXP_BM, dtype=jnp.int32)[None, :]
    valid = q < (pick(counts) - first)[:, None]
    kv = skey[jnp.clip((pick(starts) + first)[:, None] + q, 0, a - 1)]
    tok = (kv >> 2) & ((1 << tok_bits) - 1)
    tok_rows = jnp.where(valid, tok, 0)
    dump = a + (blk % N_DUMP)[:, None] * EXP_BM + q
    dst_rows = jnp.where(valid, (kv & (TOP_K - 1)) * t + tok, dump)
    dst_prev = jnp.concatenate([a + EXP_BM + q, dst_rows[:-1]], axis=0)
    shape = (n_blk, 1, EXP_BM)
    return block_e, nused, tok_rows.reshape(shape), dst_rows.reshape(shape), dst_prev.reshape(shape)


def kernel(x, norm_mix_g, w_in, b_gate, qn_a, kn_a, qn_b, kn_b, w_proj_a, w_proj_b, w_out,
           norm_ffn_g, w_router, b_router, w_gate_up, b_gate_up, w_down, b_down):
    b, s, d = x.shape
    t = b * s
    x2 = x.reshape(t, d)
    tabs_a, tabs_b = _rope_tables(s)
    d4, d16 = DIL_PAIRS[1][1], DIL_PAIRS[2][1]

    gq = jnp.tile(qn_a[:, None, :], (1, HEADS_A, 1)).reshape(1, WIDTH_A) * SCALE
    gk = jnp.tile(kn_a[:, None, :], (1, HEADS_A, 1)).reshape(1, WIDTH_A)
    gqk = jnp.concatenate([gq, gk], axis=1)
    gqb = jnp.tile(qn_b, HEADS_B_Q).reshape(1, WIDTH_B_Q) * SCALE
    gkb = jnp.tile(kn_b, HEADS_B_KV).reshape(1, WIDTH_B_KV)
    hid = np.arange(MXU_N) // HEAD_DIM
    bd = jnp.asarray(hid[:, None] == hid[None, :], dtype=BF16)

    qkv1, qkv2, qkv3, qb, kvb, gl = _inproj(
        x2, norm_mix_g.reshape(1, d), w_in.astype(BF16), bd, _perm_matrix(d4), _perm_matrix(d16),
        gqk, gqb, gkb, tabs_a, tabs_b, b, s)

    o1, l1 = _mixa(qkv1.reshape(b, s, 3 * OUT_A), s)
    o2, l2 = _mixa(qkv2.reshape(b * d4, s // d4, 3 * OUT_A), s // d4)
    o3, l3 = _mixa(qkv3.reshape(b * d16, s // d16, 3 * OUT_A), s // d16)
    ob = _mixb(qb, kvb, s, MIXB_TQ)

    ex = jnp.asarray(np.arange(LANES)[:, None] == (np.arange(OUT_A) // HEAD_DIM)[None, :], dtype=BF16)
    wr_pad = jnp.zeros((d, LANES), F32).at[:, :N_EXPERTS].set(w_router)
    wr_hi = wr_pad.astype(BF16)
    wr = jnp.stack([wr_hi, (wr_pad - wr_hi.astype(F32)).astype(BF16)])
    br = jnp.full((1, LANES), NEG, F32).at[0, :N_EXPERTS].set(b_router)
    x1, h2, idx_full, gw = _merge(
        o1.reshape(t, OUT_A), o2.reshape(b, d4, s // d4, OUT_A), o3.reshape(b, d16, s // d16, OUT_A),
        l1.reshape(t, LANES), l2.reshape(b, d4, s // d4, LANES), l3.reshape(b, d16, s // d16, LANES),
        _perm_matrix(d4, True), _perm_matrix(d16, True), ob, gl, b_gate.reshape(1, 2 * d), x2, ex,
        w_proj_a.astype(BF16), w_proj_b.astype(BF16), w_out.astype(BF16), norm_ffn_g.reshape(1, d), wr, br, s)

    block_e, nused, tok_rows, dst_rows, dst_prev = _routing_plan(idx_full[:, :TOP_K])
    n_rows = t * TOP_K + N_DUMP * EXP_BM
    y = _experts(block_e, nused, tok_rows, dst_rows, dst_prev, h2.reshape(t, ROW_SUB, LANES), w_gate_up,
                 b_gate_up.reshape(N_EXPERTS, 1, 2 * D_FF), w_down, b_down.reshape(N_EXPERTS, 1, d), n_rows)
    out = _combine(y.reshape(n_rows * ROW_SUB, LANES), gw, x1)
    return out.reshape(b, s, d)
```

```python
import functools

import jax
import jax.numpy as jnp
import numpy as np
from jax import lax
from jax.experimental import pallas as pl
from jax.experimental.pallas import tpu as pltpu

F32 = jnp.float32
BF16 = jnp.bfloat16

D_MODEL = 1024
HEAD_DIM = 64
SCALE = HEAD_DIM ** -0.5
EPS = 1e-6
NEG = -1e30
DIL_PAIRS = ((128, 1), (512, 4), (2048, 16))
HALF_WIN = 64
N_GROUPS = 3
HEADS_A = 8
OUT_A = HEADS_A * HEAD_DIM
WIDTH_A = N_GROUPS * OUT_A
ROT_DIM_A = 16
THETA_PARTIAL = 500000.0
HEADS_B_Q = 8
HEADS_B_KV = 2
WIDTH_B_Q = 512
WIDTH_B_KV = 128
GRID_W = 64
THETA_AXIAL = 10000.0
N_EXPERTS = 32
TOP_K = 4
D_FF = 1024
SWIGLU_LIMIT = 7.0
SWIGLU_ALPHA = 1.702

LANES = 128
MXU_N = 256
VMEM_LIMIT = 56 * 1024 * 1024
TILE = 256
INPROJ_TM = 512
GATE_CHUNK = 2048
MERGE_TM = 512
MIXB_TQ = 512
ROW_SUB = D_MODEL // LANES

COL_VA = 2 * WIDTH_A
COL_QB = 3 * WIDTH_A
COL_KB = COL_QB + WIDTH_B_Q
COL_GATE = COL_KB + 2 * WIDTH_B_KV


def _cparams(sem):
    return pltpu.CompilerParams(dimension_semantics=sem, vmem_limit_bytes=VMEM_LIMIT)


def _perm_rows(dil):
    n = np.arange(TILE)
    per = TILE // dil
    return (n % per) * dil + n // per


def _perm_matrix(dil, transpose=False):
    p = np.zeros((TILE, TILE), np.float32)
    p[np.arange(TILE), _perm_rows(dil)] = 1.0
    return jnp.asarray(p.T if transpose else p, dtype=BF16)


def _rope_tables(seq):
    pos = np.arange(seq, dtype=np.float64)[:, None]
    d = np.arange(LANES) % HEAD_DIM
    half = ROT_DIM_A // 2
    inv = THETA_PARTIAL ** (-((d % half) / half))
    ang = pos * inv[None, :]
    in_rot = (d < ROT_DIM_A)[None, :]
    first = (d < half)[None, :]
    second = ((d >= half) & (d < ROT_DIM_A))[None, :]
    ta = np.stack([np.where(in_rot, np.cos(ang), 1.0), np.where(first, -np.sin(ang), 0.0),
                   np.where(second, np.sin(ang), 0.0)])
    per_group = []
    for _, dil in DIL_PAIRS:
        order = (np.arange(seq // TILE) * TILE)[:, None] + _perm_rows(dil)[None, :]
        per_group.append(ta[:, order.reshape(-1), :])
    hb = HEAD_DIM // 4
    row = np.floor(pos / GRID_W)
    col = pos - row * GRID_W
    invb = THETA_AXIAL ** (-((d % hb) / hb))
    angb = np.where((d < HEAD_DIM // 2)[None, :], row, col) * invb[None, :]
    firstb = ((d % (2 * hb)) < hb)[None, :]
    tb = np.stack([np.cos(angb), np.where(firstb, -np.sin(angb), 0.0), np.where(firstb, 0.0, np.sin(angb))])
    return jnp.asarray(np.stack(per_group), F32), jnp.asarray(tb, F32)


def _inproj_kernel(x_ref, g_ref, w_ref, bd_ref, p4_ref, p16_ref, gqk_ref, gqb_ref, gkb_ref, ta_ref, tb_ref,
                   qkv1_ref, qkv2_ref, qkv3_ref, qb_ref, kvb_ref, gate_ref):
    x = x_ref[...]
    tm = x.shape[0]
    nsub = tm // TILE
    ms = jnp.mean(x * x, axis=-1, keepdims=True)
    h = (x * lax.rsqrt(ms + EPS) * g_ref[...]).astype(BF16)

    def regroup(p_ref):
        return jnp.concatenate(
            [jnp.dot(p_ref[...], h[s * TILE:(s + 1) * TILE], preferred_element_type=F32) for s in range(nsub)],
            axis=0).astype(BF16)

    hg = (h, regroup(p4_ref), regroup(p16_ref))
    bd = bd_ref[...]
    lane = lax.broadcasted_iota(jnp.int32, (tm, LANES), 1)
    lo = lane < HEAD_DIM

    def proj(lhs, c0, width=MXU_N):
        return jnp.dot(lhs, w_ref[:, c0:c0 + width], preferred_element_type=F32)

    def head_norm(y, gain, bdm):
        ss = jnp.dot((y * y).astype(BF16), bdm, preferred_element_type=F32)
        return y * lax.rsqrt(ss * (1.0 / HEAD_DIM) + EPS) * gain

    def rope(z, tab, sh):
        return z * tab[0] + pltpu.roll(z, LANES - sh, 1) * tab[1] + pltpu.roll(z, sh, 1) * tab[2]

    sh_a = ROT_DIM_A // 2
    sh_b = HEAD_DIM // 4

    def store_group(gi, c_out, val):
        if gi == 0:
            qkv1_ref[:, c_out:c_out + LANES] = val
        else:
            ref, dil = ((qkv2_ref, DIL_PAIRS[1][1]), (qkv3_ref, DIL_PAIRS[2][1]))[gi - 1]
            per = TILE // dil
            for s in range(nsub):
                ref[0, :, s * per:(s + 1) * per, c_out:c_out + LANES] = (
                    val[s * TILE:(s + 1) * TILE].reshape(dil, per, LANES))

    for gi in range(N_GROUPS):
        tab = (ta_ref[gi, 0], ta_ref[gi, 1], ta_ref[gi, 2])
        for which in range(3):
            wide = proj(hg[gi], which * WIDTH_A + gi * OUT_A, OUT_A)
            for cc in range(OUT_A // MXU_N):
                c_in = which * WIDTH_A + gi * OUT_A + cc * MXU_N
                y = wide[:, cc * MXU_N:(cc + 1) * MXU_N]
                if which < 2:
                    y = head_norm(y, gqk_ref[:, c_in:c_in + MXU_N], bd)
                for hf in range(2):
                    z = y[:, hf * LANES:(hf + 1) * LANES]
                    if which < 2:
                        z = rope(z, tab, sh_a)
                    store_group(gi, which * OUT_A + cc * MXU_N + hf * LANES, z.astype(BF16))

    tabb = (tb_ref[0], tb_ref[1], tb_ref[2])
    wide_b = proj(h, COL_QB, WIDTH_B_Q + 2 * WIDTH_B_KV)
    for c in range(WIDTH_B_Q // MXU_N):
        c0 = c * MXU_N
        yn = head_norm(wide_b[:, c0:c0 + MXU_N], gqb_ref[:, c0:c0 + MXU_N], bd)
        for hf in range(2):
            z = yn[:, hf * LANES:(hf + 1) * LANES]
            qb_ref[:, c0 + hf * LANES:c0 + (hf + 1) * LANES] = rope(z, tabb, sh_b).astype(BF16)
    ykv = wide_b[:, WIDTH_B_Q:]
    kb = rope(head_norm(ykv[:, :LANES], gkb_ref[...], bd[:LANES, :LANES]), tabb, sh_b)
    vb = ykv[:, LANES:]
    for j, t in enumerate((kb, vb)):
        sw = pltpu.roll(t, HEAD_DIM, 1)
        kvb_ref[:, (2 * j) * LANES:(2 * j + 1) * LANES] = jnp.where(lo, t, sw).astype(BF16)
        kvb_ref[:, (2 * j + 1) * LANES:(2 * j + 2) * LANES] = jnp.where(lo, sw, t).astype(BF16)
    for c in range(2 * D_MODEL // GATE_CHUNK):
        c0 = c * GATE_CHUNK
        gate_ref[:, c0:c0 + GATE_CHUNK] = proj(h, COL_GATE + c0, GATE_CHUNK).astype(BF16)


def _inproj(x2, g, w_bf, bd, p4, p16, gqk, gqb, gkb, tabs_a, tabs_b, batch, seq):
    t = x2.shape[0]
    n_cols = w_bf.shape[1]
    tm = INPROJ_TM
    npos = seq // tm
    row = lambda i: (i, 0)
    fixed = lambda i: (0, 0)
    d4, d16 = DIL_PAIRS[1][1], DIL_PAIRS[2][1]
    sub = lambda i: (i // npos, 0, i % npos, 0)
    return pl.pallas_call(
        _inproj_kernel,
        grid=(t // tm,),
        in_specs=[
            pl.BlockSpec((tm, D_MODEL), row),
            pl.BlockSpec((1, D_MODEL), fixed),
            pl.BlockSpec((D_MODEL, n_cols), fixed, pipeline_mode=pl.Buffered(1)),
            pl.BlockSpec((MXU_N, MXU_N), fixed),
            pl.BlockSpec((TILE, TILE), fixed),
            pl.BlockSpec((TILE, TILE), fixed),
            pl.BlockSpec((1, 2 * WIDTH_A), fixed),
            pl.BlockSpec((1, WIDTH_B_Q), fixed),
            pl.BlockSpec((1, WIDTH_B_KV), fixed),
            pl.BlockSpec((N_GROUPS, 3, tm, LANES), lambda i: (0, 0, i % npos, 0)),
            pl.BlockSpec((3, tm, LANES), lambda i: (0, i % npos, 0)),
        ],
        out_specs=[
            pl.BlockSpec((tm, 3 * OUT_A), row),
            pl.BlockSpec((1, d4, tm // d4, 3 * OUT_A), sub),
            pl.BlockSpec((1, d16, tm // d16, 3 * OUT_A), sub),
            pl.BlockSpec((tm, WIDTH_B_Q), row),
            pl.BlockSpec((tm, 4 * LANES), row),
            pl.BlockSpec((tm, 2 * D_MODEL), row),
        ],
        out_shape=[
            jax.ShapeDtypeStruct((t, 3 * OUT_A), BF16),
            jax.ShapeDtypeStruct((batch, d4, seq // d4, 3 * OUT_A), BF16),
            jax.ShapeDtypeStruct((batch, d16, seq // d16, 3 * OUT_A), BF16),
            jax.ShapeDtypeStruct((t, WIDTH_B_Q), BF16),
            jax.ShapeDtypeStruct((t, 4 * LANES), BF16),
            jax.ShapeDtypeStruct((t, 2 * D_MODEL), BF16),
        ],
        compiler_params=_cparams(("arbitrary",)),
        name="inproj",
    )(x2, g, w_bf, bd, p4, p16, gqk, gqb, gkb, tabs_a, tabs_b)


QBLK = 128
NSUB = 4


def _mixa_kernel(q_ref, k_ref, v_ref, o_ref, lse_ref, *, seq_len, win, batched):
    step = pl.program_id(1)
    lane = lax.broadcasted_iota(jnp.int32, (QBLK, LANES), 1)
    lo = lane < HEAD_DIM
    qi = lax.broadcasted_iota(jnp.int32, (QBLK, win), 0)
    ki = lax.broadcasted_iota(jnp.int32, (QBLK, win), 1)
    for b in range(NSUB):
        if batched:
            sq, r0, blk = b, 0, 0
            kstart = 0
        else:
            sq, r0 = 0, b * QBLK
            blk = step * NSUB + b
            kstart = pl.multiple_of(jnp.clip(blk * QBLK - HALF_WIN, 0, seq_len - win), HALF_WIN)
        valid = jnp.abs((ki + kstart) - (qi + blk * QBLK)) <= HALF_WIN
        scores, vals = [], []
        for p in range(HEADS_A // 2):
            cs = slice(p * LANES, (p + 1) * LANES)
            qp = q_ref[sq, r0:r0 + QBLK, cs]
            kp = k_ref[sq, pl.ds(kstart, win), cs]
            vals.append(v_ref[sq, pl.ds(kstart, win), cs])
            for hh in range(2):
                qh = jnp.where(lo if hh == 0 else jnp.logical_not(lo), qp, jnp.zeros_like(qp))
                scores.append(lax.dot_general(qh, kp, (((1,), (1,)), ((), ())), preferred_element_type=F32))
        s = jnp.where(valid[None], jnp.stack(scores), NEG)
        m = jnp.max(s, axis=-1, keepdims=True)
        e = jnp.exp(s - m)
        l = jnp.sum(e, axis=-1, keepdims=True)
        rl = 1.0 / l
        lse = m + jnp.log(l)
        eb = e.astype(BF16)
        lse_acc = jnp.zeros((QBLK, LANES), F32)
        for p in range(HEADS_A // 2):
            outs = [jnp.dot(eb[2 * p + hh], vals[p], preferred_element_type=F32) * rl[2 * p + hh] for hh in range(2)]
            o_ref[sq, r0:r0 + QBLK, p * LANES:(p + 1) * LANES] = jnp.where(lo, outs[0], outs[1]).astype(BF16)
            for hh in range(2):
                lse_acc = jnp.where(lane == 2 * p + hh, lse[2 * p + hh], lse_acc)
        lse_ref[sq, r0:r0 + QBLK, :] = lse_acc


def _mixa(qkv, seq_len):
    n_seq = qkv.shape[0]
    win = min(2 * QBLK, seq_len)
    batched = seq_len == QBLK
    if batched:
        grid = (n_seq // NSUB, 1)
        qspec = lambda c: pl.BlockSpec((NSUB, QBLK, OUT_A), lambda s, i: (s, 0, c))
        kspec = qspec
        ospec = pl.BlockSpec((NSUB, QBLK, OUT_A), lambda s, i: (s, 0, 0))
        lspec = pl.BlockSpec((NSUB, QBLK, LANES), lambda s, i: (s, 0, 0))
    else:
        rows = NSUB * QBLK
        grid = (n_seq, seq_len // rows)
        qspec = lambda c: pl.BlockSpec((1, rows, OUT_A), lambda s, i: (s, i, c))
        kspec = lambda c: pl.BlockSpec((1, seq_len, OUT_A), lambda s, i: (s, 0, c))
        ospec = pl.BlockSpec((1, rows, OUT_A), lambda s, i: (s, i, 0))
        lspec = pl.BlockSpec((1, rows, LANES), lambda s, i: (s, i, 0))
    return pl.pallas_call(
        functools.partial(_mixa_kernel, seq_len=seq_len, win=win, batched=batched),
        grid=grid,
        in_specs=[qspec(0), kspec(1), kspec(2)],
        out_specs=[ospec, lspec],
        out_shape=[
            jax.ShapeDtypeStruct((n_seq, seq_len, OUT_A), BF16),
            jax.ShapeDtypeStruct((n_seq, seq_len, LANES), F32),
        ],
        compiler_params=_cparams(("arbitrary", "arbitrary")),
        name=f"mixa_len{seq_len}",
    )(qkv, qkv, qkv)


def _mixb_kernel(q_ref, k_ref, v_ref, o_ref):
    tq = q_ref.shape[0]
    lane = lax.broadcasted_iota(jnp.int32, (tq, LANES), 1)
    lo = lane < HEAD_DIM
    for p in range(HEADS_B_Q // 2):
        j = p // 2
        cs = slice(p * LANES, (p + 1) * LANES)
        qp = q_ref[:, cs]
        kd = k_ref[:, j * LANES:(j + 1) * LANES]
        vd = v_ref[:, j * LANES:(j + 1) * LANES]
        outs = []
        for hh in range(2):
            qh = jnp.where(lo if hh == 0 else jnp.logical_not(lo), qp, jnp.zeros_like(qp))
            s = lax.dot_general(qh, kd, (((1,), (1,)), ((), ())), preferred_element_type=F32)
            m = jnp.max(s, axis=-1, keepdims=True)
            e = jnp.exp(s - m)
            l = jnp.sum(e, axis=-1, keepdims=True)
            outs.append(jnp.dot(e.astype(BF16), vd, preferred_element_type=F32) * (1.0 / l))
        o_ref[:, cs] = jnp.where(lo, outs[0], outs[1]).astype(BF16)


def _mixb(qb, kvb, seq, tq):
    t = qb.shape[0]
    nq = seq // tq
    return pl.pallas_call(
        _mixb_kernel,
        grid=(t // seq, nq),
        in_specs=[
            pl.BlockSpec((tq, WIDTH_B_Q), lambda b, i: (b * nq + i, 0)),
            pl.BlockSpec((seq, 2 * LANES), lambda b, i: (b, 0)),
            pl.BlockSpec((seq, 2 * LANES), lambda b, i: (b, 1)),
        ],
        out_specs=pl.BlockSpec((tq, WIDTH_B_Q), lambda b, i: (b * nq + i, 0)),
        out_shape=jax.ShapeDtypeStruct((t, WIDTH_B_Q), BF16),
        compiler_params=_cparams(("arbitrary", "arbitrary")),
        name="mixb",
    )(qb, kvb, kvb)


def _merge_kernel(o1_ref, o2_ref, o3_ref, l1_ref, l2_ref, l3_ref, p4t_ref, p16t_ref, ob_ref, gl_ref, bg_ref,
                  x_ref, ex_ref, wpa_ref, wpb_ref, wo_ref, g2_ref, wr_ref, br_ref,
                  x1_ref, h2_ref, idx_ref, gw_ref):
    tm = x_ref.shape[0]
    nsub = tm // TILE
    lane = lax.broadcasted_iota(jnp.int32, (tm, LANES), 1)

    def split(w):
        hi = w.astype(BF16)
        return hi, (w - hi.astype(F32)).astype(BF16)

    def unperm(pt, ref):
        dil = ref.shape[1]
        per = TILE // dil
        outs = []
        for s in range(nsub):
            val = ref[0, :, s * per:(s + 1) * per, :].reshape(TILE, ref.shape[3])
            if val.dtype == BF16:
                outs.append(jnp.dot(pt, val, preferred_element_type=F32))
            else:
                hi, lo_ = split(val)
                outs.append(jnp.dot(pt, hi, preferred_element_type=F32) + jnp.dot(pt, lo_, preferred_element_type=F32))
        return jnp.concatenate(outs, axis=0)

    p4t, p16t = p4t_ref[...], p16t_ref[...]
    o1 = o1_ref[...].astype(F32)
    o2 = unperm(p4t, o2_ref)
    o3 = unperm(p16t, o3_ref)
    l1 = l1_ref[...]
    l2 = unperm(p4t, l2_ref)
    l3 = unperm(p16t, l3_ref)
    mx = jnp.maximum(jnp.maximum(l1, l2), l3)
    e1, e2, e3 = jnp.exp(l1 - mx), jnp.exp(l2 - mx), jnp.exp(l3 - mx)
    rden = 1.0 / (e1 + e2 + e3)
    packed = jnp.zeros((tm, LANES), F32)
    for g, e in enumerate((e1, e2, e3)):
        hi, lo_ = split(jnp.where(lane < HEADS_A, e * rden, 0.0))
        for part, v in enumerate((hi, lo_)):
            shift = (2 * g + part) * HEADS_A
            vf = v.astype(F32)
            packed = packed + (vf if shift == 0 else pltpu.roll(vf, shift, 1))
    wexp = jnp.dot(packed.astype(BF16), ex_ref[...], preferred_element_type=F32)
    oa = wexp[:, :OUT_A] * o1 + wexp[:, OUT_A:2 * OUT_A] * o2 + wexp[:, 2 * OUT_A:] * o3
    pa = jnp.dot(oa.astype(BF16), wpa_ref[...], preferred_element_type=F32)
    pb = jnp.dot(ob_ref[...], wpb_ref[...], preferred_element_type=F32)
    gl = gl_ref[...].astype(F32) + bg_ref[...]
    gates = 1.0 / (1.0 + jnp.exp(-gl))
    merged = gates[:, :D_MODEL] * pa + gates[:, D_MODEL:] * pb
    x1 = x_ref[...] + jnp.dot(merged.astype(BF16), wo_ref[...], preferred_element_type=F32)
    x1_ref[...] = x1
    ms = jnp.mean(x1 * x1, axis=-1, keepdims=True)
    h2 = x1 * lax.rsqrt(ms + EPS) * g2_ref[...]
    for c in range(D_MODEL // LANES):
        h2_ref[pl.ds(c, tm, stride=ROW_SUB), :] = h2[:, c * LANES:(c + 1) * LANES]
    work = jnp.dot(h2.astype(BF16), wr_ref[...], preferred_element_type=F32) + br_ref[...]
    lane_f = lane.astype(F32)
    vals, idxs = [], []
    for _ in range(TOP_K):
        m = jnp.max(work, axis=-1, keepdims=True)
        ix = jnp.min(jnp.where(work == m, lane_f, float(LANES)), axis=-1, keepdims=True)
        vals.append(m)
        idxs.append(ix)
        work = jnp.where(lane_f == ix, -jnp.inf, work)
    es = [jnp.exp(v - vals[0]) for v in vals]
    rsum = 1.0 / (es[0] + es[1] + es[2] + es[3])
    idx_out = jnp.zeros((tm, LANES), F32)
    gw_out = jnp.zeros((tm, LANES), F32)
    for k in range(TOP_K):
        idx_out = jnp.where(lane == k, idxs[k], idx_out)
        gw_out = jnp.where(lane == k, es[k] * rsum, gw_out)
    idx_ref[...] = idx_out.astype(jnp.int32)
    gw_ref[...] = gw_out


def _merge(o1, o2, o3, l1, l2, l3, p4t, p16t, ob, gl, bg, x2, ex, wpa, wpb, wo, g2, wr, br, seq):
    t = x2.shape[0]
    tm = MERGE_TM
    npos = seq // tm
    row = lambda i: (i, 0)
    fixed = lambda i: (0, 0)
    d4, d16 = DIL_PAIRS[1][1], DIL_PAIRS[2][1]
    sub = lambda i: (i // npos, 0, i % npos, 0)
    return pl.pallas_call(
        _merge_kernel,
        grid=(t // tm,),
        in_specs=[
            pl.BlockSpec((tm, OUT_A), row),
            pl.BlockSpec((1, d4, tm // d4, OUT_A), sub),
            pl.BlockSpec((1, d16, tm // d16, OUT_A), sub),
            pl.BlockSpec((tm, LANES), row),
            pl.BlockSpec((1, d4, tm // d4, LANES), sub),
            pl.BlockSpec((1, d16, tm // d16, LANES), sub),
            pl.BlockSpec((TILE, TILE), fixed),
            pl.BlockSpec((TILE, TILE), fixed),
            pl.BlockSpec((tm, WIDTH_B_Q), row),
            pl.BlockSpec((tm, 2 * D_MODEL), row),
            pl.BlockSpec((1, 2 * D_MODEL), fixed),
            pl.BlockSpec((tm, D_MODEL), row),
            pl.BlockSpec((LANES, N_GROUPS * OUT_A), fixed),
            pl.BlockSpec((OUT_A, D_MODEL), fixed),
            pl.BlockSpec((WIDTH_B_Q, D_MODEL), fixed),
            pl.BlockSpec((D_MODEL, D_MODEL), fixed),
            pl.BlockSpec((1, D_MODEL), fixed),
            pl.BlockSpec((D_MODEL, LANES), fixed),
            pl.BlockSpec((1, LANES), fixed),
        ],
        out_specs=[
            pl.BlockSpec((tm, D_MODEL), row), pl.BlockSpec((tm * ROW_SUB, LANES), row),
            pl.BlockSpec((tm, LANES), row), pl.BlockSpec((tm, LANES), row),
        ],
        out_shape=[
            jax.ShapeDtypeStruct((t, D_MODEL), F32),
            jax.ShapeDtypeStruct((t * ROW_SUB, LANES), F32),
            jax.ShapeDtypeStruct((t, LANES), jnp.int32),
            jax.ShapeDtypeStruct((t, LANES), F32),
        ],
        compiler_params=_cparams(("arbitrary",)),
        name="merge",
    )(o1, o2, o3, l1, l2, l3, p4t, p16t, ob, gl, bg, x2, ex, wpa, wpb, wo, g2, wr, br)


EXP_BM = 256


W_CHUNK = 256


N_DUMP = 2
N_DMA_THREADS = 2


def _experts_kernel(be_ref, nused_ref, idx_ref,
                    h2_hbm, wgu_ref, bgu_ref, wd_ref, bd_ref,
                    y_hbm, xbuf, ybuf, wgu_bf, wd_bf, gsem, ssem):
    i = pl.program_id(0)
    nused = nused_ref[0]
    n_real = y_hbm.shape[0] - N_DUMP * EXP_BM
    tok_cur = lambda r: idx_ref[0, 0, r]
    tok_next = lambda r: idx_ref[0, 0, EXP_BM + r]
    dst_cur = lambda r: idx_ref[0, 0, 2 * EXP_BM + r]
    dst_prev = lambda r: idx_ref[0, 0, 3 * EXP_BM + r]

    def gather(tok, s, r):
        return pltpu.make_async_copy(h2_hbm.at[tok], xbuf.at[s, pl.ds(r * ROW_SUB, ROW_SUB)], gsem.at[s])

    def scatter(s, dst, r):
        return pltpu.make_async_copy(ybuf.at[s, pl.ds(r * ROW_SUB, ROW_SUB)], y_hbm.at[dst], ssem.at[s])

    @pl.when(i == 0)
    def _():
        ybuf[...] = jnp.zeros_like(ybuf)
        for r in range(EXP_BM):
            gather(tok_cur(r), 0, r).start()
            scatter(0, n_real + r, r).start()

    @pl.when(jnp.logical_and(i < nused, jnp.logical_or(i == 0, be_ref[i] != be_ref[jnp.maximum(i - 1, 0)])))
    def _():
        for c in range(2 * D_FF // W_CHUNK):
            cs = slice(c * W_CHUNK, (c + 1) * W_CHUNK)
            wgu_bf[:, cs] = wgu_ref[0, :, cs].astype(BF16)
        for c in range(D_MODEL // W_CHUNK):
            cs = slice(c * W_CHUNK, (c + 1) * W_CHUNK)
            wd_bf[:, cs] = wd_ref[0, :, cs].astype(BF16)

    def block(slot):
        other = 1 - slot

        @pl.when(nused > 0)
        def _():
            for r in range(EXP_BM):
                gather(tok_next(r), other, r).start(priority=r % N_DMA_THREADS)

        for r in range(EXP_BM):
            gather(0, slot, r).wait()
        xb = jnp.concatenate([xbuf[slot, pl.ds(c, EXP_BM, stride=ROW_SUB), :] for c in range(ROW_SUB)],
                             axis=1).astype(BF16)
        g = jnp.dot(xb, wgu_bf[:, :D_FF], preferred_element_type=F32) + bgu_ref[0, :, :D_FF]
        u = jnp.dot(xb, wgu_bf[:, D_FF:], preferred_element_type=F32) + bgu_ref[0, :, D_FF:]
        gate = jnp.minimum(g, SWIGLU_LIMIT)
        up = jnp.clip(u, -SWIGLU_LIMIT, SWIGLU_LIMIT)
        act = (up + 1.0) * (gate * (1.0 / (1.0 + jnp.exp(-SWIGLU_ALPHA * gate))))
        y = jnp.dot(act.astype(BF16), wd_bf[...], preferred_element_type=F32) + bd_ref[0]
        for r in range(EXP_BM):
            scatter(other, dst_prev(r), r).start(priority=r % N_DMA_THREADS)

        for r in range(EXP_BM):
            scatter(slot, 0, r).wait()
        for c in range(ROW_SUB):
            ybuf[slot, pl.ds(c, EXP_BM, stride=ROW_SUB), :] = y[:, c * LANES:(c + 1) * LANES]

        @pl.when(i == nused - 1)
        def _():
            for r in range(EXP_BM):
                scatter(slot, dst_cur(r), r).start()
            for r in range(EXP_BM):
                scatter(other, 0, r).wait()
                scatter(slot, 0, r).wait()
                gather(0, other, r).wait()

    for parity in range(2):
        pl.when(jnp.logical_and(i < nused, i % 2 == parity))(functools.partial(block, parity))


def _experts(block_e, nused, row_idx, h2, wgu, bgu, wd, bd, n_out_rows):
    n_blk = block_e.shape[0]
    d = D_MODEL
    grid_spec = pltpu.PrefetchScalarGridSpec(
        num_scalar_prefetch=2,
        grid=(n_blk,),
        in_specs=[
            pl.BlockSpec((1, 1, 4 * EXP_BM), lambda i, be, nu: (i, 0, 0), memory_space=pltpu.SMEM),
            pl.BlockSpec(memory_space=pl.ANY),
            pl.BlockSpec((1, d, 2 * D_FF), lambda i, be, nu: (be[i], 0, 0)),
            pl.BlockSpec((1, 1, 2 * D_FF), lambda i, be, nu: (be[i], 0, 0)),
            pl.BlockSpec((1, D_FF, d), lambda i, be, nu: (be[i], 0, 0)),
            pl.BlockSpec((1, 1, d), lambda i, be, nu: (be[i], 0, 0)),
        ],
        out_specs=pl.BlockSpec(memory_space=pl.ANY),
        scratch_shapes=[
            pltpu.VMEM((2, EXP_BM * ROW_SUB, LANES), F32),
            pltpu.VMEM((2, EXP_BM * ROW_SUB, LANES), F32),
            pltpu.VMEM((d, 2 * D_FF), BF16),
            pltpu.VMEM((D_FF, d), BF16),
            pltpu.SemaphoreType.DMA((2,)),
            pltpu.SemaphoreType.DMA((2,)),
        ],
    )
    return pl.pallas_call(
        _experts_kernel,
        grid_spec=grid_spec,
        out_shape=jax.ShapeDtypeStruct((n_out_rows, ROW_SUB, LANES), F32),
        compiler_params=_cparams(("arbitrary",)),
        name="experts",
    )(block_e, nused, row_idx, h2, wgu, bgu, wd, bd)


def _combine_kernel(y0_ref, y1_ref, y2_ref, y3_ref, gw_ref, x1_ref, o_ref):
    gw = gw_ref[...]
    for c in range(ROW_SUB):
        cs = slice(c * LANES, (c + 1) * LANES)
        acc = x1_ref[:, cs]
        for k, y_ref in enumerate((y0_ref, y1_ref, y2_ref, y3_ref)):
            acc = acc + gw[:, k:k + 1] * y_ref[pl.ds(c, TILE, stride=ROW_SUB), :]
        o_ref[:, cs] = acc


def _combine(y, gw, x1):
    t = x1.shape[0]
    nt = t // TILE
    row = lambda i: (i, 0)
    yspec = lambda k: pl.BlockSpec((TILE * ROW_SUB, LANES), lambda i: (k * nt + i, 0))
    return pl.pallas_call(
        _combine_kernel,
        grid=(nt,),
        in_specs=[yspec(0), yspec(1), yspec(2), yspec(3), pl.BlockSpec((TILE, LANES), row),
                  pl.BlockSpec((TILE, D_MODEL), row)],
        out_specs=pl.BlockSpec((TILE, D_MODEL), row),
        out_shape=jax.ShapeDtypeStruct((t, D_MODEL), F32),
        compiler_params=_cparams(("arbitrary",)),
        name="combine",
    )(y, y, y, y, gw, x1)


def _routing_plan(idx):
    t = idx.shape[0]
    a = t * TOP_K
    tok_bits = (t - 1).bit_length()
    key = ((idx << (tok_bits + 2)) | (jnp.arange(t, dtype=jnp.int32)[:, None] << 2)
           | jnp.arange(TOP_K, dtype=jnp.int32)[None, :])
    skey = jnp.sort(key.reshape(-1))
    bounds = jnp.searchsorted(skey, jnp.arange(N_EXPERTS + 1, dtype=jnp.int32) << (tok_bits + 2),
                              method='compare_all').astype(jnp.int32)
    starts, counts = bounds[:-1], bounds[1:] - bounds[:-1]
    padded = ((counts + EXP_BM - 1) // EXP_BM) * EXP_BM
    pends = jnp.cumsum(padded)
    pstarts = pends - padded
    n_blk = a // EXP_BM + N_EXPERTS
    blk = jnp.arange(n_blk, dtype=jnp.int32)
    block_e = jnp.minimum(jnp.sum((pends[None, :] <= (blk * EXP_BM)[:, None]).astype(jnp.int32), axis=1),
                          N_EXPERTS - 1)
    nused = (pends[-1] // EXP_BM).astype(jnp.int32).reshape(1)
    sel = block_e[:, None] == jnp.arange(N_EXPERTS, dtype=jnp.int32)[None, :]
    pick = lambda tab: jnp.sum(jnp.where(sel, tab[None, :], 0), axis=1)
    first = blk * EXP_BM - pick(pstarts)
    q = jnp.arange(EXP_BM, dtype=jnp.int32)[None, :]
    valid = q < (pick(counts) - first)[:, None]
    kv = skey[jnp.clip((pick(starts) + first)[:, None] + q, 0, a - 1)]
    tok = (kv >> 2) & ((1 << tok_bits) - 1)
    tok_rows = jnp.where(valid, tok, 0)
    dump = a + (blk % N_DUMP)[:, None] * EXP_BM + q
    dst_rows = jnp.where(valid, (kv & (TOP_K - 1)) * t + tok, dump)
    tok_next = jnp.concatenate([tok_rows[1:], tok_rows[-1:]], axis=0)
    dst_prev = jnp.concatenate([a + EXP_BM + q, dst_rows[:-1]], axis=0)
    row_idx = jnp.concatenate([tok_rows, tok_next, dst_rows, dst_prev], axis=1)
    return block_e, nused, row_idx.reshape(n_blk, 1, 4 * EXP_BM)


def kernel(x, norm_mix_g, w_in, b_gate, qn_a, kn_a, qn_b, kn_b, w_proj_a, w_proj_b, w_out,
           norm_ffn_g, w_router, b_router, w_gate_up, b_gate_up, w_down, b_down):
    b, s, d = x.shape
    t = b * s
    x2 = x.reshape(t, d)
    tabs_a, tabs_b = _rope_tables(s)
    d4, d16 = DIL_PAIRS[1][1], DIL_PAIRS[2][1]

    gq = jnp.tile(qn_a[:, None, :], (1, HEADS_A, 1)).reshape(1, WIDTH_A) * SCALE
    gk = jnp.tile(kn_a[:, None, :], (1, HEADS_A, 1)).reshape(1, WIDTH_A)
    gqk = jnp.concatenate([gq, gk], axis=1)
    gqb = jnp.tile(qn_b, HEADS_B_Q).reshape(1, WIDTH_B_Q) * SCALE
    gkb = jnp.tile(kn_b, HEADS_B_KV).reshape(1, WIDTH_B_KV)
    hid = np.arange(MXU_N) // HEAD_DIM
    bd = jnp.asarray(hid[:, None] == hid[None, :], dtype=BF16)

    qkv1, qkv2, qkv3, qb, kvb, gl = _inproj(
        x2, norm_mix_g.reshape(1, d), w_in.astype(BF16), bd, _perm_matrix(d4), _perm_matrix(d16),
        gqk, gqb, gkb, tabs_a, tabs_b, b, s)

    o1, l1 = _mixa(qkv1.reshape(b, s, 3 * OUT_A), s)
    o2, l2 = _mixa(qkv2.reshape(b * d4, s // d4, 3 * OUT_A), s // d4)
    o3, l3 = _mixa(qkv3.reshape(b * d16, s // d16, 3 * OUT_A), s // d16)
    ob = _mixb(qb, kvb, s, MIXB_TQ)

    pk = np.arange(LANES)
    col = np.arange(N_GROUPS * OUT_A)
    ex = jnp.asarray((pk[:, None] < 2 * N_GROUPS * HEADS_A)
                     & ((pk // (2 * HEADS_A))[:, None] == (col // OUT_A)[None, :])
                     & ((pk % HEADS_A)[:, None] == ((col % OUT_A) // HEAD_DIM)[None, :]), dtype=BF16)
    wr = jnp.zeros((d, LANES), F32).at[:, :N_EXPERTS].set(w_router).astype(BF16)
    br = jnp.full((1, LANES), NEG, F32).at[0, :N_EXPERTS].set(b_router)
    x1, h2, idx_full, gw = _merge(
        o1.reshape(t, OUT_A), o2.reshape(b, d4, s // d4, OUT_A), o3.reshape(b, d16, s // d16, OUT_A),
        l1.reshape(t, LANES), l2.reshape(b, d4, s // d4, LANES), l3.reshape(b, d16, s // d16, LANES),
        _perm_matrix(d4, True), _perm_matrix(d16, True), ob, gl, b_gate.reshape(1, 2 * d), x2, ex,
        w_proj_a.astype(BF16), w_proj_b.astype(BF16), w_out.astype(BF16), norm_ffn_g.reshape(1, d), wr, br, s)

    block_e, nused, row_idx = _routing_plan(idx_full[:, :TOP_K])
    n_rows = t * TOP_K + N_DUMP * EXP_BM
    y = _experts(block_e, nused, row_idx, h2.reshape(t, ROW_SUB, LANES), w_gate_up,
                 b_gate_up.reshape(N_EXPERTS, 1, 2 * D_FF), w_down, b_down.reshape(N_EXPERTS, 1, d), n_rows)
    out = _combine(y.reshape(n_rows * ROW_SUB, LANES), gw, x1)
    return out.reshape(b, s, d)
```

```python
import functools

import jax
import jax.numpy as jnp
import numpy as np
from jax import lax
from jax.experimental import pallas as pl
from jax.experimental.pallas import tpu as pltpu

F32 = jnp.float32
BF16 = jnp.bfloat16

D_MODEL = 1024
HEAD_DIM = 64
SCALE = HEAD_DIM ** -0.5
EPS = 1e-6
NEG = -1e30
DIL_PAIRS = ((128, 1), (512, 4), (2048, 16))
HALF_WIN = 64
N_GROUPS = 3
HEADS_A = 8
OUT_A = HEADS_A * HEAD_DIM
WIDTH_A = N_GROUPS * OUT_A
ROT_DIM_A = 16
THETA_PARTIAL = 500000.0
HEADS_B_Q = 8
HEADS_B_KV = 2
WIDTH_B_Q = 512
WIDTH_B_KV = 128
GRID_W = 64
THETA_AXIAL = 10000.0
N_EXPERTS = 32
TOP_K = 4
D_FF = 1024
SWIGLU_LIMIT = 7.0
SWIGLU_ALPHA = 1.702

LANES = 128
MXU_N = 256
VMEM_LIMIT = 56 * 1024 * 1024
TILE = 256
INPROJ_TM = 512
GATE_CHUNK = 2048
MERGE_TM = 512
MIXB_TQ = 512
ROW_SUB = D_MODEL // LANES

COL_VA = 2 * WIDTH_A
COL_QB = 3 * WIDTH_A
COL_KB = COL_QB + WIDTH_B_Q
COL_GATE = COL_KB + 2 * WIDTH_B_KV


def _cparams(sem):
    return pltpu.CompilerParams(dimension_semantics=sem, vmem_limit_bytes=VMEM_LIMIT)


def _perm_rows(dil):
    n = np.arange(TILE)
    per = TILE // dil
    return (n % per) * dil + n // per


def _perm_matrix(dil, transpose=False):
    p = np.zeros((TILE, TILE), np.float32)
    p[np.arange(TILE), _perm_rows(dil)] = 1.0
    return jnp.asarray(p.T if transpose else p, dtype=BF16)


def _rope_tables(seq):
    pos = np.arange(seq, dtype=np.float64)[:, None]
    d = np.arange(LANES) % HEAD_DIM
    half = ROT_DIM_A // 2
    inv = THETA_PARTIAL ** (-((d % half) / half))
    ang = pos * inv[None, :]
    in_rot = (d < ROT_DIM_A)[None, :]
    first = (d < half)[None, :]
    second = ((d >= half) & (d < ROT_DIM_A))[None, :]
    ta = np.stack([np.where(in_rot, np.cos(ang), 1.0), np.where(first, -np.sin(ang), 0.0),
                   np.where(second, np.sin(ang), 0.0)])
    per_group = []
    for _, dil in DIL_PAIRS:
        order = (np.arange(seq // TILE) * TILE)[:, None] + _perm_rows(dil)[None, :]
        per_group.append(ta[:, order.reshape(-1), :])
    hb = HEAD_DIM // 4
    row = np.floor(pos / GRID_W)
    col = pos - row * GRID_W
    invb = THETA_AXIAL ** (-((d % hb) / hb))
    angb = np.where((d < HEAD_DIM // 2)[None, :], row, col) * invb[None, :]
    firstb = ((d % (2 * hb)) < hb)[None, :]
    tb = np.stack([np.cos(angb), np.where(firstb, -np.sin(angb), 0.0), np.where(firstb, 0.0, np.sin(angb))])
    return jnp.asarray(np.stack(per_group), F32), jnp.asarray(tb, F32)


def _inproj_kernel(x_ref, g_ref, w_ref, bd_ref, p4_ref, p16_ref, gqk_ref, gqb_ref, gkb_ref, ta_ref, tb_ref,
                   qkv1_ref, qkv2_ref, qkv3_ref, qb_ref, kvb_ref, gate_ref):
    x = x_ref[...]
    tm = x.shape[0]
    nsub = tm // TILE
    ms = jnp.mean(x * x, axis=-1, keepdims=True)
    h = (x * lax.rsqrt(ms + EPS) * g_ref[...]).astype(BF16)

    def regroup(p_ref):
        return jnp.concatenate(
            [jnp.dot(p_ref[...], h[s * TILE:(s + 1) * TILE], preferred_element_type=F32) for s in range(nsub)],
            axis=0).astype(BF16)

    hg = (h, regroup(p4_ref), regroup(p16_ref))
    bd = bd_ref[...]
    lane = lax.broadcasted_iota(jnp.int32, (tm, LANES), 1)
    lo = lane < HEAD_DIM

    def proj(lhs, c0, width=MXU_N):
        return jnp.dot(lhs, w_ref[:, c0:c0 + width], preferred_element_type=F32)

    def head_norm(y, gain, bdm):
        ss = jnp.dot((y * y).astype(BF16), bdm, preferred_element_type=F32)
        return y * lax.rsqrt(ss * (1.0 / HEAD_DIM) + EPS) * gain

    def rope(z, tab, sh):
        return z * tab[0] + pltpu.roll(z, LANES - sh, 1) * tab[1] + pltpu.roll(z, sh, 1) * tab[2]

    sh_a = ROT_DIM_A // 2
    sh_b = HEAD_DIM // 4

    def store_group(gi, c_out, val):
        if gi == 0:
            qkv1_ref[:, c_out:c_out + LANES] = val
        else:
            ref, dil = ((qkv2_ref, DIL_PAIRS[1][1]), (qkv3_ref, DIL_PAIRS[2][1]))[gi - 1]
            per = TILE // dil
            for s in range(nsub):
                ref[0, :, s * per:(s + 1) * per, c_out:c_out + LANES] = (
                    val[s * TILE:(s + 1) * TILE].reshape(dil, per, LANES))

    for gi in range(N_GROUPS):
        tab = (ta_ref[gi, 0], ta_ref[gi, 1], ta_ref[gi, 2])
        for which in range(3):
            wide = proj(hg[gi], which * WIDTH_A + gi * OUT_A, OUT_A)
            for cc in range(OUT_A // MXU_N):
                c_in = which * WIDTH_A + gi * OUT_A + cc * MXU_N
                y = wide[:, cc * MXU_N:(cc + 1) * MXU_N]
                if which < 2:
                    y = head_norm(y, gqk_ref[:, c_in:c_in + MXU_N], bd)
                for hf in range(2):
                    z = y[:, hf * LANES:(hf + 1) * LANES]
                    if which < 2:
                        z = rope(z, tab, sh_a)
                    store_group(gi, which * OUT_A + cc * MXU_N + hf * LANES, z.astype(BF16))

    tabb = (tb_ref[0], tb_ref[1], tb_ref[2])
    wide_b = proj(h, COL_QB, WIDTH_B_Q + 2 * WIDTH_B_KV)
    for c in range(WIDTH_B_Q // MXU_N):
        c0 = c * MXU_N
        yn = head_norm(wide_b[:, c0:c0 + MXU_N], gqb_ref[:, c0:c0 + MXU_N], bd)
        for hf in range(2):
            z = yn[:, hf * LANES:(hf + 1) * LANES]
            qb_ref[:, c0 + hf * LANES:c0 + (hf + 1) * LANES] = rope(z, tabb, sh_b).astype(BF16)
    ykv = wide_b[:, WIDTH_B_Q:]
    kb = rope(head_norm(ykv[:, :LANES], gkb_ref[...], bd[:LANES, :LANES]), tabb, sh_b)
    vb = ykv[:, LANES:]
    for j, t in enumerate((kb, vb)):
        sw = pltpu.roll(t, HEAD_DIM, 1)
        kvb_ref[:, (2 * j) * LANES:(2 * j + 1) * LANES] = jnp.where(lo, t, sw).astype(BF16)
        kvb_ref[:, (2 * j + 1) * LANES:(2 * j + 2) * LANES] = jnp.where(lo, sw, t).astype(BF16)
    for c in range(2 * D_MODEL // GATE_CHUNK):
        c0 = c * GATE_CHUNK
        gate_ref[:, c0:c0 + GATE_CHUNK] = proj(h, COL_GATE + c0, GATE_CHUNK).astype(BF16)


def _inproj(x2, g, w_bf, bd, p4, p16, gqk, gqb, gkb, tabs_a, tabs_b, batch, seq):
    t = x2.shape[0]
    n_cols = w_bf.shape[1]
    tm = INPROJ_TM
    npos = seq // tm
    row = lambda i: (i, 0)
    fixed = lambda i: (0, 0)
    d4, d16 = DIL_PAIRS[1][1], DIL_PAIRS[2][1]
    sub = lambda i: (i // npos, 0, i % npos, 0)
    return pl.pallas_call(
        _inproj_kernel,
        grid=(t // tm,),
        in_specs=[
            pl.BlockSpec((tm, D_MODEL), row),
            pl.BlockSpec((1, D_MODEL), fixed),
            pl.BlockSpec((D_MODEL, n_cols), fixed, pipeline_mode=pl.Buffered(1)),
            pl.BlockSpec((MXU_N, MXU_N), fixed),
            pl.BlockSpec((TILE, TILE), fixed),
            pl.BlockSpec((TILE, TILE), fixed),
            pl.BlockSpec((1, 2 * WIDTH_A), fixed),
            pl.BlockSpec((1, WIDTH_B_Q), fixed),
            pl.BlockSpec((1, WIDTH_B_KV), fixed),
            pl.BlockSpec((N_GROUPS, 3, tm, LANES), lambda i: (0, 0, i % npos, 0)),
            pl.BlockSpec((3, tm, LANES), lambda i: (0, i % npos, 0)),
        ],
        out_specs=[
            pl.BlockSpec((tm, 3 * OUT_A), row),
            pl.BlockSpec((1, d4, tm // d4, 3 * OUT_A), sub),
            pl.BlockSpec((1, d16, tm // d16, 3 * OUT_A), sub),
            pl.BlockSpec((tm, WIDTH_B_Q), row),
            pl.BlockSpec((tm, 4 * LANES), row),
            pl.BlockSpec((tm, 2 * D_MODEL), row),
        ],
        out_shape=[
            jax.ShapeDtypeStruct((t, 3 * OUT_A), BF16),
            jax.ShapeDtypeStruct((batch, d4, seq // d4, 3 * OUT_A), BF16),
            jax.ShapeDtypeStruct((batch, d16, seq // d16, 3 * OUT_A), BF16),
            jax.ShapeDtypeStruct((t, WIDTH_B_Q), BF16),
            jax.ShapeDtypeStruct((t, 4 * LANES), BF16),
            jax.ShapeDtypeStruct((t, 2 * D_MODEL), BF16),
        ],
        compiler_params=_cparams(("arbitrary",)),
        name="inproj",
    )(x2, g, w_bf, bd, p4, p16, gqk, gqb, gkb, tabs_a, tabs_b)


QBLK = 128
NSUB = 4


def _mixa_kernel(q_ref, k_ref, v_ref, o_ref, lse_ref, *, seq_len, win, batched):
    step = pl.program_id(1)
    lane = lax.broadcasted_iota(jnp.int32, (QBLK, LANES), 1)
    lo = lane < HEAD_DIM
    qi = lax.broadcasted_iota(jnp.int32, (QBLK, win), 0)
    ki = lax.broadcasted_iota(jnp.int32, (QBLK, win), 1)
    for b in range(NSUB):
        if batched:
            sq, r0, blk = b, 0, 0
            kstart = 0
        else:
            sq, r0 = 0, b * QBLK
            blk = step * NSUB + b
            kstart = pl.multiple_of(jnp.clip(blk * QBLK - HALF_WIN, 0, seq_len - win), HALF_WIN)
        valid = jnp.abs((ki + kstart) - (qi + blk * QBLK)) <= HALF_WIN
        scores, vals = [], []
        for p in range(HEADS_A // 2):
            cs = slice(p * LANES, (p + 1) * LANES)
            qp = q_ref[sq, r0:r0 + QBLK, cs]
            kp = k_ref[sq, pl.ds(kstart, win), cs]
            vals.append(v_ref[sq, pl.ds(kstart, win), cs])
            for hh in range(2):
                qh = jnp.where(lo if hh == 0 else jnp.logical_not(lo), qp, jnp.zeros_like(qp))
                scores.append(lax.dot_general(qh, kp, (((1,), (1,)), ((), ())), preferred_element_type=F32))
        s = jnp.where(valid[None], jnp.stack(scores), NEG)
        m = jnp.max(s, axis=-1, keepdims=True)
        e = jnp.exp(s - m)
        l = jnp.sum(e, axis=-1, keepdims=True)
        rl = 1.0 / l
        lse = m + jnp.log(l)
        eb = e.astype(BF16)
        lse_acc = jnp.zeros((QBLK, LANES), F32)
        for p in range(HEADS_A // 2):
            outs = [jnp.dot(eb[2 * p + hh], vals[p], preferred_element_type=F32) * rl[2 * p + hh] for hh in range(2)]
            o_ref[sq, r0:r0 + QBLK, p * LANES:(p + 1) * LANES] = jnp.where(lo, outs[0], outs[1]).astype(BF16)
            for hh in range(2):
                lse_acc = jnp.where(lane == 2 * p + hh, lse[2 * p + hh], lse_acc)
        lse_ref[sq, r0:r0 + QBLK, :] = lse_acc


def _mixa(qkv, seq_len):
    n_seq = qkv.shape[0]
    win = min(2 * QBLK, seq_len)
    batched = seq_len == QBLK
    if batched:
        grid = (n_seq // NSUB, 1)
        qspec = lambda c: pl.BlockSpec((NSUB, QBLK, OUT_A), lambda s, i: (s, 0, c))
        kspec = qspec
        ospec = pl.BlockSpec((NSUB, QBLK, OUT_A), lambda s, i: (s, 0, 0))
        lspec = pl.BlockSpec((NSUB, QBLK, LANES), lambda s, i: (s, 0, 0))
    else:
        rows = NSUB * QBLK
        grid = (n_seq, seq_len // rows)
        qspec = lambda c: pl.BlockSpec((1, rows, OUT_A), lambda s, i: (s, i, c))
        kspec = lambda c: pl.BlockSpec((1, seq_len, OUT_A), lambda s, i: (s, 0, c))
        ospec = pl.BlockSpec((1, rows, OUT_A), lambda s, i: (s, i, 0))
        lspec = pl.BlockSpec((1, rows, LANES), lambda s, i: (s, i, 0))
    return pl.pallas_call(
        functools.partial(_mixa_kernel, seq_len=seq_len, win=win, batched=batched),
        grid=grid,
        in_specs=[qspec(0), kspec(1), kspec(2)],
        out_specs=[ospec, lspec],
        out_shape=[
            jax.ShapeDtypeStruct((n_seq, seq_len, OUT_A), BF16),
            jax.ShapeDtypeStruct((n_seq, seq_len, LANES), F32),
        ],
        compiler_params=_cparams(("arbitrary", "arbitrary")),
        name=f"mixa_len{seq_len}",
    )(qkv, qkv, qkv)


def _mixb_kernel(q_ref, k_ref, v_ref, o_ref):
    tq = q_ref.shape[0]
    lane = lax.broadcasted_iota(jnp.int32, (tq, LANES), 1)
    lo = lane < HEAD_DIM
    for p in range(HEADS_B_Q // 2):
        j = p // 2
        cs = slice(p * LANES, (p + 1) * LANES)
        qp = q_ref[:, cs]
        kd = k_ref[:, j * LANES:(j + 1) * LANES]
        vd = v_ref[:, j * LANES:(j + 1) * LANES]
        outs = []
        for hh in range(2):
            qh = jnp.where(lo if hh == 0 else jnp.logical_not(lo), qp, jnp.zeros_like(qp))
            s = lax.dot_general(qh, kd, (((1,), (1,)), ((), ())), preferred_element_type=F32)
            m = jnp.max(s, axis=-1, keepdims=True)
            e = jnp.exp(s - m)
            l = jnp.sum(e, axis=-1, keepdims=True)
            outs.append(jnp.dot(e.astype(BF16), vd, preferred_element_type=F32) * (1.0 / l))
        o_ref[:, cs] = jnp.where(lo, outs[0], outs[1]).astype(BF16)


def _mixb(qb, kvb, seq, tq):
    t = qb.shape[0]
    nq = seq // tq
    return pl.pallas_call(
        _mixb_kernel,
        grid=(t // seq, nq),
        in_specs=[
            pl.BlockSpec((tq, WIDTH_B_Q), lambda b, i: (b * nq + i, 0)),
            pl.BlockSpec((seq, 2 * LANES), lambda b, i: (b, 0)),
            pl.BlockSpec((seq, 2 * LANES), lambda b, i: (b, 1)),
        ],
        out_specs=pl.BlockSpec((tq, WIDTH_B_Q), lambda b, i: (b * nq + i, 0)),
        out_shape=jax.ShapeDtypeStruct((t, WIDTH_B_Q), BF16),
        compiler_params=_cparams(("arbitrary", "arbitrary")),
        name="mixb",
    )(qb, kvb, kvb)


def _merge_kernel(o1_ref, o2_ref, o3_ref, l1_ref, l2_ref, l3_ref, p4t_ref, p16t_ref, ob_ref, gl_ref, bg_ref,
                  x_ref, ex_ref, wpa_ref, wpb_ref, wo_ref, g2_ref, wr_ref, br_ref,
                  x1_ref, h2_ref, idx_ref, gw_ref):
    tm = x_ref.shape[0]
    nsub = tm // TILE
    lane = lax.broadcasted_iota(jnp.int32, (tm, LANES), 1)

    def split(w):
        hi = w.astype(BF16)
        return hi, (w - hi.astype(F32)).astype(BF16)

    def unperm(pt, ref):
        dil = ref.shape[1]
        per = TILE // dil
        outs = []
        for s in range(nsub):
            val = ref[0, :, s * per:(s + 1) * per, :].reshape(TILE, ref.shape[3])
            if val.dtype == BF16:
                outs.append(jnp.dot(pt, val, preferred_element_type=F32))
            else:
                hi, lo_ = split(val)
                outs.append(jnp.dot(pt, hi, preferred_element_type=F32) + jnp.dot(pt, lo_, preferred_element_type=F32))
        return jnp.concatenate(outs, axis=0)

    p4t, p16t = p4t_ref[...], p16t_ref[...]
    o1 = o1_ref[...].astype(F32)
    o2 = unperm(p4t, o2_ref)
    o3 = unperm(p16t, o3_ref)
    l1 = l1_ref[...]
    l2 = unperm(p4t, l2_ref)
    l3 = unperm(p16t, l3_ref)
    mx = jnp.maximum(jnp.maximum(l1, l2), l3)
    e1, e2, e3 = jnp.exp(l1 - mx), jnp.exp(l2 - mx), jnp.exp(l3 - mx)
    rden = 1.0 / (e1 + e2 + e3)
    packed = jnp.zeros((tm, LANES), F32)
    for g, e in enumerate((e1, e2, e3)):
        hi, lo_ = split(jnp.where(lane < HEADS_A, e * rden, 0.0))
        for part, v in enumerate((hi, lo_)):
            shift = (2 * g + part) * HEADS_A
            vf = v.astype(F32)
            packed = packed + (vf if shift == 0 else pltpu.roll(vf, shift, 1))
    wexp = jnp.dot(packed.astype(BF16), ex_ref[...], preferred_element_type=F32)
    oa = wexp[:, :OUT_A] * o1 + wexp[:, OUT_A:2 * OUT_A] * o2 + wexp[:, 2 * OUT_A:] * o3
    pa = jnp.dot(oa.astype(BF16), wpa_ref[...], preferred_element_type=F32)
    pb = jnp.dot(ob_ref[...], wpb_ref[...], preferred_element_type=F32)
    gl = gl_ref[...].astype(F32) + bg_ref[...]
    gates = 1.0 / (1.0 + jnp.exp(-gl))
    merged = gates[:, :D_MODEL] * pa + gates[:, D_MODEL:] * pb
    x1 = x_ref[...] + jnp.dot(merged.astype(BF16), wo_ref[...], preferred_element_type=F32)
    x1_ref[...] = x1
    ms = jnp.mean(x1 * x1, axis=-1, keepdims=True)
    h2 = x1 * lax.rsqrt(ms + EPS) * g2_ref[...]
    for c in range(D_MODEL // LANES):
        h2_ref[pl.ds(c, tm, stride=ROW_SUB), :] = h2[:, c * LANES:(c + 1) * LANES]
    work = jnp.dot(h2.astype(BF16), wr_ref[...], preferred_element_type=F32) + br_ref[...]
    lane_f = lane.astype(F32)
    vals, idxs = [], []
    for _ in range(TOP_K):
        m = jnp.max(work, axis=-1, keepdims=True)
        ix = jnp.min(jnp.where(work == m, lane_f, float(LANES)), axis=-1, keepdims=True)
        vals.append(m)
        idxs.append(ix)
        work = jnp.where(lane_f == ix, -jnp.inf, work)
    es = [jnp.exp(v - vals[0]) for v in vals]
    rsum = 1.0 / (es[0] + es[1] + es[2] + es[3])
    idx_out = jnp.zeros((tm, LANES), F32)
    gw_out = jnp.zeros((tm, LANES), F32)
    for k in range(TOP_K):
        idx_out = jnp.where(lane == k, idxs[k], idx_out)
        gw_out = jnp.where(lane == k, es[k] * rsum, gw_out)
    idx_ref[...] = idx_out.astype(jnp.int32)
    gw_ref[...] = gw_out


def _merge(o1, o2, o3, l1, l2, l3, p4t, p16t, ob, gl, bg, x2, ex, wpa, wpb, wo, g2, wr, br, seq):
    t = x2.shape[0]
    tm = MERGE_TM
    npos = seq // tm
    row = lambda i: (i, 0)
    fixed = lambda i: (0, 0)
    d4, d16 = DIL_PAIRS[1][1], DIL_PAIRS[2][1]
    sub = lambda i: (i // npos, 0, i % npos, 0)
    return pl.pallas_call(
        _merge_kernel,
        grid=(t // tm,),
        in_specs=[
            pl.BlockSpec((tm, OUT_A), row),
            pl.BlockSpec((1, d4, tm // d4, OUT_A), sub),
            pl.BlockSpec((1, d16, tm // d16, OUT_A), sub),
            pl.BlockSpec((tm, LANES), row),
            pl.BlockSpec((1, d4, tm // d4, LANES), sub),
            pl.BlockSpec((1, d16, tm // d16, LANES), sub),
            pl.BlockSpec((TILE, TILE), fixed),
            pl.BlockSpec((TILE, TILE), fixed),
            pl.BlockSpec((tm, WIDTH_B_Q), row),
            pl.BlockSpec((tm, 2 * D_MODEL), row),
            pl.BlockSpec((1, 2 * D_MODEL), fixed),
            pl.BlockSpec((tm, D_MODEL), row),
            pl.BlockSpec((LANES, N_GROUPS * OUT_A), fixed),
            pl.BlockSpec((OUT_A, D_MODEL), fixed),
            pl.BlockSpec((WIDTH_B_Q, D_MODEL), fixed),
            pl.BlockSpec((D_MODEL, D_MODEL), fixed),
            pl.BlockSpec((1, D_MODEL), fixed),
            pl.BlockSpec((D_MODEL, LANES), fixed),
            pl.BlockSpec((1, LANES), fixed),
        ],
        out_specs=[
            pl.BlockSpec((tm, D_MODEL), row), pl.BlockSpec((tm * ROW_SUB, LANES), row),
            pl.BlockSpec((tm, LANES), row), pl.BlockSpec((tm, LANES), row),
        ],
        out_shape=[
            jax.ShapeDtypeStruct((t, D_MODEL), F32),
            jax.ShapeDtypeStruct((t * ROW_SUB, LANES), F32),
            jax.ShapeDtypeStruct((t, LANES), jnp.int32),
            jax.ShapeDtypeStruct((t, LANES), F32),
        ],
        compiler_params=_cparams(("arbitrary",)),
        name="merge",
    )(o1, o2, o3, l1, l2, l3, p4t, p16t, ob, gl, bg, x2, ex, wpa, wpb, wo, g2, wr, br)


EXP_BM = 256


W_CHUNK = 256


RING = 3
N_LISTS = 5


def _experts_kernel(be_ref, nused_ref, idx_ref,
                    h2_hbm, wgu_ref, bgu_ref, wd_ref, bd_ref,
                    y_hbm, xbuf, ybuf, wgu_bf, wd_bf, gsem, ssem):
    i = pl.program_id(0)
    nused = nused_ref[0]
    n_real = y_hbm.shape[0] - RING * EXP_BM
    tok_list = lambda k, r: idx_ref[0, 0, k * EXP_BM + r]
    dst_cur = lambda r: idx_ref[0, 0, 3 * EXP_BM + r]
    dst_prev = lambda r: idx_ref[0, 0, 4 * EXP_BM + r]

    def gather(tok, s, r):
        return pltpu.make_async_copy(h2_hbm.at[tok], xbuf.at[s, pl.ds(r * ROW_SUB, ROW_SUB)], gsem.at[s])

    def scatter(s, dst, r):
        return pltpu.make_async_copy(ybuf.at[s, pl.ds(r * ROW_SUB, ROW_SUB)], y_hbm.at[dst], ssem.at[s])

    @pl.when(i == 0)
    def _():
        ybuf[...] = jnp.zeros_like(ybuf)
        for r in range(EXP_BM):
            gather(tok_list(0, r), 0, r).start()
            gather(tok_list(1, r), 1, r).start()
            scatter(0, n_real + r, r).start()
            scatter(1, n_real + EXP_BM + r, r).start()

    @pl.when(jnp.logical_and(i < nused, jnp.logical_or(i == 0, be_ref[i] != be_ref[jnp.maximum(i - 1, 0)])))
    def _():
        for c in range(2 * D_FF // W_CHUNK):
            cs = slice(c * W_CHUNK, (c + 1) * W_CHUNK)
            wgu_bf[:, cs] = wgu_ref[0, :, cs].astype(BF16)
        for c in range(D_MODEL // W_CHUNK):
            cs = slice(c * W_CHUNK, (c + 1) * W_CHUNK)
            wd_bf[:, cs] = wd_ref[0, :, cs].astype(BF16)

    def block(slot):
        prev = (slot - 1) % RING
        ahead = (slot + 2) % RING
        for r in range(EXP_BM):
            gather(0, slot, r).wait()
        xb = jnp.concatenate([xbuf[slot, pl.ds(c, EXP_BM, stride=ROW_SUB), :] for c in range(ROW_SUB)],
                             axis=1).astype(BF16)
        g = jnp.dot(xb, wgu_bf[:, :D_FF], preferred_element_type=F32) + bgu_ref[0, :, :D_FF]
        u = jnp.dot(xb, wgu_bf[:, D_FF:], preferred_element_type=F32) + bgu_ref[0, :, D_FF:]
        gate = jnp.minimum(g, SWIGLU_LIMIT)
        up = jnp.clip(u, -SWIGLU_LIMIT, SWIGLU_LIMIT)
        act = (up + 1.0) * (gate * (1.0 / (1.0 + jnp.exp(-SWIGLU_ALPHA * gate))))
        y = jnp.dot(act.astype(BF16), wd_bf[...], preferred_element_type=F32) + bd_ref[0]
        for r in range(EXP_BM):
            scatter(prev, dst_prev(r), r).start()
            gather(tok_list(2, r), ahead, r).start()

        for r in range(EXP_BM):
            scatter(slot, 0, r).wait()
        for c in range(ROW_SUB):
            ybuf[slot, pl.ds(c, EXP_BM, stride=ROW_SUB), :] = y[:, c * LANES:(c + 1) * LANES]

        @pl.when(i == nused - 1)
        def _():
            for r in range(EXP_BM):
                scatter(slot, dst_cur(r), r).start()
            for r in range(EXP_BM):
                for s in range(RING):
                    scatter(s, 0, r).wait()
                gather(0, (slot + 1) % RING, r).wait()
                gather(0, ahead, r).wait()

    for phase in range(RING):
        pl.when(jnp.logical_and(i < nused, i % RING == phase))(functools.partial(block, phase))


def _experts(block_e, nused, row_idx, h2, wgu, bgu, wd, bd, n_out_rows):
    n_blk = block_e.shape[0]
    d = D_MODEL
    grid_spec = pltpu.PrefetchScalarGridSpec(
        num_scalar_prefetch=2,
        grid=(n_blk,),
        in_specs=[
            pl.BlockSpec((1, 1, N_LISTS * EXP_BM), lambda i, be, nu: (i, 0, 0), memory_space=pltpu.SMEM),
            pl.BlockSpec(memory_space=pl.ANY),
            pl.BlockSpec((1, d, 2 * D_FF), lambda i, be, nu: (be[i], 0, 0)),
            pl.BlockSpec((1, 1, 2 * D_FF), lambda i, be, nu: (be[i], 0, 0)),
            pl.BlockSpec((1, D_FF, d), lambda i, be, nu: (be[i], 0, 0)),
            pl.BlockSpec((1, 1, d), lambda i, be, nu: (be[i], 0, 0)),
        ],
        out_specs=pl.BlockSpec(memory_space=pl.ANY),
        scratch_shapes=[
            pltpu.VMEM((RING, EXP_BM * ROW_SUB, LANES), F32),
            pltpu.VMEM((RING, EXP_BM * ROW_SUB, LANES), F32),
            pltpu.VMEM((d, 2 * D_FF), BF16),
            pltpu.VMEM((D_FF, d), BF16),
            pltpu.SemaphoreType.DMA((RING,)),
            pltpu.SemaphoreType.DMA((RING,)),
        ],
    )
    return pl.pallas_call(
        _experts_kernel,
        grid_spec=grid_spec,
        out_shape=jax.ShapeDtypeStruct((n_out_rows, ROW_SUB, LANES), F32),
        compiler_params=_cparams(("arbitrary",)),
        name="experts",
    )(block_e, nused, row_idx, h2, wgu, bgu, wd, bd)


def _combine_kernel(y0_ref, y1_ref, y2_ref, y3_ref, gw_ref, x1_ref, o_ref):
    gw = gw_ref[...]
    for c in range(ROW_SUB):
        cs = slice(c * LANES, (c + 1) * LANES)
        acc = x1_ref[:, cs]
        for k, y_ref in enumerate((y0_ref, y1_ref, y2_ref, y3_ref)):
            acc = acc + gw[:, k:k + 1] * y_ref[pl.ds(c, TILE, stride=ROW_SUB), :]
        o_ref[:, cs] = acc


def _combine(y, gw, x1):
    t = x1.shape[0]
    nt = t // TILE
    row = lambda i: (i, 0)
    yspec = lambda k: pl.BlockSpec((TILE * ROW_SUB, LANES), lambda i: (k * nt + i, 0))
    return pl.pallas_call(
        _combine_kernel,
        grid=(nt,),
        in_specs=[yspec(0), yspec(1), yspec(2), yspec(3), pl.BlockSpec((TILE, LANES), row),
                  pl.BlockSpec((TILE, D_MODEL), row)],
        out_specs=pl.BlockSpec((TILE, D_MODEL), row),
        out_shape=jax.ShapeDtypeStruct((t, D_MODEL), F32),
        compiler_params=_cparams(("arbitrary",)),
        name="combine",
    )(y, y, y, y, gw, x1)


def _routing_plan(idx):
    t = idx.shape[0]
    a = t * TOP_K
    tok_bits = (t - 1).bit_length()
    key = ((idx << (tok_bits + 2)) | (jnp.arange(t, dtype=jnp.int32)[:, None] << 2)
           | jnp.arange(TOP_K, dtype=jnp.int32)[None, :])
    skey = jnp.sort(key.reshape(-1))
    bounds = jnp.searchsorted(skey, jnp.arange(N_EXPERTS + 1, dtype=jnp.int32) << (tok_bits + 2),
                              method='compare_all').astype(jnp.int32)
    starts, counts = bounds[:-1], bounds[1:] - bounds[:-1]
    padded = ((counts + EXP_BM - 1) // EXP_BM) * EXP_BM
    pends = jnp.cumsum(padded)
    pstarts = pends - padded
    n_blk = a // EXP_BM + N_EXPERTS
    blk = jnp.arange(n_blk, dtype=jnp.int32)
    block_e = jnp.minimum(jnp.sum((pends[None, :] <= (blk * EXP_BM)[:, None]).astype(jnp.int32), axis=1),
                          N_EXPERTS - 1)
    nused = (pends[-1] // EXP_BM).astype(jnp.int32).reshape(1)
    sel = block_e[:, None] == jnp.arange(N_EXPERTS, dtype=jnp.int32)[None, :]
    pick = lambda tab: jnp.sum(jnp.where(sel, tab[None, :], 0), axis=1)
    first = blk * EXP_BM - pick(pstarts)
    q = jnp.arange(EXP_BM, dtype=jnp.int32)[None, :]
    valid = q < (pick(counts) - first)[:, None]
    kv = skey[jnp.clip((pick(starts) + first)[:, None] + q, 0, a - 1)]
    tok = (kv >> 2) & ((1 << tok_bits) - 1)
    tok_rows = jnp.where(valid, tok, 0)
    dump = a + (blk % RING)[:, None] * EXP_BM + q
    dst_rows = jnp.where(valid, (kv & (TOP_K - 1)) * t + tok, dump)
    tok_next = jnp.concatenate([tok_rows[1:], tok_rows[-1:]], axis=0)
    tok_next2 = jnp.concatenate([tok_rows[2:], tok_rows[-1:], tok_rows[-1:]], axis=0)
    dst_prev = jnp.concatenate([a + (RING - 1) * EXP_BM + q, dst_rows[:-1]], axis=0)
    row_idx = jnp.concatenate([tok_rows, tok_next, tok_next2, dst_rows, dst_prev], axis=1)
    return block_e, nused, row_idx.reshape(n_blk, 1, N_LISTS * EXP_BM)


def kernel(x, norm_mix_g, w_in, b_gate, qn_a, kn_a, qn_b, kn_b, w_proj_a, w_proj_b, w_out,
           norm_ffn_g, w_router, b_router, w_gate_up, b_gate_up, w_down, b_down):
    b, s, d = x.shape
    t = b * s
    x2 = x.reshape(t, d)
    tabs_a, tabs_b = _rope_tables(s)
    d4, d16 = DIL_PAIRS[1][1], DIL_PAIRS[2][1]

    gq = jnp.tile(qn_a[:, None, :], (1, HEADS_A, 1)).reshape(1, WIDTH_A) * SCALE
    gk = jnp.tile(kn_a[:, None, :], (1, HEADS_A, 1)).reshape(1, WIDTH_A)
    gqk = jnp.concatenate([gq, gk], axis=1)
    gqb = jnp.tile(qn_b, HEADS_B_Q).reshape(1, WIDTH_B_Q) * SCALE
    gkb = jnp.tile(kn_b, HEADS_B_KV).reshape(1, WIDTH_B_KV)
    hid = np.arange(MXU_N) // HEAD_DIM
    bd = jnp.asarray(hid[:, None] == hid[None, :], dtype=BF16)

    qkv1, qkv2, qkv3, qb, kvb, gl = _inproj(
        x2, norm_mix_g.reshape(1, d), w_in.astype(BF16), bd, _perm_matrix(d4), _perm_matrix(d16),
        gqk, gqb, gkb, tabs_a, tabs_b, b, s)

    o1, l1 = _mixa(qkv1.reshape(b, s, 3 * OUT_A), s)
    o2, l2 = _mixa(qkv2.reshape(b * d4, s // d4, 3 * OUT_A), s // d4)
    o3, l3 = _mixa(qkv3.reshape(b * d16, s // d16, 3 * OUT_A), s // d16)
    ob = _mixb(qb, kvb, s, MIXB_TQ)

    pk = np.arange(LANES)
    col = np.arange(N_GROUPS * OUT_A)
    ex = jnp.asarray((pk[:, None] < 2 * N_GROUPS * HEADS_A)
                     & ((pk // (2 * HEADS_A))[:, None] == (col // OUT_A)[None, :])
                     & ((pk % HEADS_A)[:, None] == ((col % OUT_A) // HEAD_DIM)[None, :]), dtype=BF16)
    wr = jnp.zeros((d, LANES), F32).at[:, :N_EXPERTS].set(w_router).astype(BF16)
    br = jnp.full((1, LANES), NEG, F32).at[0, :N_EXPERTS].set(b_router)
    x1, h2, idx_full, gw = _merge(
        o1.reshape(t, OUT_A), o2.reshape(b, d4, s // d4, OUT_A), o3.reshape(b, d16, s // d16, OUT_A),
        l1.reshape(t, LANES), l2.reshape(b, d4, s // d4, LANES), l3.reshape(b, d16, s // d16, LANES),
        _perm_matrix(d4, True), _perm_matrix(d16, True), ob, gl, b_gate.reshape(1, 2 * d), x2, ex,
        w_proj_a.astype(BF16), w_proj_b.astype(BF16), w_out.astype(BF16), norm_ffn_g.reshape(1, d), wr, br, s)

    block_e, nused, row_idx = _routing_plan(idx_full[:, :TOP_K])
    n_rows = t * TOP_K + RING * EXP_BM
    y = _experts(block_e, nused, row_idx, h2.reshape(t, ROW_SUB, LANES), w_gate_up,
                 b_gate_up.reshape(N_EXPERTS, 1, 2 * D_FF), w_down, b_down.reshape(N_EXPERTS, 1, d), n_rows)
    out = _combine(y.reshape(n_rows * ROW_SUB, LANES), gw, x1)
    return out.reshape(b, s, d)
```

```python
import functools

import jax
import jax.numpy as jnp
import numpy as np
from jax import lax
from jax.experimental import pallas as pl
from jax.experimental.pallas import tpu as pltpu

F32 = jnp.float32
BF16 = jnp.bfloat16

D_MODEL = 1024
HEAD_DIM = 64
SCALE = HEAD_DIM ** -0.5
EPS = 1e-6
NEG = -1e30
DIL_PAIRS = ((128, 1), (512, 4), (2048, 16))
HALF_WIN = 64
N_GROUPS = 3
HEADS_A = 8
OUT_A = HEADS_A * HEAD_DIM
WIDTH_A = N_GROUPS * OUT_A
ROT_DIM_A = 16
THETA_PARTIAL = 500000.0
HEADS_B_Q = 8
HEADS_B_KV = 2
WIDTH_B_Q = 512
WIDTH_B_KV = 128
GRID_W = 64
THETA_AXIAL = 10000.0
N_EXPERTS = 32
TOP_K = 4
D_FF = 1024
SWIGLU_LIMIT = 7.0
SWIGLU_ALPHA = 1.702

LANES = 128
MXU_N = 256
VMEM_LIMIT = 56 * 1024 * 1024
TILE = 256
INPROJ_TM = 512
GATE_CHUNK = 2048
MERGE_TM = 512
COMBINE_TM = 512
MIXB_TQ = 512
ROW_SUB = D_MODEL // LANES

COL_VA = 2 * WIDTH_A
COL_QB = 3 * WIDTH_A
COL_KB = COL_QB + WIDTH_B_Q
COL_GATE = COL_KB + 2 * WIDTH_B_KV


def _cparams(sem):
    return pltpu.CompilerParams(dimension_semantics=sem, vmem_limit_bytes=VMEM_LIMIT)


def _perm_rows(dil):
    n = np.arange(TILE)
    per = TILE // dil
    return (n % per) * dil + n // per


def _perm_matrix(dil, transpose=False):
    p = np.zeros((TILE, TILE), np.float32)
    p[np.arange(TILE), _perm_rows(dil)] = 1.0
    return jnp.asarray(p.T if transpose else p, dtype=BF16)


def _rope_tables(seq):
    pos = np.arange(seq, dtype=np.float64)[:, None]
    d = np.arange(LANES) % HEAD_DIM
    half = ROT_DIM_A // 2
    inv = THETA_PARTIAL ** (-((d % half) / half))
    ang = pos * inv[None, :]
    in_rot = (d < ROT_DIM_A)[None, :]
    first = (d < half)[None, :]
    second = ((d >= half) & (d < ROT_DIM_A))[None, :]
    ta = np.stack([np.where(in_rot, np.cos(ang), 1.0), np.where(first, -np.sin(ang), 0.0),
                   np.where(second, np.sin(ang), 0.0)])
    per_group = []
    for _, dil in DIL_PAIRS:
        order = (np.arange(seq // TILE) * TILE)[:, None] + _perm_rows(dil)[None, :]
        per_group.append(ta[:, order.reshape(-1), :])
    hb = HEAD_DIM // 4
    row = np.floor(pos / GRID_W)
    col = pos - row * GRID_W
    invb = THETA_AXIAL ** (-((d % hb) / hb))
    angb = np.where((d < HEAD_DIM // 2)[None, :], row, col) * invb[None, :]
    firstb = ((d % (2 * hb)) < hb)[None, :]
    tb = np.stack([np.cos(angb), np.where(firstb, -np.sin(angb), 0.0), np.where(firstb, 0.0, np.sin(angb))])
    return jnp.asarray(np.stack(per_group), F32), jnp.asarray(tb, F32)


def _inproj_kernel(x_ref, g_ref, w_ref, bd_ref, p4_ref, p16_ref, gqk_ref, gqb_ref, gkb_ref, ta_ref, tb_ref,
                   qkv1_ref, qkv2_ref, qkv3_ref, qb_ref, kvb_ref, gate_ref):
    x = x_ref[...]
    tm = x.shape[0]
    nsub = tm // TILE
    ms = jnp.mean(x * x, axis=-1, keepdims=True)
    h = (x * lax.rsqrt(ms + EPS) * g_ref[...]).astype(BF16)

    def regroup(p_ref):
        return jnp.concatenate(
            [jnp.dot(p_ref[...], h[s * TILE:(s + 1) * TILE], preferred_element_type=F32) for s in range(nsub)],
            axis=0).astype(BF16)

    hg = (h, regroup(p4_ref), regroup(p16_ref))
    bd = bd_ref[...]
    lane = lax.broadcasted_iota(jnp.int32, (tm, LANES), 1)
    lo = lane < HEAD_DIM

    def proj(lhs, c0, width=MXU_N):
        return jnp.dot(lhs, w_ref[:, c0:c0 + width], preferred_element_type=F32)

    def head_norm(y, gain, bdm):
        ss = jnp.dot((y * y).astype(BF16), bdm, preferred_element_type=F32)
        return y * lax.rsqrt(ss * (1.0 / HEAD_DIM) + EPS) * gain

    def rope(z, tab, sh):
        return z * tab[0] + pltpu.roll(z, LANES - sh, 1) * tab[1] + pltpu.roll(z, sh, 1) * tab[2]

    sh_a = ROT_DIM_A // 2
    sh_b = HEAD_DIM // 4

    def store_group(gi, c_out, val):
        if gi == 0:
            qkv1_ref[:, c_out:c_out + LANES] = val
        else:
            ref, dil = ((qkv2_ref, DIL_PAIRS[1][1]), (qkv3_ref, DIL_PAIRS[2][1]))[gi - 1]
            per = TILE // dil
            for s in range(nsub):
                ref[0, :, s * per:(s + 1) * per, c_out:c_out + LANES] = (
                    val[s * TILE:(s + 1) * TILE].reshape(dil, per, LANES))

    for gi in range(N_GROUPS):
        tab = (ta_ref[gi, 0], ta_ref[gi, 1], ta_ref[gi, 2])
        for which in range(3):
            wide = proj(hg[gi], which * WIDTH_A + gi * OUT_A, OUT_A)
            for cc in range(OUT_A // MXU_N):
                c_in = which * WIDTH_A + gi * OUT_A + cc * MXU_N
                y = wide[:, cc * MXU_N:(cc + 1) * MXU_N]
                if which < 2:
                    y = head_norm(y, gqk_ref[:, c_in:c_in + MXU_N], bd)
                for hf in range(2):
                    z = y[:, hf * LANES:(hf + 1) * LANES]
                    if which < 2:
                        z = rope(z, tab, sh_a)
                    store_group(gi, which * OUT_A + cc * MXU_N + hf * LANES, z.astype(BF16))

    tabb = (tb_ref[0], tb_ref[1], tb_ref[2])
    wide_b = proj(h, COL_QB, WIDTH_B_Q + 2 * WIDTH_B_KV)
    for c in range(WIDTH_B_Q // MXU_N):
        c0 = c * MXU_N
        yn = head_norm(wide_b[:, c0:c0 + MXU_N], gqb_ref[:, c0:c0 + MXU_N], bd)
        for hf in range(2):
            z = yn[:, hf * LANES:(hf + 1) * LANES]
            qb_ref[:, c0 + hf * LANES:c0 + (hf + 1) * LANES] = rope(z, tabb, sh_b).astype(BF16)
    ykv = wide_b[:, WIDTH_B_Q:]
    kb = rope(head_norm(ykv[:, :LANES], gkb_ref[...], bd[:LANES, :LANES]), tabb, sh_b)
    vb = ykv[:, LANES:]
    for j, t in enumerate((kb, vb)):
        sw = pltpu.roll(t, HEAD_DIM, 1)
        kvb_ref[:, (2 * j) * LANES:(2 * j + 1) * LANES] = jnp.where(lo, t, sw).astype(BF16)
        kvb_ref[:, (2 * j + 1) * LANES:(2 * j + 2) * LANES] = jnp.where(lo, sw, t).astype(BF16)
    for c in range(2 * D_MODEL // GATE_CHUNK):
        c0 = c * GATE_CHUNK
        gate_ref[:, c0:c0 + GATE_CHUNK] = proj(h, COL_GATE + c0, GATE_CHUNK).astype(BF16)


def _inproj(x2, g, w_bf, bd, p4, p16, gqk, gqb, gkb, tabs_a, tabs_b, batch, seq):
    t = x2.shape[0]
    n_cols = w_bf.shape[1]
    tm = INPROJ_TM
    npos = seq // tm
    row = lambda i: (i, 0)
    fixed = lambda i: (0, 0)
    d4, d16 = DIL_PAIRS[1][1], DIL_PAIRS[2][1]
    sub = lambda i: (i // npos, 0, i % npos, 0)
    return pl.pallas_call(
        _inproj_kernel,
        grid=(t // tm,),
        in_specs=[
            pl.BlockSpec((tm, D_MODEL), row),
            pl.BlockSpec((1, D_MODEL), fixed),
            pl.BlockSpec((D_MODEL, n_cols), fixed, pipeline_mode=pl.Buffered(1)),
            pl.BlockSpec((MXU_N, MXU_N), fixed),
            pl.BlockSpec((TILE, TILE), fixed),
            pl.BlockSpec((TILE, TILE), fixed),
            pl.BlockSpec((1, 2 * WIDTH_A), fixed),
            pl.BlockSpec((1, WIDTH_B_Q), fixed),
            pl.BlockSpec((1, WIDTH_B_KV), fixed),
            pl.BlockSpec((N_GROUPS, 3, tm, LANES), lambda i: (0, 0, i % npos, 0)),
            pl.BlockSpec((3, tm, LANES), lambda i: (0, i % npos, 0)),
        ],
        out_specs=[
            pl.BlockSpec((tm, 3 * OUT_A), row),
            pl.BlockSpec((1, d4, tm // d4, 3 * OUT_A), sub),
            pl.BlockSpec((1, d16, tm // d16, 3 * OUT_A), sub),
            pl.BlockSpec((tm, WIDTH_B_Q), row),
            pl.BlockSpec((tm, 4 * LANES), row),
            pl.BlockSpec((tm, 2 * D_MODEL), row),
        ],
        out_shape=[
            jax.ShapeDtypeStruct((t, 3 * OUT_A), BF16),
            jax.ShapeDtypeStruct((batch, d4, seq // d4, 3 * OUT_A), BF16),
            jax.ShapeDtypeStruct((batch, d16, seq // d16, 3 * OUT_A), BF16),
            jax.ShapeDtypeStruct((t, WIDTH_B_Q), BF16),
            jax.ShapeDtypeStruct((t, 4 * LANES), BF16),
            jax.ShapeDtypeStruct((t, 2 * D_MODEL), BF16),
        ],
        compiler_params=_cparams(("arbitrary",)),
        name="inproj",
    )(x2, g, w_bf, bd, p4, p16, gqk, gqb, gkb, tabs_a, tabs_b)


QBLK = 128
NSUB = 8


def _mixa_kernel(q_ref, k_ref, v_ref, o_ref, lse_ref, *, seq_len, win, seqs, nblk):
    step = pl.program_id(1)
    lane = lax.broadcasted_iota(jnp.int32, (QBLK, LANES), 1)
    lo = lane < HEAD_DIM
    qi = lax.broadcasted_iota(jnp.int32, (QBLK, win), 0)
    ki = lax.broadcasted_iota(jnp.int32, (QBLK, win), 1)
    for b in range(seqs * nblk):
        sq, r0 = b // nblk, (b % nblk) * QBLK
        if seq_len == win:
            blk, kstart = 0, 0
        else:
            blk = step * nblk + b % nblk
            kstart = pl.multiple_of(jnp.clip(blk * QBLK - HALF_WIN, 0, seq_len - win), HALF_WIN)
        valid = jnp.abs((ki + kstart) - (qi + blk * QBLK)) <= HALF_WIN
        scores, vals = [], []
        for p in range(HEADS_A // 2):
            cs = slice(p * LANES, (p + 1) * LANES)
            qp = q_ref[sq, r0:r0 + QBLK, cs]
            kp = k_ref[sq, pl.ds(kstart, win), cs]
            vals.append(v_ref[sq, pl.ds(kstart, win), cs])
            for hh in range(2):
                qh = jnp.where(lo if hh == 0 else jnp.logical_not(lo), qp, jnp.zeros_like(qp))
                scores.append(lax.dot_general(qh, kp, (((1,), (1,)), ((), ())), preferred_element_type=F32))
        s = jnp.where(valid[None], jnp.stack(scores), NEG)
        m = jnp.max(s, axis=-1, keepdims=True)
        e = jnp.exp(s - m)
        l = jnp.sum(e, axis=-1, keepdims=True)
        rl = 1.0 / l
        lse = m + jnp.log(l)
        eb = e.astype(BF16)
        lse_acc = jnp.zeros((QBLK, LANES), F32)
        for p in range(HEADS_A // 2):
            outs = [jnp.dot(eb[2 * p + hh], vals[p], preferred_element_type=F32) * rl[2 * p + hh] for hh in range(2)]
            o_ref[sq, r0:r0 + QBLK, p * LANES:(p + 1) * LANES] = jnp.where(lo, outs[0], outs[1]).astype(BF16)
            for hh in range(2):
                lse_acc = jnp.where(lane == 2 * p + hh, lse[2 * p + hh], lse_acc)
        lse_ref[sq, r0:r0 + QBLK, :] = lse_acc


def _mixa(qkv, seq_len):
    n_seq = qkv.shape[0]
    win = min(2 * QBLK, seq_len)
    nblk = min(NSUB, seq_len // QBLK)
    seqs = NSUB // nblk
    rows = nblk * QBLK
    grid = (n_seq // seqs, seq_len // rows)
    qspec = lambda c: pl.BlockSpec((seqs, rows, OUT_A), lambda s, i: (s, i, c))
    kspec = lambda c: pl.BlockSpec((seqs, seq_len, OUT_A), lambda s, i: (s, 0, c))
    ospec = pl.BlockSpec((seqs, rows, OUT_A), lambda s, i: (s, i, 0))
    lspec = pl.BlockSpec((seqs, rows, LANES), lambda s, i: (s, i, 0))
    return pl.pallas_call(
        functools.partial(_mixa_kernel, seq_len=seq_len, win=win, seqs=seqs, nblk=nblk),
        grid=grid,
        in_specs=[qspec(0), kspec(1), kspec(2)],
        out_specs=[ospec, lspec],
        out_shape=[
            jax.ShapeDtypeStruct((n_seq, seq_len, OUT_A), BF16),
            jax.ShapeDtypeStruct((n_seq, seq_len, LANES), F32),
        ],
        compiler_params=_cparams(("arbitrary", "arbitrary")),
        name=f"mixa_len{seq_len}",
    )(qkv, qkv, qkv)


def _mixb_kernel(q_ref, k_ref, v_ref, o_ref):
    tq = q_ref.shape[0]
    lane = lax.broadcasted_iota(jnp.int32, (tq, LANES), 1)
    lo = lane < HEAD_DIM
    for p in range(HEADS_B_Q // 2):
        j = p // 2
        cs = slice(p * LANES, (p + 1) * LANES)
        qp = q_ref[:, cs]
        kd = k_ref[:, j * LANES:(j + 1) * LANES]
        vd = v_ref[:, j * LANES:(j + 1) * LANES]
        outs = []
        for hh in range(2):
            qh = jnp.where(lo if hh == 0 else jnp.logical_not(lo), qp, jnp.zeros_like(qp))
            s = lax.dot_general(qh, kd, (((1,), (1,)), ((), ())), preferred_element_type=F32)
            m = jnp.max(s, axis=-1, keepdims=True)
            e = jnp.exp(s - m)
            l = jnp.sum(e, axis=-1, keepdims=True)
            outs.append(jnp.dot(e.astype(BF16), vd, preferred_element_type=F32) * (1.0 / l))
        o_ref[:, cs] = jnp.where(lo, outs[0], outs[1]).astype(BF16)


def _mixb(qb, kvb, seq, tq):
    t = qb.shape[0]
    nq = seq // tq
    return pl.pallas_call(
        _mixb_kernel,
        grid=(t // seq, nq),
        in_specs=[
            pl.BlockSpec((tq, WIDTH_B_Q), lambda b, i: (b * nq + i, 0)),
            pl.BlockSpec((seq, 2 * LANES), lambda b, i: (b, 0)),
            pl.BlockSpec((seq, 2 * LANES), lambda b, i: (b, 1)),
        ],
        out_specs=pl.BlockSpec((tq, WIDTH_B_Q), lambda b, i: (b * nq + i, 0)),
        out_shape=jax.ShapeDtypeStruct((t, WIDTH_B_Q), BF16),
        compiler_params=_cparams(("arbitrary", "arbitrary")),
        name="mixb",
    )(qb, kvb, kvb)


def _merge_kernel(o1_ref, o2_ref, o3_ref, l1_ref, l2_ref, l3_ref, p4t_ref, p16t_ref, ob_ref, gl_ref, bg_ref,
                  x_ref, ex_ref, wpa_ref, wpb_ref, wo_ref, g2_ref, wr_ref, br_ref,
                  x1_ref, h2_ref, idx_ref, gw_ref):
    tm = x_ref.shape[0]
    nsub = tm // TILE
    lane = lax.broadcasted_iota(jnp.int32, (tm, LANES), 1)

    def split(w):
        hi = w.astype(BF16)
        return hi, (w - hi.astype(F32)).astype(BF16)

    def unperm(pt, ref):
        dil = ref.shape[1]
        per = TILE // dil
        outs = []
        for s in range(nsub):
            val = ref[0, :, s * per:(s + 1) * per, :].reshape(TILE, ref.shape[3])
            if val.dtype == BF16:
                outs.append(jnp.dot(pt, val, preferred_element_type=F32))
            else:
                hi, lo_ = split(val)
                outs.append(jnp.dot(pt, hi, preferred_element_type=F32) + jnp.dot(pt, lo_, preferred_element_type=F32))
        return jnp.concatenate(outs, axis=0)

    p4t, p16t = p4t_ref[...], p16t_ref[...]
    o1 = o1_ref[...].astype(F32)
    o2 = unperm(p4t, o2_ref)
    o3 = unperm(p16t, o3_ref)
    l1 = l1_ref[...]
    l2 = unperm(p4t, l2_ref)
    l3 = unperm(p16t, l3_ref)
    mx = jnp.maximum(jnp.maximum(l1, l2), l3)
    e1, e2, e3 = jnp.exp(l1 - mx), jnp.exp(l2 - mx), jnp.exp(l3 - mx)
    rden = 1.0 / (e1 + e2 + e3)
    packed = jnp.zeros((tm, LANES), F32)
    for g, e in enumerate((e1, e2, e3)):
        hi, lo_ = split(jnp.where(lane < HEADS_A, e * rden, 0.0))
        for part, v in enumerate((hi, lo_)):
            shift = (2 * g + part) * HEADS_A
            vf = v.astype(F32)
            packed = packed + (vf if shift == 0 else pltpu.roll(vf, shift, 1))
    wexp = jnp.dot(packed.astype(BF16), ex_ref[...], preferred_element_type=F32)
    oa = wexp[:, :OUT_A] * o1 + wexp[:, OUT_A:2 * OUT_A] * o2 + wexp[:, 2 * OUT_A:] * o3
    pa = jnp.dot(oa.astype(BF16), wpa_ref[...], preferred_element_type=F32)
    pb = jnp.dot(ob_ref[...], wpb_ref[...], preferred_element_type=F32)
    gl = gl_ref[...].astype(F32) + bg_ref[...]
    gates = 1.0 / (1.0 + jnp.exp(-gl))
    merged = gates[:, :D_MODEL] * pa + gates[:, D_MODEL:] * pb
    x1 = x_ref[...] + jnp.dot(merged.astype(BF16), wo_ref[...], preferred_element_type=F32)
    x1_ref[...] = x1
    ms = jnp.mean(x1 * x1, axis=-1, keepdims=True)
    h2 = x1 * lax.rsqrt(ms + EPS) * g2_ref[...]
    for c in range(D_MODEL // LANES):
        h2_ref[pl.ds(c, tm, stride=ROW_SUB), :] = h2[:, c * LANES:(c + 1) * LANES]
    work = jnp.dot(h2.astype(BF16), wr_ref[...], preferred_element_type=F32) + br_ref[...]
    lane_f = lane.astype(F32)
    vals, idxs = [], []
    for _ in range(TOP_K):
        m = jnp.max(work, axis=-1, keepdims=True)
        ix = jnp.min(jnp.where(work == m, lane_f, float(LANES)), axis=-1, keepdims=True)
        vals.append(m)
        idxs.append(ix)
        work = jnp.where(lane_f == ix, -jnp.inf, work)
    es = [jnp.exp(v - vals[0]) for v in vals]
    rsum = 1.0 / (es[0] + es[1] + es[2] + es[3])
    idx_out = jnp.zeros((tm, LANES), F32)
    gw_out = jnp.zeros((tm, LANES), F32)
    for k in range(TOP_K):
        idx_out = jnp.where(lane == k, idxs[k], idx_out)
        gw_out = jnp.where(lane == k, es[k] * rsum, gw_out)
    idx_ref[...] = idx_out.astype(jnp.int32)
    gw_ref[...] = gw_out


def _merge(o1, o2, o3, l1, l2, l3, p4t, p16t, ob, gl, bg, x2, ex, wpa, wpb, wo, g2, wr, br, seq):
    t = x2.shape[0]
    tm = MERGE_TM
    npos = seq // tm
    row = lambda i: (i, 0)
    fixed = lambda i: (0, 0)
    d4, d16 = DIL_PAIRS[1][1], DIL_PAIRS[2][1]
    sub = lambda i: (i // npos, 0, i % npos, 0)
    return pl.pallas_call(
        _merge_kernel,
        grid=(t // tm,),
        in_specs=[
            pl.BlockSpec((tm, OUT_A), row),
            pl.BlockSpec((1, d4, tm // d4, OUT_A), sub),
            pl.BlockSpec((1, d16, tm // d16, OUT_A), sub),
            pl.BlockSpec((tm, LANES), row),
            pl.BlockSpec((1, d4, tm // d4, LANES), sub),
            pl.BlockSpec((1, d16, tm // d16, LANES), sub),
            pl.BlockSpec((TILE, TILE), fixed),
            pl.BlockSpec((TILE, TILE), fixed),
            pl.BlockSpec((tm, WIDTH_B_Q), row),
            pl.BlockSpec((tm, 2 * D_MODEL), row),
            pl.BlockSpec((1, 2 * D_MODEL), fixed),
            pl.BlockSpec((tm, D_MODEL), row),
            pl.BlockSpec((LANES, N_GROUPS * OUT_A), fixed),
            pl.BlockSpec((OUT_A, D_MODEL), fixed),
            pl.BlockSpec((WIDTH_B_Q, D_MODEL), fixed),
            pl.BlockSpec((D_MODEL, D_MODEL), fixed),
            pl.BlockSpec((1, D_MODEL), fixed),
            pl.BlockSpec((D_MODEL, LANES), fixed),
            pl.BlockSpec((1, LANES), fixed),
        ],
        out_specs=[
            pl.BlockSpec((tm, D_MODEL), row), pl.BlockSpec((tm * ROW_SUB, LANES), row),
            pl.BlockSpec((tm, LANES), row), pl.BlockSpec((tm, LANES), row),
        ],
        out_shape=[
            jax.ShapeDtypeStruct((t, D_MODEL), F32),
            jax.ShapeDtypeStruct((t * ROW_SUB, LANES), F32),
            jax.ShapeDtypeStruct((t, LANES), jnp.int32),
            jax.ShapeDtypeStruct((t, LANES), F32),
        ],
        compiler_params=_cparams(("arbitrary",)),
        name="merge",
    )(o1, o2, o3, l1, l2, l3, p4t, p16t, ob, gl, bg, x2, ex, wpa, wpb, wo, g2, wr, br)


EXP_BM = 256


W_CHUNK = 256


RING = 3
N_LISTS = 5


def _experts_kernel(be_ref, nused_ref, idx_ref,
                    h2_hbm, wgu_ref, bgu_ref, wd_ref, bd_ref,
                    y_hbm, xbuf, ybuf, wgu_bf, wd_bf, gsem, ssem):
    i = pl.program_id(0)
    nused = nused_ref[0]
    n_real = y_hbm.shape[0] - RING * EXP_BM
    tok_list = lambda k, r: idx_ref[0, 0, k * EXP_BM + r]
    dst_cur = lambda r: idx_ref[0, 0, 3 * EXP_BM + r]
    dst_prev = lambda r: idx_ref[0, 0, 4 * EXP_BM + r]

    def gather(tok, s, r):
        return pltpu.make_async_copy(h2_hbm.at[tok], xbuf.at[s, pl.ds(r * ROW_SUB, ROW_SUB)], gsem.at[s])

    def scatter(s, dst, r):
        return pltpu.make_async_copy(ybuf.at[s, pl.ds(r * ROW_SUB, ROW_SUB)], y_hbm.at[dst], ssem.at[s])

    @pl.when(i == 0)
    def _():
        ybuf[...] = jnp.zeros_like(ybuf)
        for r in range(EXP_BM):
            gather(tok_list(0, r), 0, r).start()
            gather(tok_list(1, r), 1, r).start()
            scatter(0, n_real + r, r).start()
            scatter(1, n_real + EXP_BM + r, r).start()

    @pl.when(jnp.logical_and(i < nused, jnp.logical_or(i == 0, be_ref[i] != be_ref[jnp.maximum(i - 1, 0)])))
    def _():
        for c in range(2 * D_FF // W_CHUNK):
            cs = slice(c * W_CHUNK, (c + 1) * W_CHUNK)
            wgu_bf[:, cs] = wgu_ref[0, :, cs].astype(BF16)
        for c in range(D_MODEL // W_CHUNK):
            cs = slice(c * W_CHUNK, (c + 1) * W_CHUNK)
            wd_bf[:, cs] = wd_ref[0, :, cs].astype(BF16)

    def block(slot):
        prev = (slot - 1) % RING
        ahead = (slot + 2) % RING
        for r in range(EXP_BM):
            gather(0, slot, r).wait()
        xb = jnp.concatenate([xbuf[slot, pl.ds(c, EXP_BM, stride=ROW_SUB), :] for c in range(ROW_SUB)],
                             axis=1).astype(BF16)
        g = jnp.dot(xb, wgu_bf[:, :D_FF], preferred_element_type=F32) + bgu_ref[0, :, :D_FF]
        u = jnp.dot(xb, wgu_bf[:, D_FF:], preferred_element_type=F32) + bgu_ref[0, :, D_FF:]
        gate = jnp.minimum(g, SWIGLU_LIMIT)
        up = jnp.clip(u, -SWIGLU_LIMIT, SWIGLU_LIMIT)
        act = (up + 1.0) * (gate * (1.0 / (1.0 + jnp.exp(-SWIGLU_ALPHA * gate))))
        y = jnp.dot(act.astype(BF16), wd_bf[...], preferred_element_type=F32) + bd_ref[0]
        for r in range(EXP_BM):
            scatter(prev, dst_prev(r), r).start()
            gather(tok_list(2, r), ahead, r).start()

        for r in range(EXP_BM):
            scatter(slot, 0, r).wait()
        for c in range(ROW_SUB):
            ybuf[slot, pl.ds(c, EXP_BM, stride=ROW_SUB), :] = y[:, c * LANES:(c + 1) * LANES]

        @pl.when(i == nused - 1)
        def _():
            for r in range(EXP_BM):
                scatter(slot, dst_cur(r), r).start()
            for r in range(EXP_BM):
                for s in range(RING):
                    scatter(s, 0, r).wait()
                gather(0, (slot + 1) % RING, r).wait()
                gather(0, ahead, r).wait()

    for phase in range(RING):
        pl.when(jnp.logical_and(i < nused, i % RING == phase))(functools.partial(block, phase))


def _experts(block_e, nused, row_idx, h2, wgu, bgu, wd, bd, n_out_rows):
    n_blk = block_e.shape[0]
    d = D_MODEL
    grid_spec = pltpu.PrefetchScalarGridSpec(
        num_scalar_prefetch=2,
        grid=(n_blk,),
        in_specs=[
            pl.BlockSpec((1, 1, N_LISTS * EXP_BM), lambda i, be, nu: (i, 0, 0), memory_space=pltpu.SMEM),
            pl.BlockSpec(memory_space=pl.ANY),
            pl.BlockSpec((1, d, 2 * D_FF), lambda i, be, nu: (be[i], 0, 0)),
            pl.BlockSpec((1, 1, 2 * D_FF), lambda i, be, nu: (be[i], 0, 0)),
            pl.BlockSpec((1, D_FF, d), lambda i, be, nu: (be[i], 0, 0)),
            pl.BlockSpec((1, 1, d), lambda i, be, nu: (be[i], 0, 0)),
        ],
        out_specs=pl.BlockSpec(memory_space=pl.ANY),
        scratch_shapes=[
            pltpu.VMEM((RING, EXP_BM * ROW_SUB, LANES), F32),
            pltpu.VMEM((RING, EXP_BM * ROW_SUB, LANES), F32),
            pltpu.VMEM((d, 2 * D_FF), BF16),
            pltpu.VMEM((D_FF, d), BF16),
            pltpu.SemaphoreType.DMA((RING,)),
            pltpu.SemaphoreType.DMA((RING,)),
        ],
    )
    return pl.pallas_call(
        _experts_kernel,
        grid_spec=grid_spec,
        out_shape=jax.ShapeDtypeStruct((n_out_rows, ROW_SUB, LANES), F32),
        compiler_params=_cparams(("arbitrary",)),
        name="experts",
    )(block_e, nused, row_idx, h2, wgu, bgu, wd, bd)


def _combine_kernel(y0_ref, y1_ref, y2_ref, y3_ref, gw_ref, x1_ref, o_ref):
    gw = gw_ref[...]
    tm = gw.shape[0]
    for c in range(ROW_SUB):
        cs = slice(c * LANES, (c + 1) * LANES)
        acc = x1_ref[:, cs]
        for k, y_ref in enumerate((y0_ref, y1_ref, y2_ref, y3_ref)):
            acc = acc + gw[:, k:k + 1] * y_ref[pl.ds(c, tm, stride=ROW_SUB), :]
        o_ref[:, cs] = acc


def _combine(y, gw, x1):
    t = x1.shape[0]
    tm = COMBINE_TM
    nt = t // tm
    row = lambda i: (i, 0)
    yspec = lambda k: pl.BlockSpec((tm * ROW_SUB, LANES), lambda i: (k * nt + i, 0))
    return pl.pallas_call(
        _combine_kernel,
        grid=(nt,),
        in_specs=[yspec(0), yspec(1), yspec(2), yspec(3), pl.BlockSpec((tm, LANES), row),
                  pl.BlockSpec((tm, D_MODEL), row)],
        out_specs=pl.BlockSpec((tm, D_MODEL), row),
        out_shape=jax.ShapeDtypeStruct((t, D_MODEL), F32),
        compiler_params=_cparams(("arbitrary",)),
        name="combine",
    )(y, y, y, y, gw, x1)


def _routing_plan(idx):
    t = idx.shape[0]
    a = t * TOP_K
    tok_bits = (t - 1).bit_length()
    key = ((idx << (tok_bits + 2)) | (jnp.arange(t, dtype=jnp.int32)[:, None] << 2)
           | jnp.arange(TOP_K, dtype=jnp.int32)[None, :])
    skey = jnp.sort(key.reshape(-1))
    bounds = jnp.searchsorted(skey, jnp.arange(N_EXPERTS + 1, dtype=jnp.int32) << (tok_bits + 2),
                              method='compare_all').astype(jnp.int32)
    starts, counts = bounds[:-1], bounds[1:] - bounds[:-1]
    padded = ((counts + EXP_BM - 1) // EXP_BM) * EXP_BM
    pends = jnp.cumsum(padded)
    pstarts = pends - padded
    n_blk = a // EXP_BM + N_EXPERTS
    blk = jnp.arange(n_blk, dtype=jnp.int32)
    block_e = jnp.minimum(jnp.sum((pends[None, :] <= (blk * EXP_BM)[:, None]).astype(jnp.int32), axis=1),
                          N_EXPERTS - 1)
    nused = (pends[-1] // EXP_BM).astype(jnp.int32).reshape(1)
    sel = block_e[:, None] == jnp.arange(N_EXPERTS, dtype=jnp.int32)[None, :]
    pick = lambda tab: jnp.sum(jnp.where(sel, tab[None, :], 0), axis=1)
    first = blk * EXP_BM - pick(pstarts)
    q = jnp.arange(EXP_BM, dtype=jnp.int32)[None, :]
    valid = q < (pick(counts) - first)[:, None]
    kv = skey[jnp.clip((pick(starts) + first)[:, None] + q, 0, a - 1)]
    tok = (kv >> 2) & ((1 << tok_bits) - 1)
    tok_rows = jnp.where(valid, tok, 0)
    dump = a + (blk % RING)[:, None] * EXP_BM + q
    dst_rows = jnp.where(valid, (kv & (TOP_K - 1)) * t + tok, dump)
    tok_next = jnp.concatenate([tok_rows[1:], tok_rows[-1:]], axis=0)
    tok_next2 = jnp.concatenate([tok_rows[2:], tok_rows[-1:], tok_rows[-1:]], axis=0)
    dst_prev = jnp.concatenate([a + (RING - 1) * EXP_BM + q, dst_rows[:-1]], axis=0)
    row_idx = jnp.concatenate([tok_rows, tok_next, tok_next2, dst_rows, dst_prev], axis=1)
    return block_e, nused, row_idx.reshape(n_blk, 1, N_LISTS * EXP_BM)


def kernel(x, norm_mix_g, w_in, b_gate, qn_a, kn_a, qn_b, kn_b, w_proj_a, w_proj_b, w_out,
           norm_ffn_g, w_router, b_router, w_gate_up, b_gate_up, w_down, b_down):
    b, s, d = x.shape
    t = b * s
    x2 = x.reshape(t, d)
    tabs_a, tabs_b = _rope_tables(s)
    d4, d16 = DIL_PAIRS[1][1], DIL_PAIRS[2][1]

    gq = jnp.tile(qn_a[:, None, :], (1, HEADS_A, 1)).reshape(1, WIDTH_A) * SCALE
    gk = jnp.tile(kn_a[:, None, :], (1, HEADS_A, 1)).reshape(1, WIDTH_A)
    gqk = jnp.concatenate([gq, gk], axis=1)
    gqb = jnp.tile(qn_b, HEADS_B_Q).reshape(1, WIDTH_B_Q) * SCALE
    gkb = jnp.tile(kn_b, HEADS_B_KV).reshape(1, WIDTH_B_KV)
    hid = np.arange(MXU_N) // HEAD_DIM
    bd = jnp.asarray(hid[:, None] == hid[None, :], dtype=BF16)

    qkv1, qkv2, qkv3, qb, kvb, gl = _inproj(
        x2, norm_mix_g.reshape(1, d), w_in.astype(BF16), bd, _perm_matrix(d4), _perm_matrix(d16),
        gqk, gqb, gkb, tabs_a, tabs_b, b, s)

    o1, l1 = _mixa(qkv1.reshape(b, s, 3 * OUT_A), s)
    o2, l2 = _mixa(qkv2.reshape(b * d4, s // d4, 3 * OUT_A), s // d4)
    o3, l3 = _mixa(qkv3.reshape(b * d16, s // d16, 3 * OUT_A), s // d16)
    ob = _mixb(qb, kvb, s, MIXB_TQ)

    pk = np.arange(LANES)
    col = np.arange(N_GROUPS * OUT_A)
    ex = jnp.asarray((pk[:, None] < 2 * N_GROUPS * HEADS_A)
                     & ((pk // (2 * HEADS_A))[:, None] == (col // OUT_A)[None, :])
                     & ((pk % HEADS_A)[:, None] == ((col % OUT_A) // HEAD_DIM)[None, :]), dtype=BF16)
    wr = jnp.zeros((d, LANES), F32).at[:, :N_EXPERTS].set(w_router).astype(BF16)
    br = jnp.full((1, LANES), NEG, F32).at[0, :N_EXPERTS].set(b_router)
    x1, h2, idx_full, gw = _merge(
        o1.reshape(t, OUT_A), o2.reshape(b, d4, s // d4, OUT_A), o3.reshape(b, d16, s // d16, OUT_A),
        l1.reshape(t, LANES), l2.reshape(b, d4, s // d4, LANES), l3.reshape(b, d16, s // d16, LANES),
        _perm_matrix(d4, True), _perm_matrix(d16, True), ob, gl, b_gate.reshape(1, 2 * d), x2, ex,
        w_proj_a.astype(BF16), w_proj_b.astype(BF16), w_out.astype(BF16), norm_ffn_g.reshape(1, d), wr, br, s)

    block_e, nused, row_idx = _routing_plan(idx_full[:, :TOP_K])
    n_rows = t * TOP_K + RING * EXP_BM
    y = _experts(block_e, nused, row_idx, h2.reshape(t, ROW_SUB, LANES), w_gate_up,
                 b_gate_up.reshape(N_EXPERTS, 1, 2 * D_FF), w_down, b_down.reshape(N_EXPERTS, 1, d), n_rows)
    out = _combine(y.reshape(n_rows * ROW_SUB, LANES), gw, x1)
    return out.reshape(b, s, d)
```

```python
import functools

import jax
import jax.numpy as jnp
import numpy as np
from jax import lax
from jax.experimental import pallas as pl
from jax.experimental.pallas import tpu as pltpu

F32 = jnp.float32
BF16 = jnp.bfloat16

D_MODEL = 1024
HEAD_DIM = 64
SCALE = HEAD_DIM ** -0.5
EPS = 1e-6
NEG = -1e30
DIL_PAIRS = ((128, 1), (512, 4), (2048, 16))
HALF_WIN = 64
N_GROUPS = 3
HEADS_A = 8
OUT_A = HEADS_A * HEAD_DIM
WIDTH_A = N_GROUPS * OUT_A
ROT_DIM_A = 16
THETA_PARTIAL = 500000.0
HEADS_B_Q = 8
HEADS_B_KV = 2
WIDTH_B_Q = 512
WIDTH_B_KV = 128
GRID_W = 64
THETA_AXIAL = 10000.0
N_EXPERTS = 32
TOP_K = 4
D_FF = 1024
SWIGLU_LIMIT = 7.0
SWIGLU_ALPHA = 1.702

LANES = 128
MXU_N = 256
VMEM_LIMIT = 56 * 1024 * 1024
TILE = 256
INPROJ_TM = 512
GATE_CHUNK = 2048
MERGE_TM = 512
COMBINE_TM = 512
MIXB_TQ = 512
ROW_SUB = D_MODEL // LANES

COL_VA = 2 * WIDTH_A
COL_QB = 3 * WIDTH_A
COL_KB = COL_QB + WIDTH_B_Q
COL_GATE = COL_KB + 2 * WIDTH_B_KV


def _cparams(sem):
    return pltpu.CompilerParams(dimension_semantics=sem, vmem_limit_bytes=VMEM_LIMIT)


def _perm_rows(dil):
    n = np.arange(TILE)
    per = TILE // dil
    return (n % per) * dil + n // per


def _perm_matrix(dil, transpose=False):
    p = np.zeros((TILE, TILE), np.float32)
    p[np.arange(TILE), _perm_rows(dil)] = 1.0
    return jnp.asarray(p.T if transpose else p, dtype=BF16)


def _rope_tables(seq):
    pos = np.arange(seq, dtype=np.float64)[:, None]
    d = np.arange(LANES) % HEAD_DIM
    half = ROT_DIM_A // 2
    inv = THETA_PARTIAL ** (-((d % half) / half))
    ang = pos * inv[None, :]
    in_rot = (d < ROT_DIM_A)[None, :]
    first = (d < half)[None, :]
    second = ((d >= half) & (d < ROT_DIM_A))[None, :]
    ta = np.stack([np.where(in_rot, np.cos(ang), 1.0), np.where(first, -np.sin(ang), 0.0),
                   np.where(second, np.sin(ang), 0.0)])
    per_group = []
    for _, dil in DIL_PAIRS:
        order = (np.arange(seq // TILE) * TILE)[:, None] + _perm_rows(dil)[None, :]
        per_group.append(ta[:, order.reshape(-1), :])
    hb = HEAD_DIM // 4
    row = np.floor(pos / GRID_W)
    col = pos - row * GRID_W
    invb = THETA_AXIAL ** (-((d % hb) / hb))
    angb = np.where((d < HEAD_DIM // 2)[None, :], row, col) * invb[None, :]
    firstb = ((d % (2 * hb)) < hb)[None, :]
    tb = np.stack([np.cos(angb), np.where(firstb, -np.sin(angb), 0.0), np.where(firstb, 0.0, np.sin(angb))])
    return jnp.asarray(np.stack(per_group), F32), jnp.asarray(tb, F32)


def _inproj_kernel(x_ref, g_ref, w_ref, bd_ref, p4_ref, p16_ref, gqk_ref, gqb_ref, gkb_ref, ta_ref, tb_ref,
                   qkv1_ref, qkv2_ref, qkv3_ref, qb_ref, kvb_ref, gate_ref):
    x = x_ref[...]
    tm = x.shape[0]
    nsub = tm // TILE
    ms = jnp.mean(x * x, axis=-1, keepdims=True)
    h = (x * lax.rsqrt(ms + EPS) * g_ref[...]).astype(BF16)

    def regroup(p_ref):
        return jnp.concatenate(
            [jnp.dot(p_ref[...], h[s * TILE:(s + 1) * TILE], preferred_element_type=F32) for s in range(nsub)],
            axis=0).astype(BF16)

    hg = (h, regroup(p4_ref), regroup(p16_ref))
    bd = bd_ref[...]
    lane = lax.broadcasted_iota(jnp.int32, (tm, LANES), 1)
    lo = lane < HEAD_DIM

    def proj(lhs, c0, width=MXU_N):
        return jnp.dot(lhs, w_ref[:, c0:c0 + width], preferred_element_type=F32)

    def head_norm(y, gain, bdm):
        ss = jnp.dot((y * y).astype(BF16), bdm, preferred_element_type=F32)
        return y * lax.rsqrt(ss * (1.0 / HEAD_DIM) + EPS) * gain

    def rope(z, tab, sh):
        return z * tab[0] + pltpu.roll(z, LANES - sh, 1) * tab[1] + pltpu.roll(z, sh, 1) * tab[2]

    sh_a = ROT_DIM_A // 2
    sh_b = HEAD_DIM // 4

    def store_group(gi, c_out, val):
        if gi == 0:
            qkv1_ref[:, c_out:c_out + LANES] = val
        else:
            ref, dil = ((qkv2_ref, DIL_PAIRS[1][1]), (qkv3_ref, DIL_PAIRS[2][1]))[gi - 1]
            per = TILE // dil
            for s in range(nsub):
                ref[0, :, s * per:(s + 1) * per, c_out:c_out + LANES] = (
                    val[s * TILE:(s + 1) * TILE].reshape(dil, per, LANES))

    for gi in range(N_GROUPS):
        tab = (ta_ref[gi, 0], ta_ref[gi, 1], ta_ref[gi, 2])
        for which in range(3):
            wide = proj(hg[gi], which * WIDTH_A + gi * OUT_A, OUT_A)
            for cc in range(OUT_A // MXU_N):
                c_in = which * WIDTH_A + gi * OUT_A + cc * MXU_N
                y = wide[:, cc * MXU_N:(cc + 1) * MXU_N]
                if which < 2:
                    y = head_norm(y, gqk_ref[:, c_in:c_in + MXU_N], bd)
                for hf in range(2):
                    z = y[:, hf * LANES:(hf + 1) * LANES]
                    if which < 2:
                        z = rope(z, tab, sh_a)
                    store_group(gi, which * OUT_A + cc * MXU_N + hf * LANES, z.astype(BF16))

    tabb = (tb_ref[0], tb_ref[1], tb_ref[2])
    wide_b = proj(h, COL_QB, WIDTH_B_Q + 2 * WIDTH_B_KV)
    for c in range(WIDTH_B_Q // MXU_N):
        c0 = c * MXU_N
        yn = head_norm(wide_b[:, c0:c0 + MXU_N], gqb_ref[:, c0:c0 + MXU_N], bd)
        for hf in range(2):
            z = yn[:, hf * LANES:(hf + 1) * LANES]
            qb_ref[:, c0 + hf * LANES:c0 + (hf + 1) * LANES] = rope(z, tabb, sh_b).astype(BF16)
    ykv = wide_b[:, WIDTH_B_Q:]
    kb = rope(head_norm(ykv[:, :LANES], gkb_ref[...], bd[:LANES, :LANES]), tabb, sh_b)
    vb = ykv[:, LANES:]
    for j, t in enumerate((kb, vb)):
        sw = pltpu.roll(t, HEAD_DIM, 1)
        kvb_ref[:, (2 * j) * LANES:(2 * j + 1) * LANES] = jnp.where(lo, t, sw).astype(BF16)
        kvb_ref[:, (2 * j + 1) * LANES:(2 * j + 2) * LANES] = jnp.where(lo, sw, t).astype(BF16)
    for c in range(2 * D_MODEL // GATE_CHUNK):
        c0 = c * GATE_CHUNK
        gate_ref[:, c0:c0 + GATE_CHUNK] = proj(h, COL_GATE + c0, GATE_CHUNK).astype(BF16)


def _inproj(x2, g, w_bf, bd, p4, p16, gqk, gqb, gkb, tabs_a, tabs_b, batch, seq):
    t = x2.shape[0]
    n_cols = w_bf.shape[1]
    tm = INPROJ_TM
    npos = seq // tm
    row = lambda i: (i, 0)
    fixed = lambda i: (0, 0)
    d4, d16 = DIL_PAIRS[1][1], DIL_PAIRS[2][1]
    sub = lambda i: (i // npos, 0, i % npos, 0)
    return pl.pallas_call(
        _inproj_kernel,
        grid=(t // tm,),
        in_specs=[
            pl.BlockSpec((tm, D_MODEL), row),
            pl.BlockSpec((1, D_MODEL), fixed),
            pl.BlockSpec((D_MODEL, n_cols), fixed, pipeline_mode=pl.Buffered(1)),
            pl.BlockSpec((MXU_N, MXU_N), fixed),
            pl.BlockSpec((TILE, TILE), fixed),
            pl.BlockSpec((TILE, TILE), fixed),
            pl.BlockSpec((1, 2 * WIDTH_A), fixed),
            pl.BlockSpec((1, WIDTH_B_Q), fixed),
            pl.BlockSpec((1, WIDTH_B_KV), fixed),
            pl.BlockSpec((N_GROUPS, 3, tm, LANES), lambda i: (0, 0, i % npos, 0)),
            pl.BlockSpec((3, tm, LANES), lambda i: (0, i % npos, 0)),
        ],
        out_specs=[
            pl.BlockSpec((tm, 3 * OUT_A), row),
            pl.BlockSpec((1, d4, tm // d4, 3 * OUT_A), sub),
            pl.BlockSpec((1, d16, tm // d16, 3 * OUT_A), sub),
            pl.BlockSpec((tm, WIDTH_B_Q), row),
            pl.BlockSpec((tm, 4 * LANES), row),
            pl.BlockSpec((tm, 2 * D_MODEL), row),
        ],
        out_shape=[
            jax.ShapeDtypeStruct((t, 3 * OUT_A), BF16),
            jax.ShapeDtypeStruct((batch, d4, seq // d4, 3 * OUT_A), BF16),
            jax.ShapeDtypeStruct((batch, d16, seq // d16, 3 * OUT_A), BF16),
            jax.ShapeDtypeStruct((t, WIDTH_B_Q), BF16),
            jax.ShapeDtypeStruct((t, 4 * LANES), BF16),
            jax.ShapeDtypeStruct((t, 2 * D_MODEL), BF16),
        ],
        compiler_params=_cparams(("arbitrary",)),
        name="inproj",
    )(x2, g, w_bf, bd, p4, p16, gqk, gqb, gkb, tabs_a, tabs_b)


QBLK = 128
NSUB = 8


def _mixa_kernel(q_ref, k_ref, v_ref, o_ref, lse_ref, *, seq_len, win, seqs, nblk):
    step = pl.program_id(1)
    lane = lax.broadcasted_iota(jnp.int32, (QBLK, LANES), 1)
    lo = lane < HEAD_DIM
    qi = lax.broadcasted_iota(jnp.int32, (QBLK, win), 0)
    ki = lax.broadcasted_iota(jnp.int32, (QBLK, win), 1)
    for b in range(seqs * nblk):
        sq, r0 = b // nblk, (b % nblk) * QBLK
        if seq_len == win:
            blk, kstart = 0, 0
        else:
            blk = step * nblk + b % nblk
            kstart = pl.multiple_of(jnp.clip(blk * QBLK - HALF_WIN, 0, seq_len - win), HALF_WIN)
        valid = jnp.abs((ki + kstart) - (qi + blk * QBLK)) <= HALF_WIN
        scores, vals = [], []
        for p in range(HEADS_A // 2):
            cs = slice(p * LANES, (p + 1) * LANES)
            qp = q_ref[sq, r0:r0 + QBLK, cs]
            kp = k_ref[sq, pl.ds(kstart, win), cs]
            vals.append(v_ref[sq, pl.ds(kstart, win), cs])
            for hh in range(2):
                qh = jnp.where(lo if hh == 0 else jnp.logical_not(lo), qp, jnp.zeros_like(qp))
                scores.append(lax.dot_general(qh, kp, (((1,), (1,)), ((), ())), preferred_element_type=F32))
        s = jnp.where(valid[None], jnp.stack(scores), NEG)
        m = jnp.max(s, axis=-1, keepdims=True)
        e = jnp.exp(s - m)
        l = jnp.sum(e, axis=-1, keepdims=True)
        rl = 1.0 / l
        lse = m + jnp.log(l)
        eb = e.astype(BF16)
        lse_acc = jnp.zeros((QBLK, LANES), F32)
        for p in range(HEADS_A // 2):
            outs = [jnp.dot(eb[2 * p + hh], vals[p], preferred_element_type=F32) * rl[2 * p + hh] for hh in range(2)]
            o_ref[sq, r0:r0 + QBLK, p * LANES:(p + 1) * LANES] = jnp.where(lo, outs[0], outs[1]).astype(BF16)
            for hh in range(2):
                lse_acc = jnp.where(lane == 2 * p + hh, lse[2 * p + hh], lse_acc)
        lse_ref[sq, r0:r0 + QBLK, :] = lse_acc


def _mixa(qkv, seq_len):
    n_seq = qkv.shape[0]
    win = min(2 * QBLK, seq_len)
    nblk = min(NSUB, seq_len // QBLK)
    seqs = NSUB // nblk
    rows = nblk * QBLK
    grid = (n_seq // seqs, seq_len // rows)
    qspec = lambda c: pl.BlockSpec((seqs, rows, OUT_A), lambda s, i: (s, i, c))
    kspec = lambda c: pl.BlockSpec((seqs, seq_len, OUT_A), lambda s, i: (s, 0, c))
    ospec = pl.BlockSpec((seqs, rows, OUT_A), lambda s, i: (s, i, 0))
    lspec = pl.BlockSpec((seqs, rows, LANES), lambda s, i: (s, i, 0))
    return pl.pallas_call(
        functools.partial(_mixa_kernel, seq_len=seq_len, win=win, seqs=seqs, nblk=nblk),
        grid=grid,
        in_specs=[qspec(0), kspec(1), kspec(2)],
        out_specs=[ospec, lspec],
        out_shape=[
            jax.ShapeDtypeStruct((n_seq, seq_len, OUT_A), BF16),
            jax.ShapeDtypeStruct((n_seq, seq_len, LANES), F32),
        ],
        compiler_params=_cparams(("arbitrary", "arbitrary")),
        name=f"mixa_len{seq_len}",
    )(qkv, qkv, qkv)


def _mixb_kernel(q_ref, k_ref, v_ref, o_ref):
    tq = q_ref.shape[0]
    lane = lax.broadcasted_iota(jnp.int32, (tq, LANES), 1)
    lo = lane < HEAD_DIM
    for p in range(HEADS_B_Q // 2):
        j = p // 2
        cs = slice(p * LANES, (p + 1) * LANES)
        qp = q_ref[:, cs]
        kd = k_ref[:, j * LANES:(j + 1) * LANES]
        vd = v_ref[:, j * LANES:(j + 1) * LANES]
        outs = []
        for hh in range(2):
            qh = jnp.where(lo if hh == 0 else jnp.logical_not(lo), qp, jnp.zeros_like(qp))
            s = lax.dot_general(qh, kd, (((1,), (1,)), ((), ())), preferred_element_type=F32)
            m = jnp.max(s, axis=-1, keepdims=True)
            e = jnp.exp(s - m)
            l = jnp.sum(e, axis=-1, keepdims=True)
            outs.append(jnp.dot(e.astype(BF16), vd, preferred_element_type=F32) * (1.0 / l))
        o_ref[:, cs] = jnp.where(lo, outs[0], outs[1]).astype(BF16)


def _mixb(qb, kvb, seq, tq):
    t = qb.shape[0]
    nq = seq // tq
    return pl.pallas_call(
        _mixb_kernel,
        grid=(t // seq, nq),
        in_specs=[
            pl.BlockSpec((tq, WIDTH_B_Q), lambda b, i: (b * nq + i, 0)),
            pl.BlockSpec((seq, 2 * LANES), lambda b, i: (b, 0)),
            pl.BlockSpec((seq, 2 * LANES), lambda b, i: (b, 1)),
        ],
        out_specs=pl.BlockSpec((tq, WIDTH_B_Q), lambda b, i: (b * nq + i, 0)),
        out_shape=jax.ShapeDtypeStruct((t, WIDTH_B_Q), BF16),
        compiler_params=_cparams(("arbitrary", "arbitrary")),
        name="mixb",
    )(qb, kvb, kvb)


def _merge_kernel(o1_ref, o2_ref, o3_ref, l1_ref, l2_ref, l3_ref, p4t_ref, p16t_ref, ob_ref, gl_ref, bg_ref,
                  x_ref, ex_ref, wpa_ref, wpb_ref, wo_ref, g2_ref, wr_ref, br_ref,
                  x1_ref, h2_ref, idx_ref, gw_ref):
    tm = x_ref.shape[0]
    nsub = tm // TILE
    lane = lax.broadcasted_iota(jnp.int32, (tm, LANES), 1)

    def split(w):
        hi = w.astype(BF16)
        return hi, (w - hi.astype(F32)).astype(BF16)

    def unperm(pt, ref):
        dil = ref.shape[1]
        per = TILE // dil
        outs = []
        for s in range(nsub):
            val = ref[0, :, s * per:(s + 1) * per, :].reshape(TILE, ref.shape[3])
            if val.dtype == BF16:
                outs.append(jnp.dot(pt, val, preferred_element_type=F32))
            else:
                hi, lo_ = split(val)
                outs.append(jnp.dot(pt, hi, preferred_element_type=F32) + jnp.dot(pt, lo_, preferred_element_type=F32))
        return jnp.concatenate(outs, axis=0)

    p4t, p16t = p4t_ref[...], p16t_ref[...]
    o1 = o1_ref[...].astype(F32)
    o2 = unperm(p4t, o2_ref)
    o3 = unperm(p16t, o3_ref)
    l1 = l1_ref[...]
    l2 = unperm(p4t, l2_ref)
    l3 = unperm(p16t, l3_ref)
    mx = jnp.maximum(jnp.maximum(l1, l2), l3)
    e1, e2, e3 = jnp.exp(l1 - mx), jnp.exp(l2 - mx), jnp.exp(l3 - mx)
    rden = 1.0 / (e1 + e2 + e3)
    packed = jnp.zeros((tm, LANES), F32)
    for g, e in enumerate((e1, e2, e3)):
        hi, lo_ = split(jnp.where(lane < HEADS_A, e * rden, 0.0))
        for part, v in enumerate((hi, lo_)):
            shift = (2 * g + part) * HEADS_A
            vf = v.astype(F32)
            packed = packed + (vf if shift == 0 else pltpu.roll(vf, shift, 1))
    wexp = jnp.dot(packed.astype(BF16), ex_ref[...], preferred_element_type=F32)
    oa = wexp[:, :OUT_A] * o1 + wexp[:, OUT_A:2 * OUT_A] * o2 + wexp[:, 2 * OUT_A:] * o3
    pa = jnp.dot(oa.astype(BF16), wpa_ref[...], preferred_element_type=F32)
    pb = jnp.dot(ob_ref[...], wpb_ref[...], preferred_element_type=F32)
    gl = gl_ref[...].astype(F32) + bg_ref[...]
    gates = 1.0 / (1.0 + jnp.exp(-gl))
    merged = gates[:, :D_MODEL] * pa + gates[:, D_MODEL:] * pb
    x1 = x_ref[...] + jnp.dot(merged.astype(BF16), wo_ref[...], preferred_element_type=F32)
    x1_ref[...] = x1
    ms = jnp.mean(x1 * x1, axis=-1, keepdims=True)
    h2 = x1 * lax.rsqrt(ms + EPS) * g2_ref[...]
    for c in range(D_MODEL // LANES):
        h2_ref[pl.ds(c, tm, stride=ROW_SUB), :] = h2[:, c * LANES:(c + 1) * LANES]
    work = jnp.dot(h2.astype(BF16), wr_ref[...], preferred_element_type=F32) + br_ref[...]
    lane_f = lane.astype(F32)
    vals, idxs = [], []
    for _ in range(TOP_K):
        m = jnp.max(work, axis=-1, keepdims=True)
        ix = jnp.min(jnp.where(work == m, lane_f, float(LANES)), axis=-1, keepdims=True)
        vals.append(m)
        idxs.append(ix)
        work = jnp.where(lane_f == ix, -jnp.inf, work)
    es = [jnp.exp(v - vals[0]) for v in vals]
    rsum = 1.0 / (es[0] + es[1] + es[2] + es[3])
    idx_out = jnp.zeros((tm, LANES), F32)
    gw_out = jnp.zeros((tm, LANES), F32)
    for k in range(TOP_K):
        idx_out = jnp.where(lane == k, idxs[k], idx_out)
        gw_out = jnp.where(lane == k, es[k] * rsum, gw_out)
    idx_ref[...] = idx_out.astype(jnp.int32)
    gw_ref[...] = gw_out


def _merge(o1, o2, o3, l1, l2, l3, p4t, p16t, ob, gl, bg, x2, ex, wpa, wpb, wo, g2, wr, br, seq):
    t = x2.shape[0]
    tm = MERGE_TM
    npos = seq // tm
    row = lambda i: (i, 0)
    fixed = lambda i: (0, 0)
    d4, d16 = DIL_PAIRS[1][1], DIL_PAIRS[2][1]
    sub = lambda i: (i // npos, 0, i % npos, 0)
    return pl.pallas_call(
        _merge_kernel,
        grid=(t // tm,),
        in_specs=[
            pl.BlockSpec((tm, OUT_A), row),
            pl.BlockSpec((1, d4, tm // d4, OUT_A), sub),
            pl.BlockSpec((1, d16, tm // d16, OUT_A), sub),
            pl.BlockSpec((tm, LANES), row),
            pl.BlockSpec((1, d4, tm // d4, LANES), sub),
            pl.BlockSpec((1, d16, tm // d16, LANES), sub),
            pl.BlockSpec((TILE, TILE), fixed),
            pl.BlockSpec((TILE, TILE), fixed),
            pl.BlockSpec((tm, WIDTH_B_Q), row),
            pl.BlockSpec((tm, 2 * D_MODEL), row),
            pl.BlockSpec((1, 2 * D_MODEL), fixed),
            pl.BlockSpec((tm, D_MODEL), row),
            pl.BlockSpec((LANES, N_GROUPS * OUT_A), fixed),
            pl.BlockSpec((OUT_A, D_MODEL), fixed),
            pl.BlockSpec((WIDTH_B_Q, D_MODEL), fixed),
            pl.BlockSpec((D_MODEL, D_MODEL), fixed),
            pl.BlockSpec((1, D_MODEL), fixed),
            pl.BlockSpec((D_MODEL, LANES), fixed),
            pl.BlockSpec((1, LANES), fixed),
        ],
        out_specs=[
            pl.BlockSpec((tm, D_MODEL), row), pl.BlockSpec((tm * ROW_SUB, LANES), row),
            pl.BlockSpec((tm, LANES), row), pl.BlockSpec((tm, LANES), row),
        ],
        out_shape=[
            jax.ShapeDtypeStruct((t, D_MODEL), F32),
            jax.ShapeDtypeStruct((t * ROW_SUB, LANES), F32),
            jax.ShapeDtypeStruct((t, LANES), jnp.int32),
            jax.ShapeDtypeStruct((t, LANES), F32),
        ],
        compiler_params=_cparams(("arbitrary",)),
        name="merge",
    )(o1, o2, o3, l1, l2, l3, p4t, p16t, ob, gl, bg, x2, ex, wpa, wpb, wo, g2, wr, br)


EXP_BM = 256


W_CHUNK = 256


RING = 3
N_LISTS = 5


def _experts_kernel(be_ref, nxt_ref, nused_ref, idx_ref,
                    h2_hbm, wgu_hbm, bgu_ref, wd_hbm, bd_ref,
                    y_hbm, xbuf, ybuf, wgu_f32, wd_f32, wgu_bf, wd_bf, gsem, ssem, wsem):
    i = pl.program_id(0)
    nused = nused_ref[0]

    def fetch_weights(e):
        return (pltpu.make_async_copy(wgu_hbm.at[e], wgu_f32, wsem.at[0]),
                pltpu.make_async_copy(wd_hbm.at[e], wd_f32, wsem.at[1]))
    n_real = y_hbm.shape[0] - RING * EXP_BM
    tok_list = lambda k, r: idx_ref[0, 0, k * EXP_BM + r]
    dst_cur = lambda r: idx_ref[0, 0, 3 * EXP_BM + r]
    dst_prev = lambda r: idx_ref[0, 0, 4 * EXP_BM + r]

    def gather(tok, s, r):
        return pltpu.make_async_copy(h2_hbm.at[tok], xbuf.at[s, pl.ds(r * ROW_SUB, ROW_SUB)], gsem.at[s])

    def scatter(s, dst, r):
        return pltpu.make_async_copy(ybuf.at[s, pl.ds(r * ROW_SUB, ROW_SUB)], y_hbm.at[dst], ssem.at[s])

    @pl.when(i == 0)
    def _():
        ybuf[...] = jnp.zeros_like(ybuf)
        for r in range(EXP_BM):
            gather(tok_list(0, r), 0, r).start()
            gather(tok_list(1, r), 1, r).start()
            scatter(0, n_real + r, r).start()
            scatter(1, n_real + EXP_BM + r, r).start()
        for c in fetch_weights(be_ref[0]):
            c.start()

    @pl.when(jnp.logical_and(i < nused, jnp.logical_or(i == 0, be_ref[i] != be_ref[jnp.maximum(i - 1, 0)])))
    def _():
        for c in fetch_weights(0):
            c.wait()
        for c in range(2 * D_FF // W_CHUNK):
            cs = slice(c * W_CHUNK, (c + 1) * W_CHUNK)
            wgu_bf[:, cs] = wgu_f32[:, cs].astype(BF16)
        for c in range(D_MODEL // W_CHUNK):
            cs = slice(c * W_CHUNK, (c + 1) * W_CHUNK)
            wd_bf[:, cs] = wd_f32[:, cs].astype(BF16)
        for c in fetch_weights(nxt_ref[i]):
            c.start()

    def block(slot):
        prev = (slot - 1) % RING
        ahead = (slot + 2) % RING
        for r in range(EXP_BM):
            gather(0, slot, r).wait()
        xb = jnp.concatenate([xbuf[slot, pl.ds(c, EXP_BM, stride=ROW_SUB), :] for c in range(ROW_SUB)],
                             axis=1).astype(BF16)
        g = jnp.dot(xb, wgu_bf[:, :D_FF], preferred_element_type=F32) + bgu_ref[0, :, :D_FF]
        u = jnp.dot(xb, wgu_bf[:, D_FF:], preferred_element_type=F32) + bgu_ref[0, :, D_FF:]
        gate = jnp.minimum(g, SWIGLU_LIMIT)
        up = jnp.clip(u, -SWIGLU_LIMIT, SWIGLU_LIMIT)
        act = (up + 1.0) * (gate * (1.0 / (1.0 + jnp.exp(-SWIGLU_ALPHA * gate))))
        y = jnp.dot(act.astype(BF16), wd_bf[...], preferred_element_type=F32) + bd_ref[0]
        for r in range(EXP_BM):
            scatter(prev, dst_prev(r), r).start()
            gather(tok_list(2, r), ahead, r).start()

        for r in range(EXP_BM):
            scatter(slot, 0, r).wait()
        for c in range(ROW_SUB):
            ybuf[slot, pl.ds(c, EXP_BM, stride=ROW_SUB), :] = y[:, c * LANES:(c + 1) * LANES]

        @pl.when(i == nused - 1)
        def _():
            for r in range(EXP_BM):
                scatter(slot, dst_cur(r), r).start()
            for r in range(EXP_BM):
                for s in range(RING):
                    scatter(s, 0, r).wait()
                gather(0, (slot + 1) % RING, r).wait()
                gather(0, ahead, r).wait()
            for c in fetch_weights(0):
                c.wait()

    for phase in range(RING):
        pl.when(jnp.logical_and(i < nused, i % RING == phase))(functools.partial(block, phase))


def _experts(block_e, next_e, nused, row_idx, h2, wgu, bgu, wd, bd, n_out_rows):
    n_blk = block_e.shape[0]
    d = D_MODEL
    grid_spec = pltpu.PrefetchScalarGridSpec(
        num_scalar_prefetch=3,
        grid=(n_blk,),
        in_specs=[
            pl.BlockSpec((1, 1, N_LISTS * EXP_BM), lambda i, be, nx, nu: (i, 0, 0), memory_space=pltpu.SMEM),
            pl.BlockSpec(memory_space=pl.ANY),
            pl.BlockSpec(memory_space=pl.ANY),
            pl.BlockSpec((1, 1, 2 * D_FF), lambda i, be, nx, nu: (be[i], 0, 0)),
            pl.BlockSpec(memory_space=pl.ANY),
            pl.BlockSpec((1, 1, d), lambda i, be, nx, nu: (be[i], 0, 0)),
        ],
        out_specs=pl.BlockSpec(memory_space=pl.ANY),
        scratch_shapes=[
            pltpu.VMEM((RING, EXP_BM * ROW_SUB, LANES), F32),
            pltpu.VMEM((RING, EXP_BM * ROW_SUB, LANES), F32),
            pltpu.VMEM((d, 2 * D_FF), F32),
            pltpu.VMEM((D_FF, d), F32),
            pltpu.VMEM((d, 2 * D_FF), BF16),
            pltpu.VMEM((D_FF, d), BF16),
            pltpu.SemaphoreType.DMA((RING,)),
            pltpu.SemaphoreType.DMA((RING,)),
            pltpu.SemaphoreType.DMA((2,)),
        ],
    )
    return pl.pallas_call(
        _experts_kernel,
        grid_spec=grid_spec,
        out_shape=jax.ShapeDtypeStruct((n_out_rows, ROW_SUB, LANES), F32),
        compiler_params=_cparams(("arbitrary",)),
        name="experts",
    )(block_e, next_e, nused, row_idx, h2, wgu, bgu, wd, bd)


def _combine_kernel(y0_ref, y1_ref, y2_ref, y3_ref, gw_ref, x1_ref, o_ref):
    gw = gw_ref[...]
    tm = gw.shape[0]
    for c in range(ROW_SUB):
        cs = slice(c * LANES, (c + 1) * LANES)
        acc = x1_ref[:, cs]
        for k, y_ref in enumerate((y0_ref, y1_ref, y2_ref, y3_ref)):
            acc = acc + gw[:, k:k + 1] * y_ref[pl.ds(c, tm, stride=ROW_SUB), :]
        o_ref[:, cs] = acc


def _combine(y, gw, x1):
    t = x1.shape[0]
    tm = COMBINE_TM
    nt = t // tm
    row = lambda i: (i, 0)
    yspec = lambda k: pl.BlockSpec((tm * ROW_SUB, LANES), lambda i: (k * nt + i, 0))
    return pl.pallas_call(
        _combine_kernel,
        grid=(nt,),
        in_specs=[yspec(0), yspec(1), yspec(2), yspec(3), pl.BlockSpec((tm, LANES), row),
                  pl.BlockSpec((tm, D_MODEL), row)],
        out_specs=pl.BlockSpec((tm, D_MODEL), row),
        out_shape=jax.ShapeDtypeStruct((t, D_MODEL), F32),
        compiler_params=_cparams(("arbitrary",)),
        name="combine",
    )(y, y, y, y, gw, x1)


def _routing_plan(idx):
    t = idx.shape[0]
    a = t * TOP_K
    tok_bits = (t - 1).bit_length()
    key = ((idx << (tok_bits + 2)) | (jnp.arange(t, dtype=jnp.int32)[:, None] << 2)
           | jnp.arange(TOP_K, dtype=jnp.int32)[None, :])
    skey = jnp.sort(key.reshape(-1))
    bounds = jnp.searchsorted(skey, jnp.arange(N_EXPERTS + 1, dtype=jnp.int32) << (tok_bits + 2),
                              method='compare_all').astype(jnp.int32)
    starts, counts = bounds[:-1], bounds[1:] - bounds[:-1]
    padded = ((counts + EXP_BM - 1) // EXP_BM) * EXP_BM
    pends = jnp.cumsum(padded)
    pstarts = pends - padded
    n_blk = a // EXP_BM + N_EXPERTS
    blk = jnp.arange(n_blk, dtype=jnp.int32)
    block_e = jnp.minimum(jnp.sum((pends[None, :] <= (blk * EXP_BM)[:, None]).astype(jnp.int32), axis=1),
                          N_EXPERTS - 1)
    nused = (pends[-1] // EXP_BM).astype(jnp.int32).reshape(1)
    sel = block_e[:, None] == jnp.arange(N_EXPERTS, dtype=jnp.int32)[None, :]
    pick = lambda tab: jnp.sum(jnp.where(sel, tab[None, :], 0), axis=1)
    first = blk * EXP_BM - pick(pstarts)
    q = jnp.arange(EXP_BM, dtype=jnp.int32)[None, :]
    valid = q < (pick(counts) - first)[:, None]
    kv = skey[jnp.clip((pick(starts) + first)[:, None] + q, 0, a - 1)]
    tok = (kv >> 2) & ((1 << tok_bits) - 1)
    tok_rows = jnp.where(valid, tok, 0)
    dump = a + (blk % RING)[:, None] * EXP_BM + q
    dst_rows = jnp.where(valid, (kv & (TOP_K - 1)) * t + tok, dump)
    tok_next = jnp.concatenate([tok_rows[1:], tok_rows[-1:]], axis=0)
    tok_next2 = jnp.concatenate([tok_rows[2:], tok_rows[-1:], tok_rows[-1:]], axis=0)
    dst_prev = jnp.concatenate([a + (RING - 1) * EXP_BM + q, dst_rows[:-1]], axis=0)
    row_idx = jnp.concatenate([tok_rows, tok_next, tok_next2, dst_rows, dst_prev], axis=1)
    nxt = jnp.minimum(jnp.sum((block_e[None, :] <= block_e[:, None]).astype(jnp.int32), axis=1), n_blk - 1)
    next_e = jnp.sum(jnp.where(blk[None, :] == nxt[:, None], block_e[None, :], 0), axis=1)
    return block_e, next_e, nused, row_idx.reshape(n_blk, 1, N_LISTS * EXP_BM)


def kernel(x, norm_mix_g, w_in, b_gate, qn_a, kn_a, qn_b, kn_b, w_proj_a, w_proj_b, w_out,
           norm_ffn_g, w_router, b_router, w_gate_up, b_gate_up, w_down, b_down):
    b, s, d = x.shape
    t = b * s
    x2 = x.reshape(t, d)
    tabs_a, tabs_b = _rope_tables(s)
    d4, d16 = DIL_PAIRS[1][1], DIL_PAIRS[2][1]

    gq = jnp.tile(qn_a[:, None, :], (1, HEADS_A, 1)).reshape(1, WIDTH_A) * SCALE
    gk = jnp.tile(kn_a[:, None, :], (1, HEADS_A, 1)).reshape(1, WIDTH_A)
    gqk = jnp.concatenate([gq, gk], axis=1)
    gqb = jnp.tile(qn_b, HEADS_B_Q).reshape(1, WIDTH_B_Q) * SCALE
    gkb = jnp.tile(kn_b, HEADS_B_KV).reshape(1, WIDTH_B_KV)
    hid = np.arange(MXU_N) // HEAD_DIM
    bd = jnp.asarray(hid[:, None] == hid[None, :], dtype=BF16)

    qkv1, qkv2, qkv3, qb, kvb, gl = _inproj(
        x2, norm_mix_g.reshape(1, d), w_in.astype(BF16), bd, _perm_matrix(d4), _perm_matrix(d16),
        gqk, gqb, gkb, tabs_a, tabs_b, b, s)

    o1, l1 = _mixa(qkv1.reshape(b, s, 3 * OUT_A), s)
    o2, l2 = _mixa(qkv2.reshape(b * d4, s // d4, 3 * OUT_A), s // d4)
    o3, l3 = _mixa(qkv3.reshape(b * d16, s // d16, 3 * OUT_A), s // d16)
    ob = _mixb(qb, kvb, s, MIXB_TQ)

    pk = np.arange(LANES)
    col = np.arange(N_GROUPS * OUT_A)
    ex = jnp.asarray((pk[:, None] < 2 * N_GROUPS * HEADS_A)
                     & ((pk // (2 * HEADS_A))[:, None] == (col // OUT_A)[None, :])
                     & ((pk % HEADS_A)[:, None] == ((col % OUT_A) // HEAD_DIM)[None, :]), dtype=BF16)
    wr = jnp.zeros((d, LANES), F32).at[:, :N_EXPERTS].set(w_router).astype(BF16)
    br = jnp.full((1, LANES), NEG, F32).at[0, :N_EXPERTS].set(b_router)
    x1, h2, idx_full, gw = _merge(
        o1.reshape(t, OUT_A), o2.reshape(b, d4, s // d4, OUT_A), o3.reshape(b, d16, s // d16, OUT_A),
        l1.reshape(t, LANES), l2.reshape(b, d4, s // d4, LANES), l3.reshape(b, d16, s // d16, LANES),
        _perm_matrix(d4, True), _perm_matrix(d16, True), ob, gl, b_gate.reshape(1, 2 * d), x2, ex,
        w_proj_a.astype(BF16), w_proj_b.astype(BF16), w_out.astype(BF16), norm_ffn_g.reshape(1, d), wr, br, s)

    block_e, next_e, nused, row_idx = _routing_plan(idx_full[:, :TOP_K])
    n_rows = t * TOP_K + RING * EXP_BM
    y = _experts(block_e, next_e, nused, row_idx, h2.reshape(t, ROW_SUB, LANES), w_gate_up,
                 b_gate_up.reshape(N_EXPERTS, 1, 2 * D_FF), w_down, b_down.reshape(N_EXPERTS, 1, d), n_rows)
    out = _combine(y.reshape(n_rows * ROW_SUB, LANES), gw, x1)
    return out.reshape(b, s, d)
```

```python
import functools

import jax
import jax.numpy as jnp
import numpy as np
from jax import lax
from jax.experimental import pallas as pl
from jax.experimental.pallas import tpu as pltpu

F32 = jnp.float32
BF16 = jnp.bfloat16

D_MODEL = 1024
HEAD_DIM = 64
SCALE = HEAD_DIM ** -0.5
LOG2E = 1.4426950408889634
LN2 = 0.6931471805599453
EPS = 1e-6
NEG = -1e30
DIL_PAIRS = ((128, 1), (512, 4), (2048, 16))
HALF_WIN = 64
N_GROUPS = 3
HEADS_A = 8
OUT_A = HEADS_A * HEAD_DIM
WIDTH_A = N_GROUPS * OUT_A
ROT_DIM_A = 16
THETA_PARTIAL = 500000.0
HEADS_B_Q = 8
HEADS_B_KV = 2
WIDTH_B_Q = 512
WIDTH_B_KV = 128
GRID_W = 64
THETA_AXIAL = 10000.0
N_EXPERTS = 32
TOP_K = 4
D_FF = 1024
SWIGLU_LIMIT = 7.0
SWIGLU_ALPHA = 1.702

LANES = 128
MXU_N = 256
VMEM_LIMIT = 56 * 1024 * 1024
TILE = 256
INPROJ_TM = 512
GATE_CHUNK = 2048
MERGE_TM = 512
COMBINE_TM = 512
MIXB_TQ = 512
ROW_SUB = D_MODEL // LANES

COL_VA = 2 * WIDTH_A
COL_QB = 3 * WIDTH_A
COL_KB = COL_QB + WIDTH_B_Q
COL_GATE = COL_KB + 2 * WIDTH_B_KV


def _cparams(sem):
    return pltpu.CompilerParams(dimension_semantics=sem, vmem_limit_bytes=VMEM_LIMIT)


def _perm_rows(dil):
    n = np.arange(TILE)
    per = TILE // dil
    return (n % per) * dil + n // per


def _perm_matrix(dil, transpose=False):
    p = np.zeros((TILE, TILE), np.float32)
    p[np.arange(TILE), _perm_rows(dil)] = 1.0
    return jnp.asarray(p.T if transpose else p, dtype=BF16)


def _rope_tables(seq):
    pos = np.arange(seq, dtype=np.float64)[:, None]
    d = np.arange(LANES) % HEAD_DIM
    half = ROT_DIM_A // 2
    inv = THETA_PARTIAL ** (-((d % half) / half))
    ang = pos * inv[None, :]
    in_rot = (d < ROT_DIM_A)[None, :]
    first = (d < half)[None, :]
    second = ((d >= half) & (d < ROT_DIM_A))[None, :]
    ta = np.stack([np.where(in_rot, np.cos(ang), 1.0), np.where(first, -np.sin(ang), 0.0),
                   np.where(second, np.sin(ang), 0.0)])
    per_group = []
    for _, dil in DIL_PAIRS:
        order = (np.arange(seq // TILE) * TILE)[:, None] + _perm_rows(dil)[None, :]
        per_group.append(ta[:, order.reshape(-1), :])
    hb = HEAD_DIM // 4
    row = np.floor(pos / GRID_W)
    col = pos - row * GRID_W
    invb = THETA_AXIAL ** (-((d % hb) / hb))
    angb = np.where((d < HEAD_DIM // 2)[None, :], row, col) * invb[None, :]
    firstb = ((d % (2 * hb)) < hb)[None, :]
    tb = np.stack([np.cos(angb), np.where(firstb, -np.sin(angb), 0.0), np.where(firstb, 0.0, np.sin(angb))])
    return jnp.asarray(np.stack(per_group), F32), jnp.asarray(tb, F32)


def _inproj_kernel(x_ref, g_ref, w_ref, bd_ref, p4_ref, p16_ref, gqk_ref, gqb_ref, gkb_ref, ta_ref, tb_ref,
                   qkv1_ref, qkv2_ref, qkv3_ref, qb_ref, kvb_ref, gate_ref):
    x = x_ref[...]
    tm = x.shape[0]
    nsub = tm // TILE
    ms = jnp.mean(x * x, axis=-1, keepdims=True)
    h = (x * lax.rsqrt(ms + EPS) * g_ref[...]).astype(BF16)

    def regroup(p_ref):
        return jnp.concatenate(
            [jnp.dot(p_ref[...], h[s * TILE:(s + 1) * TILE], preferred_element_type=F32) for s in range(nsub)],
            axis=0).astype(BF16)

    hg = (h, regroup(p4_ref), regroup(p16_ref))
    bd = bd_ref[...]
    lane = lax.broadcasted_iota(jnp.int32, (tm, LANES), 1)
    lo = lane < HEAD_DIM

    def proj(lhs, c0, width=MXU_N):
        return jnp.dot(lhs, w_ref[:, c0:c0 + width], preferred_element_type=F32)

    def head_norm(y, gain, bdm):
        ss = jnp.dot((y * y).astype(BF16), bdm, preferred_element_type=F32)
        return y * lax.rsqrt(ss * (1.0 / HEAD_DIM) + EPS) * gain

    def rope(z, tab, sh):
        return z * tab[0] + pltpu.roll(z, LANES - sh, 1) * tab[1] + pltpu.roll(z, sh, 1) * tab[2]

    sh_a = ROT_DIM_A // 2
    sh_b = HEAD_DIM // 4

    def store_group(gi, c_out, val):
        if gi == 0:
            qkv1_ref[:, c_out:c_out + LANES] = val
        else:
            ref, dil = ((qkv2_ref, DIL_PAIRS[1][1]), (qkv3_ref, DIL_PAIRS[2][1]))[gi - 1]
            per = TILE // dil
            for s in range(nsub):
                ref[0, :, s * per:(s + 1) * per, c_out:c_out + LANES] = (
                    val[s * TILE:(s + 1) * TILE].reshape(dil, per, LANES))

    for gi in range(N_GROUPS):
        tab = (ta_ref[gi, 0], ta_ref[gi, 1], ta_ref[gi, 2])
        for which in range(3):
            wide = proj(hg[gi], which * WIDTH_A + gi * OUT_A, OUT_A)
            for cc in range(OUT_A // MXU_N):
                c_in = which * WIDTH_A + gi * OUT_A + cc * MXU_N
                y = wide[:, cc * MXU_N:(cc + 1) * MXU_N]
                if which < 2:
                    y = head_norm(y, gqk_ref[:, c_in:c_in + MXU_N], bd)
                for hf in range(2):
                    z = y[:, hf * LANES:(hf + 1) * LANES]
                    if which < 2:
                        z = rope(z, tab, sh_a)
                    store_group(gi, which * OUT_A + cc * MXU_N + hf * LANES, z.astype(BF16))

    tabb = (tb_ref[0], tb_ref[1], tb_ref[2])
    wide_b = proj(h, COL_QB, WIDTH_B_Q + 2 * WIDTH_B_KV)
    for c in range(WIDTH_B_Q // MXU_N):
        c0 = c * MXU_N
        yn = head_norm(wide_b[:, c0:c0 + MXU_N], gqb_ref[:, c0:c0 + MXU_N], bd)
        for hf in range(2):
            z = yn[:, hf * LANES:(hf + 1) * LANES]
            qb_ref[:, c0 + hf * LANES:c0 + (hf + 1) * LANES] = rope(z, tabb, sh_b).astype(BF16)
    ykv = wide_b[:, WIDTH_B_Q:]
    kb = rope(head_norm(ykv[:, :LANES], gkb_ref[...], bd[:LANES, :LANES]), tabb, sh_b)
    vb = ykv[:, LANES:]
    for j, t in enumerate((kb, vb)):
        sw = pltpu.roll(t, HEAD_DIM, 1)
        kvb_ref[:, (2 * j) * LANES:(2 * j + 1) * LANES] = jnp.where(lo, t, sw).astype(BF16)
        kvb_ref[:, (2 * j + 1) * LANES:(2 * j + 2) * LANES] = jnp.where(lo, sw, t).astype(BF16)
    for c in range(2 * D_MODEL // GATE_CHUNK):
        c0 = c * GATE_CHUNK
        gate_ref[:, c0:c0 + GATE_CHUNK] = proj(h, COL_GATE + c0, GATE_CHUNK).astype(BF16)


def _inproj(x2, g, w_bf, bd, p4, p16, gqk, gqb, gkb, tabs_a, tabs_b, batch, seq):
    t = x2.shape[0]
    n_cols = w_bf.shape[1]
    tm = INPROJ_TM
    npos = seq // tm
    row = lambda i: (i, 0)
    fixed = lambda i: (0, 0)
    d4, d16 = DIL_PAIRS[1][1], DIL_PAIRS[2][1]
    sub = lambda i: (i // npos, 0, i % npos, 0)
    return pl.pallas_call(
        _inproj_kernel,
        grid=(t // tm,),
        in_specs=[
            pl.BlockSpec((tm, D_MODEL), row),
            pl.BlockSpec((1, D_MODEL), fixed),
            pl.BlockSpec((D_MODEL, n_cols), fixed, pipeline_mode=pl.Buffered(1)),
            pl.BlockSpec((MXU_N, MXU_N), fixed),
            pl.BlockSpec((TILE, TILE), fixed),
            pl.BlockSpec((TILE, TILE), fixed),
            pl.BlockSpec((1, 2 * WIDTH_A), fixed),
            pl.BlockSpec((1, WIDTH_B_Q), fixed),
            pl.BlockSpec((1, WIDTH_B_KV), fixed),
            pl.BlockSpec((N_GROUPS, 3, tm, LANES), lambda i: (0, 0, i % npos, 0)),
            pl.BlockSpec((3, tm, LANES), lambda i: (0, i % npos, 0)),
        ],
        out_specs=[
            pl.BlockSpec((tm, 3 * OUT_A), row),
            pl.BlockSpec((1, d4, tm // d4, 3 * OUT_A), sub),
            pl.BlockSpec((1, d16, tm // d16, 3 * OUT_A), sub),
            pl.BlockSpec((tm, WIDTH_B_Q), row),
            pl.BlockSpec((tm, 4 * LANES), row),
            pl.BlockSpec((tm, 2 * D_MODEL), row),
        ],
        out_shape=[
            jax.ShapeDtypeStruct((t, 3 * OUT_A), BF16),
            jax.ShapeDtypeStruct((batch, d4, seq // d4, 3 * OUT_A), BF16),
            jax.ShapeDtypeStruct((batch, d16, seq // d16, 3 * OUT_A), BF16),
            jax.ShapeDtypeStruct((t, WIDTH_B_Q), BF16),
            jax.ShapeDtypeStruct((t, 4 * LANES), BF16),
            jax.ShapeDtypeStruct((t, 2 * D_MODEL), BF16),
        ],
        compiler_params=_cparams(("arbitrary",)),
        name="inproj",
    )(x2, g, w_bf, bd, p4, p16, gqk, gqb, gkb, tabs_a, tabs_b)


QBLK = 128
NSUB = 8


def _mixa_kernel(q_ref, k_ref, v_ref, o_ref, lse_ref, *, seq_len, win, seqs, nblk):
    step = pl.program_id(1)
    lane = lax.broadcasted_iota(jnp.int32, (QBLK, LANES), 1)
    lo = lane < HEAD_DIM
    qi = lax.broadcasted_iota(jnp.int32, (QBLK, win), 0)
    ki = lax.broadcasted_iota(jnp.int32, (QBLK, win), 1)
    for b in range(seqs * nblk):
        sq, r0 = b // nblk, (b % nblk) * QBLK
        if seq_len == win:
            blk, kstart = 0, 0
        else:
            blk = step * nblk + b % nblk
            kstart = pl.multiple_of(jnp.clip(blk * QBLK - HALF_WIN, 0, seq_len - win), HALF_WIN)
        valid = jnp.abs((ki + kstart) - (qi + blk * QBLK)) <= HALF_WIN
        scores, vals = [], []
        for p in range(HEADS_A // 2):
            cs = slice(p * LANES, (p + 1) * LANES)
            qp = q_ref[sq, r0:r0 + QBLK, cs]
            kp = k_ref[sq, pl.ds(kstart, win), cs]
            vals.append(v_ref[sq, pl.ds(kstart, win), cs])
            for hh in range(2):
                qh = jnp.where(lo if hh == 0 else jnp.logical_not(lo), qp, jnp.zeros_like(qp))
                scores.append(lax.dot_general(qh, kp, (((1,), (1,)), ((), ())), preferred_element_type=F32))
        s = jnp.where(valid[None], jnp.stack(scores), NEG)
        m = jnp.max(s, axis=-1, keepdims=True)
        e = jnp.exp2(s - m)
        l = jnp.sum(e, axis=-1, keepdims=True)
        rl = 1.0 / l
        lse = m * LN2 + jnp.log(l)
        eb = e.astype(BF16)
        lse_acc = jnp.zeros((QBLK, LANES), F32)
        for p in range(HEADS_A // 2):
            outs = [jnp.dot(eb[2 * p + hh], vals[p], preferred_element_type=F32) * rl[2 * p + hh] for hh in range(2)]
            o_ref[sq, r0:r0 + QBLK, p * LANES:(p + 1) * LANES] = jnp.where(lo, outs[0], outs[1]).astype(BF16)
            for hh in range(2):
                lse_acc = jnp.where(lane == 2 * p + hh, lse[2 * p + hh], lse_acc)
        lse_ref[sq, r0:r0 + QBLK, :] = lse_acc


def _mixa(qkv, seq_len):
    n_seq = qkv.shape[0]
    win = min(2 * QBLK, seq_len)
    nblk = min(NSUB, seq_len // QBLK)
    seqs = NSUB // nblk
    rows = nblk * QBLK
    grid = (n_seq // seqs, seq_len // rows)
    qspec = lambda c: pl.BlockSpec((seqs, rows, OUT_A), lambda s, i: (s, i, c))
    kspec = lambda c: pl.BlockSpec((seqs, seq_len, OUT_A), lambda s, i: (s, 0, c))
    ospec = pl.BlockSpec((seqs, rows, OUT_A), lambda s, i: (s, i, 0))
    lspec = pl.BlockSpec((seqs, rows, LANES), lambda s, i: (s, i, 0))
    return pl.pallas_call(
        functools.partial(_mixa_kernel, seq_len=seq_len, win=win, seqs=seqs, nblk=nblk),
        grid=grid,
        in_specs=[qspec(0), kspec(1), kspec(2)],
        out_specs=[ospec, lspec],
        out_shape=[
            jax.ShapeDtypeStruct((n_seq, seq_len, OUT_A), BF16),
            jax.ShapeDtypeStruct((n_seq, seq_len, LANES), F32),
        ],
        compiler_params=_cparams(("arbitrary", "arbitrary")),
        name=f"mixa_len{seq_len}",
    )(qkv, qkv, qkv)


def _mixb_kernel(q_ref, k_ref, v_ref, o_ref):
    tq = q_ref.shape[0]
    lane = lax.broadcasted_iota(jnp.int32, (tq, LANES), 1)
    lo = lane < HEAD_DIM
    for p in range(HEADS_B_Q // 2):
        j = p // 2
        cs = slice(p * LANES, (p + 1) * LANES)
        qp = q_ref[:, cs]
        kd = k_ref[:, j * LANES:(j + 1) * LANES]
        vd = v_ref[:, j * LANES:(j + 1) * LANES]
        outs = []
        for hh in range(2):
            qh = jnp.where(lo if hh == 0 else jnp.logical_not(lo), qp, jnp.zeros_like(qp))
            s = lax.dot_general(qh, kd, (((1,), (1,)), ((), ())), preferred_element_type=F32)
            m = jnp.max(s, axis=-1, keepdims=True)
            e = jnp.exp2(s - m)
            l = jnp.sum(e, axis=-1, keepdims=True)
            outs.append(jnp.dot(e.astype(BF16), vd, preferred_element_type=F32) * (1.0 / l))
        o_ref[:, cs] = jnp.where(lo, outs[0], outs[1]).astype(BF16)


def _mixb(qb, kvb, seq, tq):
    t = qb.shape[0]
    nq = seq // tq
    return pl.pallas_call(
        _mixb_kernel,
        grid=(t // seq, nq),
        in_specs=[
            pl.BlockSpec((tq, WIDTH_B_Q), lambda b, i: (b * nq + i, 0)),
            pl.BlockSpec((seq, 2 * LANES), lambda b, i: (b, 0)),
            pl.BlockSpec((seq, 2 * LANES), lambda b, i: (b, 1)),
        ],
        out_specs=pl.BlockSpec((tq, WIDTH_B_Q), lambda b, i: (b * nq + i, 0)),
        out_shape=jax.ShapeDtypeStruct((t, WIDTH_B_Q), BF16),
        compiler_params=_cparams(("arbitrary", "arbitrary")),
        name="mixb",
    )(qb, kvb, kvb)


def _merge_kernel(o1_ref, o2_ref, o3_ref, l1_ref, l2_ref, l3_ref, p4t_ref, p16t_ref, ob_ref, gl_ref, bg_ref,
                  x_ref, ex_ref, wpa_ref, wpb_ref, wo_ref, g2_ref, wr_ref, br_ref,
                  x1_ref, h2_ref, idx_ref, gw_ref):
    tm = x_ref.shape[0]
    nsub = tm // TILE
    lane = lax.broadcasted_iota(jnp.int32, (tm, LANES), 1)

    def split(w):
        hi = w.astype(BF16)
        return hi, (w - hi.astype(F32)).astype(BF16)

    def unperm(pt, ref):
        dil = ref.shape[1]
        per = TILE // dil
        outs = []
        for s in range(nsub):
            val = ref[0, :, s * per:(s + 1) * per, :].reshape(TILE, ref.shape[3])
            if val.dtype == BF16:
                outs.append(jnp.dot(pt, val, preferred_element_type=F32))
            else:
                hi, lo_ = split(val)
                outs.append(jnp.dot(pt, hi, preferred_element_type=F32) + jnp.dot(pt, lo_, preferred_element_type=F32))
        return jnp.concatenate(outs, axis=0)

    p4t, p16t = p4t_ref[...], p16t_ref[...]
    o1 = o1_ref[...].astype(F32)
    o2 = unperm(p4t, o2_ref)
    o3 = unperm(p16t, o3_ref)
    l1 = l1_ref[...]
    l2 = unperm(p4t, l2_ref)
    l3 = unperm(p16t, l3_ref)
    mx = jnp.maximum(jnp.maximum(l1, l2), l3)
    e1, e2, e3 = jnp.exp(l1 - mx), jnp.exp(l2 - mx), jnp.exp(l3 - mx)
    rden = 1.0 / (e1 + e2 + e3)
    packed = jnp.zeros((tm, LANES), F32)
    for g, e in enumerate((e1, e2, e3)):
        hi, lo_ = split(jnp.where(lane < HEADS_A, e * rden, 0.0))
        for part, v in enumerate((hi, lo_)):
            shift = (2 * g + part) * HEADS_A
            vf = v.astype(F32)
            packed = packed + (vf if shift == 0 else pltpu.roll(vf, shift, 1))
    wexp = jnp.dot(packed.astype(BF16), ex_ref[...], preferred_element_type=F32)
    oa = wexp[:, :OUT_A] * o1 + wexp[:, OUT_A:2 * OUT_A] * o2 + wexp[:, 2 * OUT_A:] * o3
    pa = jnp.dot(oa.astype(BF16), wpa_ref[...], preferred_element_type=F32)
    pb = jnp.dot(ob_ref[...], wpb_ref[...], preferred_element_type=F32)
    gl = gl_ref[...].astype(F32) + bg_ref[...]
    gates = 1.0 / (1.0 + jnp.exp(-gl))
    merged = gates[:, :D_MODEL] * pa + gates[:, D_MODEL:] * pb
    x1 = x_ref[...] + jnp.dot(merged.astype(BF16), wo_ref[...], preferred_element_type=F32)
    x1_ref[...] = x1
    ms = jnp.mean(x1 * x1, axis=-1, keepdims=True)
    h2 = x1 * lax.rsqrt(ms + EPS) * g2_ref[...]
    for c in range(D_MODEL // LANES):
        h2_ref[pl.ds(c, tm, stride=ROW_SUB), :] = h2[:, c * LANES:(c + 1) * LANES]
    work = jnp.dot(h2.astype(BF16), wr_ref[...], preferred_element_type=F32) + br_ref[...]
    lane_f = lane.astype(F32)
    vals, idxs = [], []
    for _ in range(TOP_K):
        m = jnp.max(work, axis=-1, keepdims=True)
        ix = jnp.min(jnp.where(work == m, lane_f, float(LANES)), axis=-1, keepdims=True)
        vals.append(m)
        idxs.append(ix)
        work = jnp.where(lane_f == ix, -jnp.inf, work)
    es = [jnp.exp(v - vals[0]) for v in vals]
    rsum = 1.0 / (es[0] + es[1] + es[2] + es[3])
    idx_out = jnp.zeros((tm, LANES), F32)
    gw_out = jnp.zeros((tm, LANES), F32)
    for k in range(TOP_K):
        idx_out = jnp.where(lane == k, idxs[k], idx_out)
        gw_out = jnp.where(lane == k, es[k] * rsum, gw_out)
    idx_ref[...] = idx_out.astype(jnp.int32)
    gw_ref[...] = gw_out


def _merge(o1, o2, o3, l1, l2, l3, p4t, p16t, ob, gl, bg, x2, ex, wpa, wpb, wo, g2, wr, br, seq):
    t = x2.shape[0]
    tm = MERGE_TM
    npos = seq // tm
    row = lambda i: (i, 0)
    fixed = lambda i: (0, 0)
    d4, d16 = DIL_PAIRS[1][1], DIL_PAIRS[2][1]
    sub = lambda i: (i // npos, 0, i % npos, 0)
    return pl.pallas_call(
        _merge_kernel,
        grid=(t // tm,),
        in_specs=[
            pl.BlockSpec((tm, OUT_A), row),
            pl.BlockSpec((1, d4, tm // d4, OUT_A), sub),
            pl.BlockSpec((1, d16, tm // d16, OUT_A), sub),
            pl.BlockSpec((tm, LANES), row),
            pl.BlockSpec((1, d4, tm // d4, LANES), sub),
            pl.BlockSpec((1, d16, tm // d16, LANES), sub),
            pl.BlockSpec((TILE, TILE), fixed),
            pl.BlockSpec((TILE, TILE), fixed),
            pl.BlockSpec((tm, WIDTH_B_Q), row),
            pl.BlockSpec((tm, 2 * D_MODEL), row),
            pl.BlockSpec((1, 2 * D_MODEL), fixed),
            pl.BlockSpec((tm, D_MODEL), row),
            pl.BlockSpec((LANES, N_GROUPS * OUT_A), fixed),
            pl.BlockSpec((OUT_A, D_MODEL), fixed),
            pl.BlockSpec((WIDTH_B_Q, D_MODEL), fixed),
            pl.BlockSpec((D_MODEL, D_MODEL), fixed),
            pl.BlockSpec((1, D_MODEL), fixed),
            pl.BlockSpec((D_MODEL, LANES), fixed),
            pl.BlockSpec((1, LANES), fixed),
        ],
        out_specs=[
            pl.BlockSpec((tm, D_MODEL), row), pl.BlockSpec((tm * ROW_SUB, LANES), row),
            pl.BlockSpec((tm, LANES), row), pl.BlockSpec((tm, LANES), row),
        ],
        out_shape=[
            jax.ShapeDtypeStruct((t, D_MODEL), F32),
            jax.ShapeDtypeStruct((t * ROW_SUB, LANES), F32),
            jax.ShapeDtypeStruct((t, LANES), jnp.int32),
            jax.ShapeDtypeStruct((t, LANES), F32),
        ],
        compiler_params=_cparams(("arbitrary",)),
        name="merge",
    )(o1, o2, o3, l1, l2, l3, p4t, p16t, ob, gl, bg, x2, ex, wpa, wpb, wo, g2, wr, br)


EXP_BM = 256


W_CHUNK = 256


RING = 3
N_LISTS = 5


def _experts_kernel(be_ref, nxt_ref, nused_ref, idx_ref,
                    h2_hbm, wgu_hbm, bgu_ref, wd_hbm, bd_ref,
                    y_hbm, xbuf, ybuf, wgu_f32, wd_f32, wgu_bf, wd_bf, gsem, ssem, wsem):
    i = pl.program_id(0)
    nused = nused_ref[0]

    def fetch_weights(e):
        return (pltpu.make_async_copy(wgu_hbm.at[e], wgu_f32, wsem.at[0]),
                pltpu.make_async_copy(wd_hbm.at[e], wd_f32, wsem.at[1]))
    n_real = y_hbm.shape[0] - RING * EXP_BM
    tok_list = lambda k, r: idx_ref[0, 0, k * EXP_BM + r]
    dst_cur = lambda r: idx_ref[0, 0, 3 * EXP_BM + r]
    dst_prev = lambda r: idx_ref[0, 0, 4 * EXP_BM + r]

    def gather(tok, s, r):
        return pltpu.make_async_copy(h2_hbm.at[tok], xbuf.at[s, pl.ds(r * ROW_SUB, ROW_SUB)], gsem.at[s])

    def scatter(s, dst, r):
        return pltpu.make_async_copy(ybuf.at[s, pl.ds(r * ROW_SUB, ROW_SUB)], y_hbm.at[dst], ssem.at[s])

    @pl.when(i == 0)
    def _():
        ybuf[...] = jnp.zeros_like(ybuf)
        for r in range(EXP_BM):
            gather(tok_list(0, r), 0, r).start()
            gather(tok_list(1, r), 1, r).start()
            scatter(0, n_real + r, r).start()
            scatter(1, n_real + EXP_BM + r, r).start()
        for c in fetch_weights(be_ref[0]):
            c.start()

    @pl.when(jnp.logical_and(i < nused, jnp.logical_or(i == 0, be_ref[i] != be_ref[jnp.maximum(i - 1, 0)])))
    def _():
        for c in fetch_weights(0):
            c.wait()
        for c in range(2 * D_FF // W_CHUNK):
            cs = slice(c * W_CHUNK, (c + 1) * W_CHUNK)
            wgu_bf[:, cs] = wgu_f32[:, cs].astype(BF16)
        for c in range(D_MODEL // W_CHUNK):
            cs = slice(c * W_CHUNK, (c + 1) * W_CHUNK)
            wd_bf[:, cs] = wd_f32[:, cs].astype(BF16)
        for c in fetch_weights(nxt_ref[i]):
            c.start()

    def block(slot):
        prev = (slot - 1) % RING
        ahead = (slot + 2) % RING
        for r in range(EXP_BM):
            gather(0, slot, r).wait()
        xb = jnp.concatenate([xbuf[slot, pl.ds(c, EXP_BM, stride=ROW_SUB), :] for c in range(ROW_SUB)],
                             axis=1).astype(BF16)
        e_row = pl.ds(be_ref[i], 1)
        bgu = bgu_ref[e_row, :]
        g = jnp.dot(xb, wgu_bf[:, :D_FF], preferred_element_type=F32) + bgu[:, :D_FF]
        u = jnp.dot(xb, wgu_bf[:, D_FF:], preferred_element_type=F32) + bgu[:, D_FF:]
        gate = jnp.minimum(g, SWIGLU_LIMIT)
        up = jnp.clip(u, -SWIGLU_LIMIT, SWIGLU_LIMIT)
        act = (up + 1.0) * (gate * (1.0 / (1.0 + jnp.exp(-SWIGLU_ALPHA * gate))))
        y = jnp.dot(act.astype(BF16), wd_bf[...], preferred_element_type=F32) + bd_ref[e_row, :]
        for r in range(EXP_BM):
            scatter(prev, dst_prev(r), r).start()
            gather(tok_list(2, r), ahead, r).start()

        for r in range(EXP_BM):
            scatter(slot, 0, r).wait()
        for c in range(ROW_SUB):
            ybuf[slot, pl.ds(c, EXP_BM, stride=ROW_SUB), :] = y[:, c * LANES:(c + 1) * LANES]

        @pl.when(i == nused - 1)
        def _():
            for r in range(EXP_BM):
                scatter(slot, dst_cur(r), r).start()
            for r in range(EXP_BM):
                for s in range(RING):
                    scatter(s, 0, r).wait()
                gather(0, (slot + 1) % RING, r).wait()
                gather(0, ahead, r).wait()
            for c in fetch_weights(0):
                c.wait()

    for phase in range(RING):
        pl.when(jnp.logical_and(i < nused, i % RING == phase))(functools.partial(block, phase))


def _experts(block_e, next_e, nused, row_idx, h2, wgu, bgu, wd, bd, n_out_rows):
    n_blk = block_e.shape[0]
    d = D_MODEL
    grid_spec = pltpu.PrefetchScalarGridSpec(
        num_scalar_prefetch=3,
        grid=(n_blk,),
        in_specs=[
            pl.BlockSpec((1, 1, N_LISTS * EXP_BM), lambda i, be, nx, nu: (i, 0, 0), memory_space=pltpu.SMEM),
            pl.BlockSpec(memory_space=pl.ANY),
            pl.BlockSpec(memory_space=pl.ANY),
            pl.BlockSpec((N_EXPERTS, 2 * D_FF), lambda i, be, nx, nu: (0, 0)),
            pl.BlockSpec(memory_space=pl.ANY),
            pl.BlockSpec((N_EXPERTS, d), lambda i, be, nx, nu: (0, 0)),
        ],
        out_specs=pl.BlockSpec(memory_space=pl.ANY),
        scratch_shapes=[
            pltpu.VMEM((RING, EXP_BM * ROW_SUB, LANES), F32),
            pltpu.VMEM((RING, EXP_BM * ROW_SUB, LANES), F32),
            pltpu.VMEM((d, 2 * D_FF), F32),
            pltpu.VMEM((D_FF, d), F32),
            pltpu.VMEM((d, 2 * D_FF), BF16),
            pltpu.VMEM((D_FF, d), BF16),
            pltpu.SemaphoreType.DMA((RING,)),
            pltpu.SemaphoreType.DMA((RING,)),
            pltpu.SemaphoreType.DMA((2,)),
        ],
    )
    return pl.pallas_call(
        _experts_kernel,
        grid_spec=grid_spec,
        out_shape=jax.ShapeDtypeStruct((n_out_rows, ROW_SUB, LANES), F32),
        compiler_params=_cparams(("arbitrary",)),
        name="experts",
    )(block_e, next_e, nused, row_idx, h2, wgu, bgu, wd, bd)


def _combine_kernel(y0_ref, y1_ref, y2_ref, y3_ref, gw_ref, x1_ref, o_ref):
    gw = gw_ref[...]
    tm = gw.shape[0]
    for c in range(ROW_SUB):
        cs = slice(c * LANES, (c + 1) * LANES)
        acc = x1_ref[:, cs]
        for k, y_ref in enumerate((y0_ref, y1_ref, y2_ref, y3_ref)):
            acc = acc + gw[:, k:k + 1] * y_ref[pl.ds(c, tm, stride=ROW_SUB), :]
        o_ref[:, cs] = acc


def _combine(y, gw, x1):
    t = x1.shape[0]
    tm = COMBINE_TM
    nt = t // tm
    row = lambda i: (i, 0)
    yspec = lambda k: pl.BlockSpec((tm * ROW_SUB, LANES), lambda i: (k * nt + i, 0))
    return pl.pallas_call(
        _combine_kernel,
        grid=(nt,),
        in_specs=[yspec(0), yspec(1), yspec(2), yspec(3), pl.BlockSpec((tm, LANES), row),
                  pl.BlockSpec((tm, D_MODEL), row)],
        out_specs=pl.BlockSpec((tm, D_MODEL), row),
        out_shape=jax.ShapeDtypeStruct((t, D_MODEL), F32),
        compiler_params=_cparams(("arbitrary",)),
        name="combine",
    )(y, y, y, y, gw, x1)


def _routing_plan(idx):
    t = idx.shape[0]
    a = t * TOP_K
    tok_bits = (t - 1).bit_length()
    key = ((idx << (tok_bits + 2)) | (jnp.arange(t, dtype=jnp.int32)[:, None] << 2)
           | jnp.arange(TOP_K, dtype=jnp.int32)[None, :])
    skey = jnp.sort(key.reshape(-1))
    bounds = jnp.searchsorted(skey, jnp.arange(N_EXPERTS + 1, dtype=jnp.int32) << (tok_bits + 2),
                              method='compare_all').astype(jnp.int32)
    starts, counts = bounds[:-1], bounds[1:] - bounds[:-1]
    padded = ((counts + EXP_BM - 1) // EXP_BM) * EXP_BM
    pends = jnp.cumsum(padded)
    pstarts = pends - padded
    n_blk = a // EXP_BM + N_EXPERTS
    blk = jnp.arange(n_blk, dtype=jnp.int32)
    block_e = jnp.minimum(jnp.sum((pends[None, :] <= (blk * EXP_BM)[:, None]).astype(jnp.int32), axis=1),
                          N_EXPERTS - 1)
    nused = (pends[-1] // EXP_BM).astype(jnp.int32).reshape(1)
    sel = block_e[:, None] == jnp.arange(N_EXPERTS, dtype=jnp.int32)[None, :]
    pick = lambda tab: jnp.sum(jnp.where(sel, tab[None, :], 0), axis=1)
    first = blk * EXP_BM - pick(pstarts)
    q = jnp.arange(EXP_BM, dtype=jnp.int32)[None, :]
    valid = q < (pick(counts) - first)[:, None]
    kv = skey[jnp.clip((pick(starts) + first)[:, None] + q, 0, a - 1)]
    tok = (kv >> 2) & ((1 << tok_bits) - 1)
    tok_rows = jnp.where(valid, tok, 0)
    dump = a + (blk % RING)[:, None] * EXP_BM + q
    dst_rows = jnp.where(valid, (kv & (TOP_K - 1)) * t + tok, dump)
    tok_next = jnp.concatenate([tok_rows[1:], tok_rows[-1:]], axis=0)
    tok_next2 = jnp.concatenate([tok_rows[2:], tok_rows[-1:], tok_rows[-1:]], axis=0)
    dst_prev = jnp.concatenate([a + (RING - 1) * EXP_BM + q, dst_rows[:-1]], axis=0)
    row_idx = jnp.concatenate([tok_rows, tok_next, tok_next2, dst_rows, dst_prev], axis=1)
    nxt = jnp.minimum(jnp.sum((block_e[None, :] <= block_e[:, None]).astype(jnp.int32), axis=1), n_blk - 1)
    next_e = jnp.sum(jnp.where(blk[None, :] == nxt[:, None], block_e[None, :], 0), axis=1)
    return block_e, next_e, nused, row_idx.reshape(n_blk, 1, N_LISTS * EXP_BM)


def kernel(x, norm_mix_g, w_in, b_gate, qn_a, kn_a, qn_b, kn_b, w_proj_a, w_proj_b, w_out,
           norm_ffn_g, w_router, b_router, w_gate_up, b_gate_up, w_down, b_down):
    b, s, d = x.shape
    t = b * s
    x2 = x.reshape(t, d)
    tabs_a, tabs_b = _rope_tables(s)
    d4, d16 = DIL_PAIRS[1][1], DIL_PAIRS[2][1]

    gq = jnp.tile(qn_a[:, None, :], (1, HEADS_A, 1)).reshape(1, WIDTH_A) * (SCALE * LOG2E)
    gk = jnp.tile(kn_a[:, None, :], (1, HEADS_A, 1)).reshape(1, WIDTH_A)
    gqk = jnp.concatenate([gq, gk], axis=1)
    gqb = jnp.tile(qn_b, HEADS_B_Q).reshape(1, WIDTH_B_Q) * (SCALE * LOG2E)
    gkb = jnp.tile(kn_b, HEADS_B_KV).reshape(1, WIDTH_B_KV)
    hid = np.arange(MXU_N) // HEAD_DIM
    bd = jnp.asarray(hid[:, None] == hid[None, :], dtype=BF16)

    qkv1, qkv2, qkv3, qb, kvb, gl = _inproj(
        x2, norm_mix_g.reshape(1, d), w_in.astype(BF16), bd, _perm_matrix(d4), _perm_matrix(d16),
        gqk, gqb, gkb, tabs_a, tabs_b, b, s)

    o1, l1 = _mixa(qkv1.reshape(b, s, 3 * OUT_A), s)
    o2, l2 = _mixa(qkv2.reshape(b * d4, s // d4, 3 * OUT_A), s // d4)
    o3, l3 = _mixa(qkv3.reshape(b * d16, s // d16, 3 * OUT_A), s // d16)
    ob = _mixb(qb, kvb, s, MIXB_TQ)

    pk = np.arange(LANES)
    col = np.arange(N_GROUPS * OUT_A)
    ex = jnp.asarray((pk[:, None] < 2 * N_GROUPS * HEADS_A)
                     & ((pk // (2 * HEADS_A))[:, None] == (col // OUT_A)[None, :])
                     & ((pk % HEADS_A)[:, None] == ((col % OUT_A) // HEAD_DIM)[None, :]), dtype=BF16)
    wr = jnp.zeros((d, LANES), F32).at[:, :N_EXPERTS].set(w_router).astype(BF16)
    br = jnp.full((1, LANES), NEG, F32).at[0, :N_EXPERTS].set(b_router)
    x1, h2, idx_full, gw = _merge(
        o1.reshape(t, OUT_A), o2.reshape(b, d4, s // d4, OUT_A), o3.reshape(b, d16, s // d16, OUT_A),
        l1.reshape(t, LANES), l2.reshape(b, d4, s // d4, LANES), l3.reshape(b, d16, s // d16, LANES),
        _perm_matrix(d4, True), _perm_matrix(d16, True), ob, gl, b_gate.reshape(1, 2 * d), x2, ex,
        w_proj_a.astype(BF16), w_proj_b.astype(BF16), w_out.astype(BF16), norm_ffn_g.reshape(1, d), wr, br, s)

    block_e, next_e, nused, row_idx = _routing_plan(idx_full[:, :TOP_K])
    n_rows = t * TOP_K + RING * EXP_BM
    y = _experts(block_e, next_e, nused, row_idx, h2.reshape(t, ROW_SUB, LANES), w_gate_up,
                 b_gate_up, w_down, b_down, n_rows)
    out = _combine(y.reshape(n_rows * ROW_SUB, LANES), gw, x1)
    return out.reshape(b, s, d)
```

```python
import functools

import jax
import jax.numpy as jnp
import numpy as np
from jax import lax
from jax.experimental import pallas as pl
from jax.experimental.pallas import tpu as pltpu

F32 = jnp.float32
BF16 = jnp.bfloat16

D_MODEL = 1024
HEAD_DIM = 64
SCALE = HEAD_DIM ** -0.5
EPS = 1e-6
NEG = -1e30
DIL_PAIRS = ((128, 1), (512, 4), (2048, 16))
HALF_WIN = 64
N_GROUPS = 3
HEADS_A = 8
OUT_A = HEADS_A * HEAD_DIM
WIDTH_A = N_GROUPS * OUT_A
ROT_DIM_A = 16
THETA_PARTIAL = 500000.0
HEADS_B_Q = 8
HEADS_B_KV = 2
WIDTH_B_Q = 512
WIDTH_B_KV = 128
GRID_W = 64
THETA_AXIAL = 10000.0
N_EXPERTS = 32
TOP_K = 4
D_FF = 1024
SWIGLU_LIMIT = 7.0
SWIGLU_ALPHA = 1.702

LANES = 128
MXU_N = 256
VMEM_LIMIT = 56 * 1024 * 1024
TILE = 256
INPROJ_TM = 512
MERGE_TM = 512
COMBINE_TM = 512
MIXB_TQ = 1024
ROW_SUB = D_MODEL // LANES

COL_QB = 3 * WIDTH_A
COL_GATE = COL_QB + WIDTH_B_Q + 2 * WIDTH_B_KV


def _cparams(sem):
    return pltpu.CompilerParams(dimension_semantics=sem, vmem_limit_bytes=VMEM_LIMIT)


def _perm_rows(dil):
    n = np.arange(TILE)
    per = TILE // dil
    return (n % per) * dil + n // per


def _perm_matrix(dil, transpose=False):
    p = np.zeros((TILE, TILE), np.float32)
    p[np.arange(TILE), _perm_rows(dil)] = 1.0
    return jnp.asarray(p.T if transpose else p, dtype=BF16)


def _rope_tables(seq):
    pos = np.arange(seq, dtype=np.float64)[:, None]
    d = np.arange(LANES) % HEAD_DIM
    half = ROT_DIM_A // 2
    inv = THETA_PARTIAL ** (-((d % half) / half))
    ang = pos * inv[None, :]
    in_rot = (d < ROT_DIM_A)[None, :]
    first = (d < half)[None, :]
    second = ((d >= half) & (d < ROT_DIM_A))[None, :]
    ta = np.stack([np.where(in_rot, np.cos(ang), 1.0), np.where(first, -np.sin(ang), 0.0),
                   np.where(second, np.sin(ang), 0.0)])
    per_group = []
    for _, dil in DIL_PAIRS:
        order = (np.arange(seq // TILE) * TILE)[:, None] + _perm_rows(dil)[None, :]
        per_group.append(ta[:, order.reshape(-1), :])
    hb = HEAD_DIM // 4
    row = np.floor(pos / GRID_W)
    col = pos - row * GRID_W
    invb = THETA_AXIAL ** (-((d % hb) / hb))
    angb = np.where((d < HEAD_DIM // 2)[None, :], row, col) * invb[None, :]
    firstb = ((d % (2 * hb)) < hb)[None, :]
    tb = np.stack([np.cos(angb), np.where(firstb, -np.sin(angb), 0.0), np.where(firstb, 0.0, np.sin(angb))])
    return jnp.asarray(np.stack(per_group), F32), jnp.asarray(tb, F32)


def _inproj_kernel(x_ref, g_ref, w_ref, bd_ref, p4_ref, p16_ref, gqk_ref, gqb_ref, gkb_ref, ta_ref, tb_ref,
                   qkv1_ref, qkv2_ref, qkv3_ref, qb_ref, kvb_ref, gate_ref):
    x = x_ref[...]
    tm = x.shape[0]
    nsub = tm // TILE
    ms = jnp.mean(x * x, axis=-1, keepdims=True)
    h = (x * lax.rsqrt(ms + EPS) * g_ref[...]).astype(BF16)

    def regroup(p_ref):
        return jnp.concatenate(
            [jnp.dot(p_ref[...], h[s * TILE:(s + 1) * TILE], preferred_element_type=F32) for s in range(nsub)],
            axis=0).astype(BF16)

    hg = (h, regroup(p4_ref), regroup(p16_ref))
    bd = bd_ref[...]
    lane = lax.broadcasted_iota(jnp.int32, (tm, LANES), 1)
    lo = lane < HEAD_DIM

    def proj(lhs, c0, width=MXU_N):
        return jnp.dot(lhs, w_ref[:, c0:c0 + width], preferred_element_type=F32)

    def head_norm(y, gain, bdm):
        ss = jnp.dot((y * y).astype(BF16), bdm, preferred_element_type=F32)
        return y * lax.rsqrt(ss * (1.0 / HEAD_DIM) + EPS) * gain

    def rope(z, tab, sh):
        return z * tab[0] + pltpu.roll(z, LANES - sh, 1) * tab[1] + pltpu.roll(z, sh, 1) * tab[2]

    sh_a = ROT_DIM_A // 2
    sh_b = HEAD_DIM // 4

    def store_group(gi, c_out, val):
        if gi == 0:
            qkv1_ref[:, c_out:c_out + LANES] = val
        else:
            ref, dil = ((qkv2_ref, DIL_PAIRS[1][1]), (qkv3_ref, DIL_PAIRS[2][1]))[gi - 1]
            per = TILE // dil
            for s in range(nsub):
                ref[0, :, s * per:(s + 1) * per, c_out:c_out + LANES] = (
                    val[s * TILE:(s + 1) * TILE].reshape(dil, per, LANES))

    for gi in range(N_GROUPS):
        tab = (ta_ref[gi, 0], ta_ref[gi, 1], ta_ref[gi, 2])
        for which in range(3):
            wide = proj(hg[gi], which * WIDTH_A + gi * OUT_A, OUT_A)
            for cc in range(OUT_A // MXU_N):
                c_in = which * WIDTH_A + gi * OUT_A + cc * MXU_N
                y = wide[:, cc * MXU_N:(cc + 1) * MXU_N]
                if which < 2:
                    y = head_norm(y, gqk_ref[:, c_in:c_in + MXU_N], bd)
                for hf in range(2):
                    z = y[:, hf * LANES:(hf + 1) * LANES]
                    if which < 2:
                        z = rope(z, tab, sh_a)
                    store_group(gi, which * OUT_A + cc * MXU_N + hf * LANES, z.astype(BF16))

    tabb = (tb_ref[0], tb_ref[1], tb_ref[2])
    wide_b = proj(h, COL_QB, WIDTH_B_Q + 2 * WIDTH_B_KV)
    for c in range(WIDTH_B_Q // MXU_N):
        c0 = c * MXU_N
        yn = head_norm(wide_b[:, c0:c0 + MXU_N], gqb_ref[:, c0:c0 + MXU_N], bd)
        for hf in range(2):
            z = yn[:, hf * LANES:(hf + 1) * LANES]
            qb_ref[:, c0 + hf * LANES:c0 + (hf + 1) * LANES] = rope(z, tabb, sh_b).astype(BF16)
    ykv = wide_b[:, WIDTH_B_Q:]
    kb = rope(head_norm(ykv[:, :LANES], gkb_ref[...], bd[:LANES, :LANES]), tabb, sh_b)
    vb = ykv[:, LANES:]
    for j, t in enumerate((kb, vb)):
        sw = pltpu.roll(t, HEAD_DIM, 1)
        kvb_ref[:, (2 * j) * LANES:(2 * j + 1) * LANES] = jnp.where(lo, t, sw).astype(BF16)
        kvb_ref[:, (2 * j + 1) * LANES:(2 * j + 2) * LANES] = jnp.where(lo, sw, t).astype(BF16)
    gate_ref[...] = proj(h, COL_GATE, 2 * D_MODEL).astype(BF16)


def _inproj(x2, g, w_bf, bd, p4, p16, gqk, gqb, gkb, tabs_a, tabs_b, batch, seq):
    t = x2.shape[0]
    n_cols = w_bf.shape[1]
    tm = INPROJ_TM
    npos = seq // tm
    row = lambda i: (i, 0)
    fixed = lambda i: (0, 0)
    d4, d16 = DIL_PAIRS[1][1], DIL_PAIRS[2][1]
    sub = lambda i: (i // npos, 0, i % npos, 0)
    return pl.pallas_call(
        _inproj_kernel,
        grid=(t // tm,),
        in_specs=[
            pl.BlockSpec((tm, D_MODEL), row),
            pl.BlockSpec((1, D_MODEL), fixed),
            pl.BlockSpec((D_MODEL, n_cols), fixed, pipeline_mode=pl.Buffered(1)),
            pl.BlockSpec((MXU_N, MXU_N), fixed),
            pl.BlockSpec((TILE, TILE), fixed),
            pl.BlockSpec((TILE, TILE), fixed),
            pl.BlockSpec((1, 2 * WIDTH_A), fixed),
            pl.BlockSpec((1, WIDTH_B_Q), fixed),
            pl.BlockSpec((1, WIDTH_B_KV), fixed),
            pl.BlockSpec((N_GROUPS, 3, tm, LANES), lambda i: (0, 0, i % npos, 0)),
            pl.BlockSpec((3, tm, LANES), lambda i: (0, i % npos, 0)),
        ],
        out_specs=[
            pl.BlockSpec((tm, 3 * OUT_A), row),
            pl.BlockSpec((1, d4, tm // d4, 3 * OUT_A), sub),
            pl.BlockSpec((1, d16, tm // d16, 3 * OUT_A), sub),
            pl.BlockSpec((tm, WIDTH_B_Q), row),
            pl.BlockSpec((tm, 4 * LANES), row),
            pl.BlockSpec((tm, 2 * D_MODEL), row),
        ],
        out_shape=[
            jax.ShapeDtypeStruct((t, 3 * OUT_A), BF16),
            jax.ShapeDtypeStruct((batch, d4, seq // d4, 3 * OUT_A), BF16),
            jax.ShapeDtypeStruct((batch, d16, seq // d16, 3 * OUT_A), BF16),
            jax.ShapeDtypeStruct((t, WIDTH_B_Q), BF16),
            jax.ShapeDtypeStruct((t, 4 * LANES), BF16),
            jax.ShapeDtypeStruct((t, 2 * D_MODEL), BF16),
        ],
        compiler_params=_cparams(("arbitrary",)),
        name="inproj",
    )(x2, g, w_bf, bd, p4, p16, gqk, gqb, gkb, tabs_a, tabs_b)


QBLK = 128
NSUB = 8


def _mixa_kernel(q_ref, k_ref, v_ref, o_ref, lse_ref, *, seq_len, win, seqs, nblk):
    step = pl.program_id(1)
    lane = lax.broadcasted_iota(jnp.int32, (QBLK, LANES), 1)
    lo = lane < HEAD_DIM
    qi = lax.broadcasted_iota(jnp.int32, (QBLK, win), 0)
    ki = lax.broadcasted_iota(jnp.int32, (QBLK, win), 1)
    for b in range(seqs * nblk):
        sq, r0 = b // nblk, (b % nblk) * QBLK
        if seq_len == win:
            blk, kstart = 0, 0
        else:
            blk = step * nblk + b % nblk
            kstart = pl.multiple_of(jnp.clip(blk * QBLK - HALF_WIN, 0, seq_len - win), HALF_WIN)
        valid = jnp.abs((ki + kstart) - (qi + blk * QBLK)) <= HALF_WIN
        scores, vals = [], []
        for p in range(HEADS_A // 2):
            cs = slice(p * LANES, (p + 1) * LANES)
            qp = q_ref[sq, r0:r0 + QBLK, cs]
            kp = k_ref[sq, pl.ds(kstart, win), cs]
            vals.append(v_ref[sq, pl.ds(kstart, win), cs])
            for hh in range(2):
                qh = jnp.where(lo if hh == 0 else jnp.logical_not(lo), qp, jnp.zeros_like(qp))
                scores.append(lax.dot_general(qh, kp, (((1,), (1,)), ((), ())), preferred_element_type=F32))
        s = jnp.where(valid[None], jnp.stack(scores), NEG)
        m = jnp.max(s, axis=-1, keepdims=True)
        e = jnp.exp(s - m)
        l = jnp.sum(e, axis=-1, keepdims=True)
        rl = 1.0 / l
        lse = m + jnp.log(l)
        eb = e.astype(BF16)
        lse_acc = jnp.zeros((QBLK, LANES), F32)
        for p in range(HEADS_A // 2):
            outs = [jnp.dot(eb[2 * p + hh], vals[p], preferred_element_type=F32) * rl[2 * p + hh] for hh in range(2)]
            o_ref[sq, r0:r0 + QBLK, p * LANES:(p + 1) * LANES] = jnp.where(lo, outs[0], outs[1]).astype(BF16)
            for hh in range(2):
                lse_acc = jnp.where(lane == 2 * p + hh, lse[2 * p + hh], lse_acc)
        lse_ref[sq, r0:r0 + QBLK, :] = lse_acc


def _mixa(qkv, seq_len):
    n_seq = qkv.shape[0]
    win = min(2 * QBLK, seq_len)
    nblk = min(NSUB, seq_len // QBLK)
    seqs = NSUB // nblk
    rows = nblk * QBLK
    grid = (n_seq // seqs, seq_len // rows)
    qspec = lambda c: pl.BlockSpec((seqs, rows, OUT_A), lambda s, i: (s, i, c))
    kspec = lambda c: pl.BlockSpec((seqs, seq_len, OUT_A), lambda s, i: (s, 0, c))
    ospec = pl.BlockSpec((seqs, rows, OUT_A), lambda s, i: (s, i, 0))
    lspec = pl.BlockSpec((seqs, rows, LANES), lambda s, i: (s, i, 0))
    return pl.pallas_call(
        functools.partial(_mixa_kernel, seq_len=seq_len, win=win, seqs=seqs, nblk=nblk),
        grid=grid,
        in_specs=[qspec(0), kspec(1), kspec(2)],
        out_specs=[ospec, lspec],
        out_shape=[
            jax.ShapeDtypeStruct((n_seq, seq_len, OUT_A), BF16),
            jax.ShapeDtypeStruct((n_seq, seq_len, LANES), F32),
        ],
        compiler_params=_cparams(("arbitrary", "arbitrary")),
        name=f"mixa_len{seq_len}",
    )(qkv, qkv, qkv)


def _mixb_kernel(q_ref, k_ref, v_ref, o_ref):
    tq = q_ref.shape[0]
    lane = lax.broadcasted_iota(jnp.int32, (tq, LANES), 1)
    lo = lane < HEAD_DIM
    for p in range(HEADS_B_Q // 2):
        j = p // 2
        cs = slice(p * LANES, (p + 1) * LANES)
        qp = q_ref[:, cs]
        kd = k_ref[:, j * LANES:(j + 1) * LANES]
        vd = v_ref[:, j * LANES:(j + 1) * LANES]
        outs = []
        for hh in range(2):
            qh = jnp.where(lo if hh == 0 else jnp.logical_not(lo), qp, jnp.zeros_like(qp))
            s = lax.dot_general(qh, kd, (((1,), (1,)), ((), ())), preferred_element_type=F32)
            m = jnp.max(s, axis=-1, keepdims=True)
            e = jnp.exp(s - m)
            l = jnp.sum(e, axis=-1, keepdims=True)
            outs.append(jnp.dot(e.astype(BF16), vd, preferred_element_type=F32) * (1.0 / l))
        o_ref[:, cs] = jnp.where(lo, outs[0], outs[1]).astype(BF16)


def _mixb(qb, kvb, seq, tq):
    t = qb.shape[0]
    nq = seq // tq
    return pl.pallas_call(
        _mixb_kernel,
        grid=(t // seq, nq),
        in_specs=[
            pl.BlockSpec((tq, WIDTH_B_Q), lambda b, i: (b * nq + i, 0)),
            pl.BlockSpec((seq, 2 * LANES), lambda b, i: (b, 0)),
            pl.BlockSpec((seq, 2 * LANES), lambda b, i: (b, 1)),
        ],
        out_specs=pl.BlockSpec((tq, WIDTH_B_Q), lambda b, i: (b * nq + i, 0)),
        out_shape=jax.ShapeDtypeStruct((t, WIDTH_B_Q), BF16),
        compiler_params=_cparams(("arbitrary", "arbitrary")),
        name="mixb",
    )(qb, kvb, kvb)


def _merge_kernel(o1_ref, o2_ref, o3_ref, l1_ref, l2_ref, l3_ref, p4t_ref, p16t_ref, ob_ref, gl_ref, bg_ref,
                  x_ref, ex_ref, wpa_ref, wpb_ref, wo_ref, g2_ref, wr_ref, br_ref,
                  x1_ref, h2_ref, idx_ref, gw_ref):
    tm = x_ref.shape[0]
    nsub = tm // TILE
    lane = lax.broadcasted_iota(jnp.int32, (tm, LANES), 1)

    def split(w):
        hi = w.astype(BF16)
        return hi, (w - hi.astype(F32)).astype(BF16)

    def unperm(pt, ref):
        dil = ref.shape[1]
        per = TILE // dil
        outs = []
        for s in range(nsub):
            val = ref[0, :, s * per:(s + 1) * per, :].reshape(TILE, ref.shape[3])
            if val.dtype == BF16:
                outs.append(jnp.dot(pt, val, preferred_element_type=F32))
            else:
                hi, lo_ = split(val)
                outs.append(jnp.dot(pt, hi, preferred_element_type=F32) + jnp.dot(pt, lo_, preferred_element_type=F32))
        return jnp.concatenate(outs, axis=0)

    p4t, p16t = p4t_ref[...], p16t_ref[...]
    o1 = o1_ref[...].astype(F32)
    o2 = unperm(p4t, o2_ref)
    o3 = unperm(p16t, o3_ref)
    l1 = l1_ref[...]
    l2 = unperm(p4t, l2_ref)
    l3 = unperm(p16t, l3_ref)
    mx = jnp.maximum(jnp.maximum(l1, l2), l3)
    e1, e2, e3 = jnp.exp(l1 - mx), jnp.exp(l2 - mx), jnp.exp(l3 - mx)
    rden = 1.0 / (e1 + e2 + e3)
    packed = jnp.zeros((tm, LANES), F32)
    for g, e in enumerate((e1, e2, e3)):
        hi, lo_ = split(jnp.where(lane < HEADS_A, e * rden, 0.0))
        for part, v in enumerate((hi, lo_)):
            shift = (2 * g + part) * HEADS_A
            vf = v.astype(F32)
            packed = packed + (vf if shift == 0 else pltpu.roll(vf, shift, 1))
    wexp = jnp.dot(packed.astype(BF16), ex_ref[...], preferred_element_type=F32)
    oa = wexp[:, :OUT_A] * o1 + wexp[:, OUT_A:2 * OUT_A] * o2 + wexp[:, 2 * OUT_A:] * o3
    pa = jnp.dot(oa.astype(BF16), wpa_ref[...], preferred_element_type=F32)
    pb = jnp.dot(ob_ref[...], wpb_ref[...], preferred_element_type=F32)
    gl = gl_ref[...].astype(F32) + bg_ref[...]
    gates = 1.0 / (1.0 + jnp.exp(-gl))
    merged = gates[:, :D_MODEL] * pa + gates[:, D_MODEL:] * pb
    x1 = x_ref[...] + jnp.dot(merged.astype(BF16), wo_ref[...], preferred_element_type=F32)
    x1_ref[...] = x1
    ms = jnp.mean(x1 * x1, axis=-1, keepdims=True)
    h2 = x1 * lax.rsqrt(ms + EPS) * g2_ref[...]
    for c in range(D_MODEL // LANES):
        h2_ref[pl.ds(c, tm, stride=ROW_SUB), :] = h2[:, c * LANES:(c + 1) * LANES]
    work = jnp.dot(h2.astype(BF16), wr_ref[...], preferred_element_type=F32) + br_ref[...]
    lane_f = lane.astype(F32)
    vals, idxs = [], []
    for _ in range(TOP_K):
        m = jnp.max(work, axis=-1, keepdims=True)
        ix = jnp.min(jnp.where(work == m, lane_f, float(LANES)), axis=-1, keepdims=True)
        vals.append(m)
        idxs.append(ix)
        work = jnp.where(lane_f == ix, -jnp.inf, work)
    es = [jnp.exp(v - vals[0]) for v in vals]
    rsum = 1.0 / (es[0] + es[1] + es[2] + es[3])
    idx_out = jnp.zeros((tm, LANES), F32)
    gw_out = jnp.zeros((tm, LANES), F32)
    for k in range(TOP_K):
        idx_out = jnp.where(lane == k, idxs[k], idx_out)
        gw_out = jnp.where(lane == k, es[k] * rsum, gw_out)
    idx_ref[...] = idx_out.astype(jnp.int32)
    gw_ref[...] = gw_out


def _merge(o1, o2, o3, l1, l2, l3, p4t, p16t, ob, gl, bg, x2, ex, wpa, wpb, wo, g2, wr, br, seq):
    t = x2.shape[0]
    tm = MERGE_TM
    npos = seq // tm
    row = lambda i: (i, 0)
    fixed = lambda i: (0, 0)
    d4, d16 = DIL_PAIRS[1][1], DIL_PAIRS[2][1]
    sub = lambda i: (i // npos, 0, i % npos, 0)
    return pl.pallas_call(
        _merge_kernel,
        grid=(t // tm,),
        in_specs=[
            pl.BlockSpec((tm, OUT_A), row),
            pl.BlockSpec((1, d4, tm // d4, OUT_A), sub),
            pl.BlockSpec((1, d16, tm // d16, OUT_A), sub),
            pl.BlockSpec((tm, LANES), row),
            pl.BlockSpec((1, d4, tm // d4, LANES), sub),
            pl.BlockSpec((1, d16, tm // d16, LANES), sub),
            pl.BlockSpec((TILE, TILE), fixed),
            pl.BlockSpec((TILE, TILE), fixed),
            pl.BlockSpec((tm, WIDTH_B_Q), row),
            pl.BlockSpec((tm, 2 * D_MODEL), row),
            pl.BlockSpec((1, 2 * D_MODEL), fixed),
            pl.BlockSpec((tm, D_MODEL), row),
            pl.BlockSpec((LANES, N_GROUPS * OUT_A), fixed),
            pl.BlockSpec((OUT_A, D_MODEL), fixed),
            pl.BlockSpec((WIDTH_B_Q, D_MODEL), fixed),
            pl.BlockSpec((D_MODEL, D_MODEL), fixed),
            pl.BlockSpec((1, D_MODEL), fixed),
            pl.BlockSpec((D_MODEL, LANES), fixed),
            pl.BlockSpec((1, LANES), fixed),
        ],
        out_specs=[
            pl.BlockSpec((tm, D_MODEL), row), pl.BlockSpec((tm * ROW_SUB, LANES), row),
            pl.BlockSpec((tm, LANES), row), pl.BlockSpec((tm, LANES), row),
        ],
        out_shape=[
            jax.ShapeDtypeStruct((t, D_MODEL), F32),
            jax.ShapeDtypeStruct((t * ROW_SUB, LANES), F32),
            jax.ShapeDtypeStruct((t, LANES), jnp.int32),
            jax.ShapeDtypeStruct((t, LANES), F32),
        ],
        compiler_params=_cparams(("arbitrary",)),
        name="merge",
    )(o1, o2, o3, l1, l2, l3, p4t, p16t, ob, gl, bg, x2, ex, wpa, wpb, wo, g2, wr, br)


EXP_BM = 256


W_CHUNK = 256


RING = 3
N_LISTS = 5


def _experts_kernel(be_ref, nxt_ref, nused_ref, idx_ref,
                    h2_hbm, wgu_hbm, bgu_ref, wd_hbm, bd_ref,
                    y_hbm, xbuf, ybuf, wgu_f32, wd_f32, wgu_bf, wd_bf, gsem, ssem, wsem):
    i = pl.program_id(0)
    nused = nused_ref[0]

    def fetch_weights(e):
        return (pltpu.make_async_copy(wgu_hbm.at[e], wgu_f32, wsem.at[0]),
                pltpu.make_async_copy(wd_hbm.at[e], wd_f32, wsem.at[1]))
    n_real = y_hbm.shape[0] - RING * EXP_BM
    tok_list = lambda k, r: idx_ref[0, 0, k * EXP_BM + r]
    dst_cur = lambda r: idx_ref[0, 0, 3 * EXP_BM + r]
    dst_prev = lambda r: idx_ref[0, 0, 4 * EXP_BM + r]

    def gather(tok, s, r):
        return pltpu.make_async_copy(h2_hbm.at[tok], xbuf.at[s, pl.ds(r * ROW_SUB, ROW_SUB)], gsem.at[s])

    def scatter(s, dst, r):
        return pltpu.make_async_copy(ybuf.at[s, pl.ds(r * ROW_SUB, ROW_SUB)], y_hbm.at[dst], ssem.at[s])

    @pl.when(i == 0)
    def _():
        ybuf[...] = jnp.zeros_like(ybuf)
        for r in range(EXP_BM):
            gather(tok_list(0, r), 0, r).start()
            gather(tok_list(1, r), 1, r).start()
            scatter(0, n_real + r, r).start()
            scatter(1, n_real + EXP_BM + r, r).start()
        for c in fetch_weights(be_ref[0]):
            c.start()

    @pl.when(jnp.logical_and(i < nused, jnp.logical_or(i == 0, be_ref[i] != be_ref[jnp.maximum(i - 1, 0)])))
    def _():
        for c in fetch_weights(0):
            c.wait()
        for c in range(2 * D_FF // W_CHUNK):
            cs = slice(c * W_CHUNK, (c + 1) * W_CHUNK)
            wgu_bf[:, cs] = wgu_f32[:, cs].astype(BF16)
        for c in range(D_MODEL // W_CHUNK):
            cs = slice(c * W_CHUNK, (c + 1) * W_CHUNK)
            wd_bf[:, cs] = wd_f32[:, cs].astype(BF16)
        for c in fetch_weights(nxt_ref[i]):
            c.start()

    def block(slot):
        prev = (slot - 1) % RING
        ahead = (slot + 2) % RING
        for r in range(EXP_BM):
            gather(0, slot, r).wait()
        xb = jnp.concatenate([xbuf[slot, pl.ds(c, EXP_BM, stride=ROW_SUB), :] for c in range(ROW_SUB)],
                             axis=1).astype(BF16)
        g = jnp.dot(xb, wgu_bf[:, :D_FF], preferred_element_type=F32) + bgu_ref[0, :, :D_FF]
        u = jnp.dot(xb, wgu_bf[:, D_FF:], preferred_element_type=F32) + bgu_ref[0, :, D_FF:]
        gate = jnp.minimum(g, SWIGLU_LIMIT)
        up = jnp.clip(u, -SWIGLU_LIMIT, SWIGLU_LIMIT)
        act = (up + 1.0) * (gate * (1.0 / (1.0 + jnp.exp(-SWIGLU_ALPHA * gate))))
        y = jnp.dot(act.astype(BF16), wd_bf[...], preferred_element_type=F32) + bd_ref[0]
        for r in range(EXP_BM):
            scatter(prev, dst_prev(r), r).start()
            gather(tok_list(2, r), ahead, r).start()

        for r in range(EXP_BM):
            scatter(slot, 0, r).wait()
        for c in range(ROW_SUB):
            ybuf[slot, pl.ds(c, EXP_BM, stride=ROW_SUB), :] = y[:, c * LANES:(c + 1) * LANES]

        @pl.when(i == nused - 1)
        def _():
            for r in range(EXP_BM):
                scatter(slot, dst_cur(r), r).start()
            for r in range(EXP_BM):
                for s in range(RING):
                    scatter(s, 0, r).wait()
                gather(0, (slot + 1) % RING, r).wait()
                gather(0, ahead, r).wait()
            for c in fetch_weights(0):
                c.wait()

    for phase in range(RING):
        pl.when(jnp.logical_and(i < nused, i % RING == phase))(functools.partial(block, phase))


def _experts(block_e, next_e, nused, row_idx, h2, wgu, bgu, wd, bd, n_out_rows):
    n_blk = block_e.shape[0]
    d = D_MODEL
    grid_spec = pltpu.PrefetchScalarGridSpec(
        num_scalar_prefetch=3,
        grid=(n_blk,),
        in_specs=[
            pl.BlockSpec((1, 1, N_LISTS * EXP_BM), lambda i, be, nx, nu: (i, 0, 0), memory_space=pltpu.SMEM),
            pl.BlockSpec(memory_space=pl.ANY),
            pl.BlockSpec(memory_space=pl.ANY),
            pl.BlockSpec((1, 1, 2 * D_FF), lambda i, be, nx, nu: (be[i], 0, 0)),
            pl.BlockSpec(memory_space=pl.ANY),
            pl.BlockSpec((1, 1, d), lambda i, be, nx, nu: (be[i], 0, 0)),
        ],
        out_specs=pl.BlockSpec(memory_space=pl.ANY),
        scratch_shapes=[
            pltpu.VMEM((RING, EXP_BM * ROW_SUB, LANES), F32),
            pltpu.VMEM((RING, EXP_BM * ROW_SUB, LANES), F32),
            pltpu.VMEM((d, 2 * D_FF), F32),
            pltpu.VMEM((D_FF, d), F32),
            pltpu.VMEM((d, 2 * D_FF), BF16),
            pltpu.VMEM((D_FF, d), BF16),
            pltpu.SemaphoreType.DMA((RING,)),
            pltpu.SemaphoreType.DMA((RING,)),
            pltpu.SemaphoreType.DMA((2,)),
        ],
    )
    return pl.pallas_call(
        _experts_kernel,
        grid_spec=grid_spec,
        out_shape=jax.ShapeDtypeStruct((n_out_rows, ROW_SUB, LANES), F32),
        compiler_params=_cparams(("arbitrary",)),
        name="experts",
    )(block_e, next_e, nused, row_idx, h2, wgu, bgu, wd, bd)


def _combine_kernel(y0_ref, y1_ref, y2_ref, y3_ref, gw_ref, x1_ref, o_ref):
    gw = gw_ref[...]
    tm = gw.shape[0]
    for c in range(ROW_SUB):
        cs = slice(c * LANES, (c + 1) * LANES)
        acc = x1_ref[:, cs]
        for k, y_ref in enumerate((y0_ref, y1_ref, y2_ref, y3_ref)):
            acc = acc + gw[:, k:k + 1] * y_ref[pl.ds(c, tm, stride=ROW_SUB), :]
        o_ref[:, cs] = acc


def _combine(y, gw, x1):
    t = x1.shape[0]
    tm = COMBINE_TM
    nt = t // tm
    row = lambda i: (i, 0)
    yspec = lambda k: pl.BlockSpec((tm * ROW_SUB, LANES), lambda i: (k * nt + i, 0))
    return pl.pallas_call(
        _combine_kernel,
        grid=(nt,),
        in_specs=[yspec(0), yspec(1), yspec(2), yspec(3), pl.BlockSpec((tm, LANES), row),
                  pl.BlockSpec((tm, D_MODEL), row)],
        out_specs=pl.BlockSpec((tm, D_MODEL), row),
        out_shape=jax.ShapeDtypeStruct((t, D_MODEL), F32),
        compiler_params=_cparams(("arbitrary",)),
        name="combine",
    )(y, y, y, y, gw, x1)


def _routing_plan(idx):
    t = idx.shape[0]
    a = t * TOP_K
    tok_bits = (t - 1).bit_length()
    key = ((idx << (tok_bits + 2)) | (jnp.arange(t, dtype=jnp.int32)[:, None] << 2)
           | jnp.arange(TOP_K, dtype=jnp.int32)[None, :])
    skey = jnp.sort(key.reshape(-1))
    bounds = jnp.searchsorted(skey, jnp.arange(N_EXPERTS + 1, dtype=jnp.int32) << (tok_bits + 2),
                              method='compare_all').astype(jnp.int32)
    starts, counts = bounds[:-1], bounds[1:] - bounds[:-1]
    padded = ((counts + EXP_BM - 1) // EXP_BM) * EXP_BM
    pends = jnp.cumsum(padded)
    pstarts = pends - padded
    n_blk = a // EXP_BM + N_EXPERTS
    blk = jnp.arange(n_blk, dtype=jnp.int32)
    block_e = jnp.minimum(jnp.sum((pends[None, :] <= (blk * EXP_BM)[:, None]).astype(jnp.int32), axis=1),
                          N_EXPERTS - 1)
    nused = (pends[-1] // EXP_BM).astype(jnp.int32).reshape(1)
    sel = block_e[:, None] == jnp.arange(N_EXPERTS, dtype=jnp.int32)[None, :]
    pick = lambda tab: jnp.sum(jnp.where(sel, tab[None, :], 0), axis=1)
    first = blk * EXP_BM - pick(pstarts)
    q = jnp.arange(EXP_BM, dtype=jnp.int32)[None, :]
    valid = q < (pick(counts) - first)[:, None]
    kv = skey[jnp.clip((pick(starts) + first)[:, None] + q, 0, a - 1)]
    tok = (kv >> 2) & ((1 << tok_bits) - 1)
    tok_rows = jnp.where(valid, tok, 0)
    dump = a + (blk % RING)[:, None] * EXP_BM + q
    dst_rows = jnp.where(valid, (kv & (TOP_K - 1)) * t + tok, dump)
    tok_next = jnp.concatenate([tok_rows[1:], tok_rows[-1:]], axis=0)
    tok_next2 = jnp.concatenate([tok_rows[2:], tok_rows[-1:], tok_rows[-1:]], axis=0)
    dst_prev = jnp.concatenate([a + (RING - 1) * EXP_BM + q, dst_rows[:-1]], axis=0)
    row_idx = jnp.concatenate([tok_rows, tok_next, tok_next2, dst_rows, dst_prev], axis=1)
    nxt = jnp.minimum(jnp.sum((block_e[None, :] <= block_e[:, None]).astype(jnp.int32), axis=1), n_blk - 1)
    next_e = jnp.sum(jnp.where(blk[None, :] == nxt[:, None], block_e[None, :], 0), axis=1)
    return block_e, next_e, nused, row_idx.reshape(n_blk, 1, N_LISTS * EXP_BM)


def kernel(x, norm_mix_g, w_in, b_gate, qn_a, kn_a, qn_b, kn_b, w_proj_a, w_proj_b, w_out,
           norm_ffn_g, w_router, b_router, w_gate_up, b_gate_up, w_down, b_down):
    b, s, d = x.shape
    t = b * s
    x2 = x.reshape(t, d)
    tabs_a, tabs_b = _rope_tables(s)
    d4, d16 = DIL_PAIRS[1][1], DIL_PAIRS[2][1]

    gq = jnp.tile(qn_a[:, None, :], (1, HEADS_A, 1)).reshape(1, WIDTH_A) * SCALE
    gk = jnp.tile(kn_a[:, None, :], (1, HEADS_A, 1)).reshape(1, WIDTH_A)
    gqk = jnp.concatenate([gq, gk], axis=1)
    gqb = jnp.tile(qn_b, HEADS_B_Q).reshape(1, WIDTH_B_Q) * SCALE
    gkb = jnp.tile(kn_b, HEADS_B_KV).reshape(1, WIDTH_B_KV)
    hid = np.arange(MXU_N) // HEAD_DIM
    bd = jnp.asarray(hid[:, None] == hid[None, :], dtype=BF16)

    qkv1, qkv2, qkv3, qb, kvb, gl = _inproj(
        x2, norm_mix_g.reshape(1, d), w_in.astype(BF16), bd, _perm_matrix(d4), _perm_matrix(d16),
        gqk, gqb, gkb, tabs_a, tabs_b, b, s)

    o1, l1 = _mixa(qkv1.reshape(b, s, 3 * OUT_A), s)
    o2, l2 = _mixa(qkv2.reshape(b * d4, s // d4, 3 * OUT_A), s // d4)
    o3, l3 = _mixa(qkv3.reshape(b * d16, s // d16, 3 * OUT_A), s // d16)
    ob = _mixb(qb, kvb, s, MIXB_TQ)

    pk = np.arange(LANES)
    col = np.arange(N_GROUPS * OUT_A)
    ex = jnp.asarray((pk[:, None] < 2 * N_GROUPS * HEADS_A)
                     & ((pk // (2 * HEADS_A))[:, None] == (col // OUT_A)[None, :])
                     & ((pk % HEADS_A)[:, None] == ((col % OUT_A) // HEAD_DIM)[None, :]), dtype=BF16)
    wr = jnp.zeros((d, LANES), F32).at[:, :N_EXPERTS].set(w_router).astype(BF16)
    br = jnp.full((1, LANES), NEG, F32).at[0, :N_EXPERTS].set(b_router)
    x1, h2, idx_full, gw = _merge(
        o1.reshape(t, OUT_A), o2.reshape(b, d4, s // d4, OUT_A), o3.reshape(b, d16, s // d16, OUT_A),
        l1.reshape(t, LANES), l2.reshape(b, d4, s // d4, LANES), l3.reshape(b, d16, s // d16, LANES),
        _perm_matrix(d4, True), _perm_matrix(d16, True), ob, gl, b_gate.reshape(1, 2 * d), x2, ex,
        w_proj_a.astype(BF16), w_proj_b.astype(BF16), w_out.astype(BF16), norm_ffn_g.reshape(1, d), wr, br, s)

    block_e, next_e, nused, row_idx = _routing_plan(idx_full[:, :TOP_K])
    n_rows = t * TOP_K + RING * EXP_BM
    y = _experts(block_e, next_e, nused, row_idx, h2.reshape(t, ROW_SUB, LANES), w_gate_up,
                 b_gate_up.reshape(N_EXPERTS, 1, 2 * D_FF), w_down, b_down.reshape(N_EXPERTS, 1, d), n_rows)
    out = _combine(y.reshape(n_rows * ROW_SUB, LANES), gw, x1)
    return out.reshape(b, s, d)
```
